```python
import math
import jax, jax.numpy as jnp
from jax import lax
import numpy as np

D_MODEL = 1024
BATCH = 8
SEQ = 8192
DEPTH = 4
DEC_BATCH = 1
DEC_SEQ = 16384
PAST_LEN = 128

N_GROUPS = 4
GROUP_W = D_MODEL // N_GROUPS
N_HEADS_G = 4
HEAD_DIM = GROUP_W // N_HEADS_G
DIFF_HALF = HEAD_DIM // 2
N_SPLITS = 14
IN_COLS = N_SPLITS * GROUP_W
D_FF = -(-8 * D_MODEL // (3 * 256)) * 256
PLE_DIM = 256
CONV_W = 4
CONV_PAD_L = 2
RG_C = 8.0
HGRN_CHUNK = 64
RET_CHUNK = 128
Q_BLOCK = 128
EPS = 1e-6

kernel_name = 'hybrid_parallel_group_encoder'


def rmsnorm(x, w):
    xf = x.astype(jnp.float32)
    y = xf * lax.rsqrt(jnp.mean(xf * xf, axis=-1, keepdims=True) + EPS)
    return (y * w.astype(jnp.float32)).astype(x.dtype)


def split_heads(t):
    b, s, _ = t.shape
    return t.reshape(b, s, N_HEADS_G, HEAD_DIM).transpose(0, 2, 1, 3)


def merge_heads(t):
    b, h, s, d = t.shape
    return t.transpose(0, 2, 1, 3).reshape(b, s, h * d)


def alibi_slopes():
    return jnp.asarray(2.0 ** (-8.0 * (np.arange(N_HEADS_G) + 1) / N_HEADS_G), jnp.float32)


def diff_attention(q, k, v, lam, lambda_init, subln_w):
    b, s, _ = q.shape
    f32 = jnp.float32

    def two_maps(t):
        return t.astype(f32).reshape(b, s, N_HEADS_G, 2, DIFF_HALF).transpose(3, 0, 2, 1, 4)

    q2, k2 = two_maps(q), two_maps(k)
    vh = split_heads(v.astype(f32))
    nblk = s // Q_BLOCK
    q_blocks = q2.reshape(2, b, N_HEADS_G, nblk, Q_BLOCK, DIFF_HALF).transpose(3, 0, 1, 2, 4, 5)
    starts = jnp.arange(nblk, dtype=f32) * Q_BLOCK
    k_pos = jnp.arange(s, dtype=f32)
    slopes = alibi_slopes()
    scale = DIFF_HALF ** -0.5

    def one_block(args):
        qb, start = args
        scores = jnp.einsum('cbhqd,cbhkd->cbhqk', qb, k2) * scale
        q_pos = start + jnp.arange(Q_BLOCK, dtype=f32)
        bias = -slopes[:, None, None] * jnp.abs(q_pos[:, None] - k_pos[None, :])
        probs = jax.nn.softmax(scores + bias, axis=-1)
        weights = probs[0] - lam * probs[1]
        return jnp.einsum('bhqk,bhkd->bhqd', weights, vh)

    o = lax.map(one_block, (q_blocks, starts))
    o = o.transpose(1, 2, 0, 3, 4).reshape(b, N_HEADS_G, s, HEAD_DIM)
    o = rmsnorm(o, subln_w) * (1.0 - lambda_init)
    return merge_heads(o).astype(q.dtype)


def gated_chunk_scan(q, k, v, log_f):
    b, h, s, dk = q.shape
    dv = v.shape[-1]
    n = s // HGRN_CHUNK

    def chunks(t):
        return t.reshape(b, h, n, HGRN_CHUNK, t.shape[-1]).transpose(2, 0, 1, 3, 4)

    lower = jnp.tril(jnp.ones((HGRN_CHUNK, HGRN_CHUNK), dtype=bool))[:, :, None]

    def step(state, inp):
        qc, kc, vc, lf = inp
        cum = jnp.cumsum(lf, axis=2)
        diff = jnp.where(lower, cum[:, :, :, None, :] - cum[:, :, None, :, :], 0.0)
        rel = jnp.where(lower, jnp.exp(diff), 0.0)
        scores = jnp.einsum('bhtk,bhsk,bhtsk->bhts', qc, kc, rel)
        intra = jnp.einsum('bhts,bhsv->bhtv', scores, vc)
        inter = jnp.einsum('bhtk,bhkv->bhtv', qc * jnp.exp(cum), state)
        last = cum[:, :, -1:, :]
        new_state = (jnp.exp(last[:, :, 0, :])[..., None] * state
                     + jnp.einsum('bhsk,bhsv->bhkv', kc * jnp.exp(last - cum), vc))
        return new_state, intra + inter

    state0 = jnp.zeros((b, h, dk, dv), jnp.float32)
    _, o = lax.scan(step, state0, (chunks(q), chunks(k), chunks(v), chunks(log_f)))
    return o.transpose(1, 2, 0, 3, 4).reshape(b, h, s, dv)


def hgrn2_mixer(q, f_fw, f_bw, i, g, lb, norm_w):
    f32 = jnp.float32
    qh = split_heads(jax.nn.silu(q.astype(f32)))
    vh = split_heads(i.astype(f32))

    def direction(f_raw, lb_d, flip):
        f_raw = f_raw.astype(f32)
        f_gate = lb_d + (1.0 - lb_d) * jax.nn.sigmoid(f_raw)
        log_f = jnp.log(f_gate)
        key_in = (1.0 - lb_d) * jax.nn.sigmoid(-f_raw)
        args = [qh, split_heads(key_in), vh, split_heads(log_f)]
        if flip:
            args = [jnp.flip(t, axis=2) for t in args]
        o = gated_chunk_scan(*args)
        return jnp.flip(o, axis=2) if flip else o

    o = direction(f_fw, lb[0], False) + direction(f_bw, lb[1], True)
    o = merge_heads(rmsnorm(o, norm_w))
    return (o * jax.nn.silu(g.astype(f32))).astype(q.dtype)


def linear_combine(e1, e2):
    a1, u1 = e1
    a2, u2 = e2
    return a1 * a2, a2 * u1 + u2


def rglru_mixer(xb, gate, conv_w, conv_b, w_a, b_a, w_x, b_x, lam):
    f32 = jnp.float32
    b, s, _ = xb.shape
    xc = lax.conv_general_dilated(
        xb.astype(f32), conv_w.astype(f32)[:, None, :], window_strides=(1,),
        padding=[(CONV_PAD_L, CONV_W - 1 - CONV_PAD_L)],
        dimension_numbers=('NWC', 'WIO', 'NWC'), feature_group_count=GROUP_W,
    ) + conv_b.astype(f32)
    xblk = xc.reshape(b, s, N_HEADS_G, HEAD_DIM)

    def direction(d, reverse):
        r = jax.nn.sigmoid(jnp.einsum('bshi,hij->bshj', xblk, w_a[d].astype(f32)).reshape(b, s, GROUP_W) + b_a[d].astype(f32))
        ig = jax.nn.sigmoid(jnp.einsum('bshi,hij->bshj', xblk, w_x[d].astype(f32)).reshape(b, s, GROUP_W) + b_x[d].astype(f32))
        log_a = -RG_C * r * jax.nn.softplus(-lam[d].astype(f32))
        a = jnp.exp(log_a)
        u = jnp.sqrt(jnp.maximum(-jnp.expm1(2.0 * log_a), 0.0)) * (ig * xc)
        _, hs = lax.associative_scan(linear_combine, (a, u), reverse=reverse, axis=1)
        return hs

    hs = direction(0, False) + direction(1, True)
    return (hs * jax.nn.gelu(gate.astype(f32))).astype(xb.dtype)


def retention_chunk_scan(q, k, v, log_g):
    b, h, s, dk = q.shape
    dv = v.shape[-1]
    n = s // RET_CHUNK
    idx = jnp.arange(RET_CHUNK, dtype=jnp.float32)
    rel = idx[:, None] - idx[None, :]
    lower = rel >= 0
    dmat = jnp.where(lower, jnp.exp(jnp.where(lower, rel, 0.0)[None] * log_g[:, None, None]), 0.0)
    xi = jnp.exp((idx + 1.0)[None, :] * log_g[:, None])[..., None]
    zeta = jnp.exp((RET_CHUNK - 1.0 - idx)[None, :] * log_g[:, None])[..., None]
    chunk_decay = jnp.exp(RET_CHUNK * log_g)[:, None, None]

    def chunks(t):
        return t.reshape(b, h, n, RET_CHUNK, t.shape[-1]).transpose(2, 0, 1, 3, 4)

    def step(state, inp):
        qc, kc, vc = inp
        scores = jnp.einsum('bhtd,bhsd->bhts', qc, kc) * dmat
        intra = jnp.einsum('bhts,bhsv->bhtv', scores, vc)
        inter = jnp.einsum('bhtd,bhdv->bhtv', qc * xi, state)
        new_state = chunk_decay * state + jnp.einsum('bhsd,bhsv->bhdv', kc * zeta, vc)
        return new_state, intra + inter

    state0 = jnp.zeros((b, h, dk, dv), jnp.float32)
    _, o = lax.scan(step, state0, (chunks(q), chunks(k), chunks(v)))
    return o.transpose(1, 2, 0, 3, 4).reshape(b, h, s, dv)


def retention_mixer(q, k, v, g, decay_logit, norm_w):
    f32 = jnp.float32
    qh = split_heads(q.astype(f32))
    kh = split_heads(k.astype(f32)) * HEAD_DIM ** -0.5
    vh = split_heads(v.astype(f32))

    def direction(logit, flip):
        log_g = jax.nn.log_sigmoid(logit.astype(f32))
        args = [qh, kh, vh]
        if flip:
            args = [jnp.flip(t, axis=2) for t in args]
        o = retention_chunk_scan(*args, log_g)
        return jnp.flip(o, axis=2) if flip else o

    o = direction(decay_logit[0], False) + direction(decay_logit[1], True)
    o = merge_heads(rmsnorm(o, norm_w))
    return (o * jax.nn.silu(g.astype(f32))).astype(q.dtype)


def trunk(x, p, norm_w, w_in, diff_lambda, diff_subln_w, hgrn_lb_raw, hgrn_norm_w,
          rg_conv_w, rg_conv_b, rg_w_a, rg_b_a, rg_w_x, rg_b_x, rg_lambda,
          ret_decay, ret_norm_w, w_out, w_ffn_in, w_ffn_out, w_ple_gate, w_ple_proj):
    lb_soft = jax.nn.softmax(hgrn_lb_raw.astype(jnp.float32), axis=0)
    lb_all = jnp.cumsum(lb_soft, axis=0) - lb_soft[0]
    r = x
    for l in range(DEPTH):
        h = rmsnorm(r, norm_w[l, 0])
        (a_q, a_k, a_v, b_q, b_f_fw, b_f_bw, b_i, b_g,
         c_x, c_g, d_q, d_k, d_v, d_g) = jnp.split(h @ w_in[l], N_SPLITS, axis=-1)
        lambda_init = 0.8 - 0.6 * math.exp(-0.3 * l)
        dl = diff_lambda[l].astype(jnp.float32)
        lam = jnp.exp(jnp.sum(dl[0] * dl[1])) - jnp.exp(jnp.sum(dl[2] * dl[3])) + lambda_init
        out_a = diff_attention(a_q, a_k, a_v, lam, lambda_init, diff_subln_w[l])
        out_b = hgrn2_mixer(b_q, b_f_fw, b_f_bw, b_i, b_g, lb_all[l], hgrn_norm_w[l])
        out_c = rglru_mixer(c_x, c_g, rg_conv_w[l], rg_conv_b[l], rg_w_a[l], rg_b_a[l],
                            rg_w_x[l], rg_b_x[l], rg_lambda[l])
        out_d = retention_mixer(d_q, d_k, d_v, d_g, ret_decay[l], ret_norm_w[l])
        mixed = jnp.concatenate([out_a, out_b, out_c, out_d], axis=-1) @ w_out[l]
        r = r + rmsnorm(mixed, norm_w[l, 1])
        h2 = rmsnorm(r, norm_w[l, 2])
        ff_gate, ff_up = jnp.split(h2 @ w_ffn_in[l], 2, axis=-1)
        ff = (jax.nn.silu(ff_gate) * ff_up) @ w_ffn_out[l]
        r = r + rmsnorm(ff, norm_w[l, 3])
        r = r + jax.nn.sigmoid(r @ w_ple_gate[l]) * (p[l] @ w_ple_proj[l])
    return r


def setup_inputs(seed: int = 0) -> dict:
    key = jax.random.key(seed)
    ks = jax.random.split(key, 24)
    f32 = jnp.float32
    nrm = lambda k, shape: jax.random.normal(k, shape, f32)
    u = jax.random.uniform(ks[16], (DEPTH, 2, GROUP_W), f32, minval=0.9, maxval=0.999)
    a0 = u ** (1.0 / RG_C)
    ret_base = jnp.asarray(np.log(2.0 ** (5 + np.arange(N_HEADS_G)) - 1.0), f32)
    return {
        'x_prompt': nrm(ks[0], (BATCH, SEQ, D_MODEL)),
        'x_sample': nrm(ks[1], (DEC_BATCH, DEC_SEQ, D_MODEL)),
        'p_prompt': nrm(ks[2], (DEPTH, BATCH, SEQ, PLE_DIM)),
        'p_sample': nrm(ks[3], (DEPTH, DEC_BATCH, DEC_SEQ, PLE_DIM)),
        'norm_w': 1.0 + 0.05 * nrm(ks[4], (DEPTH, 4, D_MODEL)),
        'w_in': nrm(ks[5], (DEPTH, D_MODEL, IN_COLS)) * D_MODEL ** -0.5,
        'diff_lambda': 0.1 * nrm(ks[6], (DEPTH, 4, DIFF_HALF)),
        'diff_subln_w': 1.0 + 0.05 * nrm(ks[7], (DEPTH, HEAD_DIM)),
        'hgrn_lb_raw': 0.5 * nrm(ks[8], (DEPTH, 2, GROUP_W)),
        'hgrn_norm_w': 1.0 + 0.05 * nrm(ks[9], (DEPTH, HEAD_DIM)),
        'rg_conv_w': nrm(ks[10], (DEPTH, CONV_W, GROUP_W)) * CONV_W ** -0.5,
        'rg_conv_b': 0.02 * nrm(ks[11], (DEPTH, GROUP_W)),
        'rg_w_a': nrm(ks[12], (DEPTH, 2, N_HEADS_G, HEAD_DIM, HEAD_DIM)) * HEAD_DIM ** -0.5,
        'rg_b_a': 0.02 * nrm(ks[13], (DEPTH, 2, GROUP_W)),
        'rg_w_x': nrm(ks[14], (DEPTH, 2, N_HEADS_G, HEAD_DIM, HEAD_DIM)) * HEAD_DIM ** -0.5,
        'rg_b_x': 0.02 * nrm(ks[15], (DEPTH, 2, GROUP_W)),
        'rg_lambda': jnp.log(a0) - jnp.log1p(-a0),
        'ret_decay': ret_base + 0.05 * nrm(ks[17], (DEPTH, 2, N_HEADS_G)),
        'ret_norm_w': 1.0 + 0.05 * nrm(ks[18], (DEPTH, HEAD_DIM)),
        'w_out': nrm(ks[19], (DEPTH, D_MODEL, D_MODEL)) * D_MODEL ** -0.5,
        'w_ffn_in': nrm(ks[20], (DEPTH, D_MODEL, 2 * D_FF)) * D_MODEL ** -0.5,
        'w_ffn_out': nrm(ks[21], (DEPTH, D_FF, D_MODEL)) * D_FF ** -0.5,
        'w_ple_gate': nrm(ks[22], (DEPTH, D_MODEL, D_MODEL)) * D_MODEL ** -0.5,
        'w_ple_proj': nrm(ks[23], (DEPTH, PLE_DIM, D_MODEL)) * PLE_DIM ** -0.5,
    }


def reference(x_prompt, x_sample, p_prompt, p_sample, norm_w, w_in, diff_lambda, diff_subln_w,
              hgrn_lb_raw, hgrn_norm_w, rg_conv_w, rg_conv_b, rg_w_a, rg_b_a, rg_w_x, rg_b_x,
              rg_lambda, ret_decay, ret_norm_w, w_out, w_ffn_in, w_ffn_out, w_ple_gate, w_ple_proj):
    y_prompt = trunk(x_prompt, p_prompt, norm_w, w_in, diff_lambda, diff_subln_w, hgrn_lb_raw,
                     hgrn_norm_w, rg_conv_w, rg_conv_b, rg_w_a, rg_b_a, rg_w_x, rg_b_x, rg_lambda,
                     ret_decay, ret_norm_w, w_out, w_ffn_in, w_ffn_out, w_ple_gate, w_ple_proj)
    y_sample = trunk(x_sample, p_sample, norm_w, w_in, diff_lambda, diff_subln_w, hgrn_lb_raw,
                     hgrn_norm_w, rg_conv_w, rg_conv_b, rg_w_a, rg_b_a, rg_w_x, rg_b_x, rg_lambda,
                     ret_decay, ret_norm_w, w_out, w_ffn_in, w_ffn_out, w_ple_gate, w_ple_proj)
    return (y_prompt, y_sample)
```

```python
import functools
import math

import jax
import jax.numpy as jnp
from jax import lax
from jax.experimental import pallas as pl
from jax.experimental.pallas import tpu as pltpu

F32 = jnp.float32
BF16 = jnp.bfloat16

D_MODEL = 1024
DEPTH = 4
GROUP_W = 256
N_HEADS = 4
HEAD_DIM = 64
DIFF_HALF = 32
D_FF = 2816
FF_CHUNK = 256
N_FF_CHUNKS = D_FF // FF_CHUNK
PLE_DIM = 256
CONV_W = 4
CONV_PAD_L = 2
RG_C = 8.0
EPS = 1e-6
LOG2E = 1.4426950408889634

(P_AK, P_BQ, P_BF_FW, P_BF_BW, P_BI, P_BG, P_CX, P_CG, P_DQ, P_DK, P_DV, P_DG) = range(12)
N_PGROUPS = 12

VMEM_LIMIT_V7X = 56 * 1024 * 1024

TM_INPROJ = 512
TM_POST = 512
TQ_ATTN = 256
TK_ATTN = 256
HGRN_C = 64
HGRN_MID = HGRN_C // 2
HGRN_SAFE_EXP = 80.0
RET_C = 256
RG_TB = 256
RG_HALO = 16
SUBLANES = 8


def _cparams(n_axes):
    return pltpu.CompilerParams(dimension_semantics=("arbitrary",) * n_axes,
                                vmem_limit_bytes=VMEM_LIMIT_V7X)


def _const_spec(shape):
    nd = len(shape)
    return pl.BlockSpec(shape, lambda *_: (0,) * nd, pipeline_mode=pl.Buffered(1))


def _rms(x, w):
    return x * lax.rsqrt(jnp.mean(x * x, axis=-1, keepdims=True) + EPS) * w


def _sigmoid(x):
    return 1.0 / (1.0 + jnp.exp(-x))


def _silu(x):
    return x * _sigmoid(x)


def _softplus(x):
    return jnp.maximum(x, 0.0) + jnp.log(1.0 + jnp.exp(-jnp.abs(x)))


def _dot(a, b):
    return jnp.dot(a, b, preferred_element_type=F32)


def _dot_nt(a, b):
    return lax.dot_general(a, b, (((1,), (1,)), ((), ())), preferred_element_type=F32)


def _dot_tn(a, b):
    return lax.dot_general(a, b, (((0,), (0,)), ((), ())), preferred_element_type=F32)


def _inproj_kernel(x_ref, nw_ref, wn_ref, wt_ref, p_ref, qt_ref, vt_ref):
    x = x_ref[...]
    y = _rms(x, nw_ref[...]).astype(BF16)
    for g in range(N_PGROUPS):
        cols = slice(g * GROUP_W, (g + 1) * GROUP_W)
        p_ref[:, cols] = _dot(y, wn_ref[:, cols]).astype(BF16)
    t = _dot_nt(wt_ref[...], y)
    qt_ref[0] = (t[:GROUP_W] * (DIFF_HALF ** -0.5 * LOG2E)).astype(BF16)
    vt_ref[0] = t[GROUP_W:].astype(BF16)


def _inproj(r, nw, w_nat, w_qv_t, batch, seq):
    tokens = batch * seq
    tm = TM_INPROJ
    nsb = seq // tm
    return pl.pallas_call(
        _inproj_kernel,
        grid=(tokens // tm,),
        in_specs=[
            pl.BlockSpec((tm, D_MODEL), lambda i: (i, 0)),
            _const_spec((1, D_MODEL)),
            _const_spec((D_MODEL, N_PGROUPS * GROUP_W)),
            _const_spec((2 * GROUP_W, D_MODEL)),
        ],
        out_specs=[
            pl.BlockSpec((tm, N_PGROUPS * GROUP_W), lambda i: (i, 0)),
            pl.BlockSpec((1, GROUP_W, tm), lambda i: (i // nsb, 0, i % nsb)),
            pl.BlockSpec((1, GROUP_W, tm), lambda i: (i // nsb, 0, i % nsb)),
        ],
        out_shape=[
            jax.ShapeDtypeStruct((tokens, N_PGROUPS * GROUP_W), BF16),
            jax.ShapeDtypeStruct((batch, GROUP_W, seq), BF16),
            jax.ShapeDtypeStruct((batch, GROUP_W, seq), BF16),
        ],
        compiler_params=_cparams(1),
        name="inproj",
    )(r, nw, w_nat, w_qv_t)


def _attn_kernel(qt_ref, k_ref, vt_ref, dl_ref, sw_ref, o_ref, *, seq, lambda_init):
    tq, tk = TQ_ATTN, TK_ATTN
    q0 = pl.program_id(1) * tq
    nk = seq // tk
    dl = dl_ref[...]
    lam = (jnp.exp(jnp.sum(dl[0:1] * dl[1:2], axis=1, keepdims=True))
           - jnp.exp(jnp.sum(dl[2:3] * dl[3:4], axis=1, keepdims=True)) + lambda_init)
    row = lax.broadcasted_iota(jnp.int32, (tk, tq), 0)
    col = lax.broadcasted_iota(jnp.int32, (tk, tq), 1)
    rel = (col - row + q0).astype(F32)
    sub = lax.broadcasted_iota(jnp.int32, (2 * HEAD_DIM, tq), 0)
    ones = jnp.ones((2 * SUBLANES, tk), BF16)
    outs = []
    for h in range(N_HEADS):
        half = h // 2
        lanes = slice(half * 2 * HEAD_DIM, (half + 1) * 2 * HEAD_DIM)
        slope2 = 2.0 ** (-8.0 * (h + 1) / N_HEADS) * LOG2E
        qt_half = qt_ref[0, lanes, :]
        per_map = []
        for c in range(2):
            lo = (h % 2) * HEAD_DIM + c * DIFF_HALF
            w = jnp.where((sub >= lo) & (sub < lo + DIFF_HALF), qt_half, jnp.zeros_like(qt_half))

            def body(j, carry, w=w, lanes=lanes, h=h, slope2=slope2):
                m, acc = carry
                k0 = pl.multiple_of(j * tk, tk)
                kt = k_ref[0, pl.ds(k0, tk), lanes]
                s = _dot(kt, w)
                s = s - slope2 * jnp.abs(rel - k0.astype(F32))
                m_new = jnp.maximum(m, jnp.max(s, axis=0, keepdims=True))
                alpha = jnp.exp2(m - m_new)
                p = jnp.exp2(s - m_new).astype(BF16)
                vt = jnp.concatenate(
                    [vt_ref[0, h * HEAD_DIM:(h + 1) * HEAD_DIM, pl.ds(k0, tk)], ones], axis=0)
                return m_new, acc * alpha + _dot(vt, p)

            m0 = jnp.full((1, tq), -1e30, F32)
            a0 = jnp.zeros((HEAD_DIM + 2 * SUBLANES, tq), F32)
            _, acc = lax.fori_loop(0, nk, body, (m0, a0))
            per_map.append(acc[:HEAD_DIM] / acc[HEAD_DIM:HEAD_DIM + 1])
        o = per_map[0] - lam * per_map[1]
        ms = jnp.mean(o * o, axis=0, keepdims=True)
        outs.append(o * lax.rsqrt(ms + EPS) * sw_ref[...] * (1.0 - lambda_init))
    o_ref[0] = jnp.concatenate(outs, axis=0).T.astype(BF16)


def _attention(qt, p3, vt, dl, sw, batch, seq, lambda_init):
    tq = TQ_ATTN
    return pl.pallas_call(
        functools.partial(_attn_kernel, seq=seq, lambda_init=lambda_init),
        grid=(batch, seq // tq),
        in_specs=[
            pl.BlockSpec((1, GROUP_W, tq), lambda b, i: (b, 0, i)),
            pl.BlockSpec((1, seq, GROUP_W), lambda b, i: (b, 0, P_AK), pipeline_mode=pl.Buffered(1)),
            pl.BlockSpec((1, GROUP_W, seq), lambda b, i: (b, 0, 0), pipeline_mode=pl.Buffered(1)),
            _const_spec((4, DIFF_HALF)),
            _const_spec((HEAD_DIM, 1)),
        ],
        out_specs=pl.BlockSpec((1, tq, GROUP_W), lambda b, i: (b, i, 0)),
        out_shape=jax.ShapeDtypeStruct((batch, seq, GROUP_W), BF16),
        compiler_params=_cparams(2),
        name="diff_attn",
    )(qt, p3, vt, dl, sw)


def _hgrn_kernel(*refs, layer, rev):
    if rev:
        (q_ref, f_ref, v_ref, lbraw_ref, part_ref, g_ref, nw_ref, o_ref,
         st_ref, qs_ref, ks_ref, fs_ref, vs_ref, os_ref) = refs
    else:
        (q_ref, f_ref, v_ref, lbraw_ref, o_ref,
         st_ref, qs_ref, ks_ref, fs_ref, vs_ref, os_ref) = refs
    C = HGRN_C

    @pl.when(pl.program_id(1) == 0)
    def _():
        st_ref[...] = jnp.zeros_like(st_ref)

    raw = lbraw_ref[...]
    e = jnp.exp(raw - jnp.max(raw, axis=0, keepdims=True))
    soft = e / jnp.sum(e, axis=0, keepdims=True)
    lb = jnp.zeros((1, GROUP_W), F32)
    for i in range(1, layer + 1):
        lb = lb + soft[i:i + 1]

    xf = f_ref[0].astype(F32)
    fg = lb + (1.0 - lb) * _sigmoid(xf)
    kk = (1.0 - lb) * _sigmoid(-xf)
    q = _silu(q_ref[0].astype(F32))
    v = v_ref[0]
    lf = jnp.log(fg)

    row = lax.broadcasted_iota(jnp.int32, (C, C), 0)
    col = lax.broadcasted_iota(jnp.int32, (C, C), 1)
    tri = (row <= col) if rev else (row >= col)
    tri_b = jnp.where(tri, 1.0, 0.0).astype(BF16)
    hi = lf.astype(BF16)
    lo = (lf - hi.astype(F32)).astype(BF16)
    cum = _dot(tri_b, hi) + _dot(tri_b, lo)
    cm = cum[HGRN_MID:HGRN_MID + 1]
    last = cum[0:1] if rev else cum[C - 1:C]
    worst = jnp.max(jnp.abs(cum - cm))

    @pl.when(worst <= HGRN_SAFE_EXP)
    def _fast():
        qt = (q * jnp.exp(cum - cm)).astype(BF16)
        kt = (kk * jnp.exp(cm - cum)).astype(BF16)
        qi = (q * jnp.exp(cum)).astype(BF16)
        ki = (kk * jnp.exp(last - cum)).astype(BF16)
        dl = jnp.exp(last)
        for h in range(N_HEADS):
            hs = slice(h * HEAD_DIM, (h + 1) * HEAD_DIM)
            a = jnp.where(tri, _dot_nt(qt[:, hs], kt[:, hs]), 0.0).astype(BF16)
            st = st_ref[h]
            os_ref[:, hs] = _dot(a, v[:, hs]) + _dot_nt(qi[:, hs], st.astype(BF16))
            st_ref[h] = st * dl[:, hs] + _dot_tn(v[:, hs], ki[:, hs])

    @pl.when(worst > HGRN_SAFE_EXP)
    def _slow():
        qs_ref[...] = q
        ks_ref[...] = kk
        fs_ref[...] = fg
        vs_ref[...] = v.astype(F32)

        def step(i, carry):
            t = (C - 1 - i) if rev else i
            qr = qs_ref[pl.ds(t, 1), :]
            kr = ks_ref[pl.ds(t, 1), :]
            fr = fs_ref[pl.ds(t, 1), :]
            vr = vs_ref[pl.ds(t, 1), :]
            outs = []
            for h in range(N_HEADS):
                hs = slice(h * HEAD_DIM, (h + 1) * HEAD_DIM)
                k8 = jnp.broadcast_to(kr[:, hs], (SUBLANES, HEAD_DIM))
                v8 = jnp.broadcast_to(vr[:, hs], (SUBLANES, HEAD_DIM)) * (1.0 / SUBLANES)
                st = st_ref[h] * fr[:, hs] + _dot_tn(v8, k8)
                st_ref[h] = st
                q8 = jnp.broadcast_to(qr[:, hs], (SUBLANES, HEAD_DIM))
                outs.append(_dot_nt(q8, st)[0:1])
            os_ref[pl.ds(t, 1), :] = jnp.concatenate(outs, axis=1)
            return carry

        lax.fori_loop(0, C, step, 0)

    if rev:
        o = os_ref[...] + part_ref[0]
        g = g_ref[0].astype(F32)
        nw = nw_ref[...]
        outs = []
        for h in range(N_HEADS):
            hs = slice(h * HEAD_DIM, (h + 1) * HEAD_DIM)
            outs.append(_rms(o[:, hs], nw))
        o_ref[0] = (jnp.concatenate(outs, axis=1) * _silu(g)).astype(BF16)
    else:
        o_ref[0] = os_ref[...]


def _hgrn(p3, lb_raw, norm_w, batch, seq, layer):
    C = HGRN_C
    nc = seq // C
    scratch = [pltpu.VMEM((N_HEADS, HEAD_DIM, HEAD_DIM), F32)] + [pltpu.VMEM((C, GROUP_W), F32)] * 5

    def grp(g, rev):
        if rev:
            return pl.BlockSpec((1, C, GROUP_W), lambda b, j: (b, nc - 1 - j, g))
        return pl.BlockSpec((1, C, GROUP_W), lambda b, j: (b, j, g))

    part = pl.pallas_call(
        functools.partial(_hgrn_kernel, layer=layer, rev=False),
        grid=(batch, nc),
        in_specs=[grp(P_BQ, False), grp(P_BF_FW, False), grp(P_BI, False), _const_spec((DEPTH, GROUP_W))],
        out_specs=pl.BlockSpec((1, C, GROUP_W), lambda b, j: (b, j, 0)),
        out_shape=jax.ShapeDtypeStruct((batch, seq, GROUP_W), F32),
        scratch_shapes=scratch,
        compiler_params=_cparams(2),
        name="hgrn_fw",
    )(p3, p3, p3, lb_raw[:, 0])
    return pl.pallas_call(
        functools.partial(_hgrn_kernel, layer=layer, rev=True),
        grid=(batch, nc),
        in_specs=[grp(P_BQ, True), grp(P_BF_BW, True), grp(P_BI, True), _const_spec((DEPTH, GROUP_W)),
                  pl.BlockSpec((1, C, GROUP_W), lambda b, j: (b, nc - 1 - j, 0)),
                  grp(P_BG, True), _const_spec((1, HEAD_DIM))],
        out_specs=pl.BlockSpec((1, C, GROUP_W), lambda b, j: (b, nc - 1 - j, 0)),
        out_shape=jax.ShapeDtypeStruct((batch, seq, GROUP_W), BF16),
        scratch_shapes=scratch,
        compiler_params=_cparams(2),
        name="hgrn_bw",
    )(p3, p3, p3, lb_raw[:, 1], part, p3, norm_w)


def _log_sigmoid(x):
    return jnp.minimum(x, 0.0) - jnp.log(1.0 + jnp.exp(-jnp.abs(x)))


def _ret_kernel(*refs, rev):
    if rev:
        q_ref, k_ref, v_ref, dec_ref, part_ref, g_ref, nw_ref, o_ref, st_ref = refs
    else:
        q_ref, k_ref, v_ref, dec_ref, o_ref, st_ref = refs
    C = RET_C

    @pl.when(pl.program_id(1) == 0)
    def _():
        st_ref[...] = jnp.zeros_like(st_ref)

    lg_f = _log_sigmoid(dec_ref[0:1, :])
    lg_b = _log_sigmoid(dec_ref[1:2, :])
    lg = lg_b if rev else lg_f
    t = lax.broadcasted_iota(jnp.int32, (C, 1), 0).astype(F32)
    q = q_ref[0].astype(F32)
    k = k_ref[0].astype(F32) * HEAD_DIM ** -0.5
    v = v_ref[0]
    if rev:
        xi = jnp.exp((C - t) * lg)
        zeta = jnp.exp(t * lg)
    else:
        xi = jnp.exp((t + 1.0) * lg)
        zeta = jnp.exp((C - 1.0 - t) * lg)
    cd = jnp.exp(C * lg)
    qx = (q * xi).astype(BF16)
    kz = (k * zeta).astype(BF16)
    if not rev:
        qb = q.astype(BF16)
        kb = k.astype(BF16)
        row = lax.broadcasted_iota(jnp.int32, (C, C), 0)
        col = lax.broadcasted_iota(jnp.int32, (C, C), 1)
        d_ts = (row - col).astype(F32)
    outs = []
    for h in range(N_HEADS):
        hs = slice(h * HEAD_DIM, (h + 1) * HEAD_DIM)
        st = st_ref[h]
        o = _dot(qx[:, hs], st.astype(BF16))
        if not rev:
            lf1 = lg_f[:, h * HEAD_DIM:h * HEAD_DIM + 1]
            lb1 = lg_b[:, h * HEAD_DIM:h * HEAD_DIM + 1]
            dm = (jnp.where(d_ts >= 0, jnp.exp(jnp.maximum(d_ts, 0.0) * lf1), 0.0)
                  + jnp.where(d_ts <= 0, jnp.exp(jnp.maximum(-d_ts, 0.0) * lb1), 0.0))
            sc = (_dot_nt(qb[:, hs], kb[:, hs]) * dm).astype(BF16)
            o = o + _dot(sc, v[:, hs])
        st_ref[h] = st * cd[:, h * HEAD_DIM:h * HEAD_DIM + 1] + _dot_tn(kz[:, hs], v[:, hs])
        outs.append(o)
    if rev:
        part = part_ref[0]
        nw = nw_ref[...]
        g = g_ref[0].astype(F32)
        normed = [_rms(outs[h] + part[:, h * HEAD_DIM:(h + 1) * HEAD_DIM], nw) for h in range(N_HEADS)]
        o_ref[0] = (jnp.concatenate(normed, axis=1) * _silu(g)).astype(BF16)
    else:
        o_ref[0] = jnp.concatenate(outs, axis=1)


def _retention(p3, dec_lanes, norm_w, batch, seq):
    C = RET_C
    nc = seq // C
    scratch = [pltpu.VMEM((N_HEADS, HEAD_DIM, HEAD_DIM), F32)]

    def grp(g, rev):
        if rev:
            return pl.BlockSpec((1, C, GROUP_W), lambda b, j: (b, nc - 1 - j, g))
        return pl.BlockSpec((1, C, GROUP_W), lambda b, j: (b, j, g))

    part = pl.pallas_call(
        functools.partial(_ret_kernel, rev=False),
        grid=(batch, nc),
        in_specs=[grp(P_DQ, False), grp(P_DK, False), grp(P_DV, False), _const_spec((2, GROUP_W))],
        out_specs=pl.BlockSpec((1, C, GROUP_W), lambda b, j: (b, j, 0)),
        out_shape=jax.ShapeDtypeStruct((batch, seq, GROUP_W), F32),
        scratch_shapes=scratch,
        compiler_params=_cparams(2),
        name="ret_fw",
    )(p3, p3, p3, dec_lanes)
    return pl.pallas_call(
        functools.partial(_ret_kernel, rev=True),
        grid=(batch, nc),
        in_specs=[grp(P_DQ, True), grp(P_DK, True), grp(P_DV, True), _const_spec((2, GROUP_W)),
                  pl.BlockSpec((1, C, GROUP_W), lambda b, j: (b, nc - 1 - j, 0)),
                  grp(P_DG, True), _const_spec((1, HEAD_DIM))],
        out_specs=pl.BlockSpec((1, C, GROUP_W), lambda b, j: (b, nc - 1 - j, 0)),
        out_shape=jax.ShapeDtypeStruct((batch, seq, GROUP_W), BF16),
        scratch_shapes=scratch,
        compiler_params=_cparams(2),
        name="ret_bw",
    )(p3, p3, p3, dec_lanes, part, p3, norm_w)


def _rglru_kernel(*refs, rev, nblk):
    if rev:
        (x_ref, xp_ref, xn_ref, cw_ref, cb_ref, wg_ref, bg_ref, lam_ref, part_ref, gate_ref,
         o_ref, carry_ref, a_ref, u_ref, h_ref) = refs
    else:
        (x_ref, xp_ref, xn_ref, cw_ref, cb_ref, wg_ref, bg_ref, lam_ref,
         o_ref, carry_ref, a_ref, u_ref, h_ref) = refs
    TB = RG_TB
    j = pl.program_id(1)
    blk = (nblk - 1 - j) if rev else j

    @pl.when(j == 0)
    def _():
        carry_ref[...] = jnp.zeros_like(carry_ref)

    prev = jnp.where(blk > 0, xp_ref[0].astype(F32), 0.0)
    nxt = jnp.where(blk < nblk - 1, xn_ref[0].astype(F32), 0.0)
    xe = jnp.concatenate([prev, x_ref[0].astype(F32), nxt], axis=0)
    cw = cw_ref[...]
    xc = cb_ref[...] + sum(
        xe[RG_HALO - CONV_PAD_L + w:RG_HALO - CONV_PAD_L + w + TB] * cw[w:w + 1] for w in range(CONV_W))
    gates = _dot(xc.astype(BF16), wg_ref[...]) + bg_ref[...]
    r = _sigmoid(gates[:, :GROUP_W])
    ig = _sigmoid(gates[:, GROUP_W:])
    log_a = -RG_C * r * _softplus(-lam_ref[...])
    a = jnp.exp(log_a)
    u = jnp.sqrt(jnp.maximum(1.0 - jnp.exp(2.0 * log_a), 0.0)) * (ig * xc)

    pos = lax.broadcasted_iota(jnp.int32, (TB, GROUP_W), 0) % SUBLANES
    for d in (1, 2, 4):
        if rev:
            keep = pos < SUBLANES - d
            shift = TB - d
        else:
            keep = pos >= d
            shift = d
        a_s = jnp.where(keep, pltpu.roll(a, shift, 0), 1.0)
        u_s = jnp.where(keep, pltpu.roll(u, shift, 0), 0.0)
        u = a * u_s + u
        a = a * a_s
    a_ref[...] = a
    u_ref[...] = u
    ntile = TB // SUBLANES

    def tile_step(i, hprev):
        ti = (ntile - 1 - i) if rev else i
        r0 = pl.multiple_of(ti * SUBLANES, SUBLANES)
        h = a_ref[pl.ds(r0, SUBLANES), :] * hprev + u_ref[pl.ds(r0, SUBLANES), :]
        h_ref[pl.ds(r0, SUBLANES), :] = h
        edge = h[0:1] if rev else h[SUBLANES - 1:SUBLANES]
        return jnp.broadcast_to(edge, (SUBLANES, GROUP_W))

    carry_ref[...] = lax.fori_loop(0, ntile, tile_step, carry_ref[...])
    if rev:
        hs = h_ref[...] + part_ref[0]
        o_ref[0] = (hs * jax.nn.gelu(gate_ref[0].astype(F32), approximate=True)).astype(BF16)
    else:
        o_ref[0] = h_ref[...]


def _rglru(p3, conv_w, conv_b, wg, bg, lam, batch, seq):
    TB = RG_TB
    nblk = seq // TB
    hpb = TB // RG_HALO
    nhalo = seq // RG_HALO
    scratch = [pltpu.VMEM((SUBLANES, GROUP_W), F32)] + [pltpu.VMEM((TB, GROUP_W), F32)] * 3

    def specs(rev):
        def blk(j):
            return (nblk - 1 - j) if rev else j
        return [
            pl.BlockSpec((1, TB, GROUP_W), lambda b, j: (b, blk(j), P_CX)),
            pl.BlockSpec((1, RG_HALO, GROUP_W), lambda b, j: (b, jnp.maximum(blk(j) * hpb - 1, 0), P_CX)),
            pl.BlockSpec((1, RG_HALO, GROUP_W),
                         lambda b, j: (b, jnp.minimum((blk(j) + 1) * hpb, nhalo - 1), P_CX)),
            _const_spec((CONV_W, GROUP_W)),
            _const_spec((1, GROUP_W)),
            _const_spec((GROUP_W, 2 * GROUP_W)),
            _const_spec((1, 2 * GROUP_W)),
            _const_spec((1, GROUP_W)),
        ]

    part = pl.pallas_call(
        functools.partial(_rglru_kernel, rev=False, nblk=nblk),
        grid=(batch, nblk),
        in_specs=specs(False),
        out_specs=pl.BlockSpec((1, TB, GROUP_W), lambda b, j: (b, j, 0)),
        out_shape=jax.ShapeDtypeStruct((batch, seq, GROUP_W), F32),
        scratch_shapes=scratch,
        compiler_params=_cparams(2),
        name="rglru_fw",
    )(p3, p3, p3, conv_w, conv_b, wg[0], bg[0], lam[0:1])
    return pl.pallas_call(
        functools.partial(_rglru_kernel, rev=True, nblk=nblk),
        grid=(batch, nblk),
        in_specs=specs(True) + [
            pl.BlockSpec((1, TB, GROUP_W), lambda b, j: (b, nblk - 1 - j, 0)),
            pl.BlockSpec((1, TB, GROUP_W), lambda b, j: (b, nblk - 1 - j, P_CG)),
        ],
        out_specs=pl.BlockSpec((1, TB, GROUP_W), lambda b, j: (b, nblk - 1 - j, 0)),
        out_shape=jax.ShapeDtypeStruct((batch, seq, GROUP_W), BF16),
        scratch_shapes=scratch,
        compiler_params=_cparams(2),
        name="rglru_bw",
    )(p3, p3, p3, conv_w, conv_b, wg[1], bg[1], lam[1:2], part, p3)


def _post_kernel(oa_ref, ob_ref, oc_ref, od_ref, r_ref, p_ref, nw_ref, wo_ref, wig_ref, wiu_ref, wfo_ref,
                 wpg_ref, wpp_ref, out_ref):
    nw = nw_ref[...]
    mixed_in = jnp.concatenate([oa_ref[...], ob_ref[...], oc_ref[...], od_ref[...]], axis=1)
    r = r_ref[...] + _rms(_dot(mixed_in, wo_ref[...]), nw[1:2])
    h2 = _rms(r, nw[2:3]).astype(BF16)
    ff = jnp.zeros_like(r)
    for c in range(N_FF_CHUNKS):
        act = _silu(_dot(h2, wig_ref[c])) * _dot(h2, wiu_ref[c])
        ff = ff + _dot(act.astype(BF16), wfo_ref[c])
    r = r + _rms(ff, nw[3:4])
    gate = _sigmoid(_dot(r.astype(BF16), wpg_ref[...]))
    out_ref[...] = r + gate * _dot(p_ref[0].astype(BF16), wpp_ref[...])


def _post(oa, ob, oc, od, r, p, layer, nw, wo, wig, wiu, wfo, wpg, wpp):
    tokens = r.shape[0]
    tm = TM_POST
    mix_spec = pl.BlockSpec((tm, GROUP_W), lambda i: (i, 0))
    return pl.pallas_call(
        _post_kernel,
        grid=(tokens // tm,),
        in_specs=[
            mix_spec, mix_spec, mix_spec, mix_spec,
            pl.BlockSpec((tm, D_MODEL), lambda i: (i, 0)),
            pl.BlockSpec((1, tm, PLE_DIM), lambda i: (layer, i, 0)),
            _const_spec((4, D_MODEL)),
            _const_spec((D_MODEL, D_MODEL)),
            _const_spec((N_FF_CHUNKS, D_MODEL, FF_CHUNK)),
            _const_spec((N_FF_CHUNKS, D_MODEL, FF_CHUNK)),
            _const_spec((N_FF_CHUNKS, FF_CHUNK, D_MODEL)),
            _const_spec((D_MODEL, D_MODEL)),
            _const_spec((PLE_DIM, D_MODEL)),
        ],
        out_specs=pl.BlockSpec((tm, D_MODEL), lambda i: (i, 0)),
        out_shape=jax.ShapeDtypeStruct((tokens, D_MODEL), F32),
        compiler_params=_cparams(1),
        name="post",
    )(oa, ob, oc, od, r, p, nw, wo, wig, wiu, wfo, wpg, wpp)


def _block_diag(w):
    out = jnp.zeros((GROUP_W, GROUP_W), w.dtype)
    for h in range(N_HEADS):
        out = out.at[h * HEAD_DIM:(h + 1) * HEAD_DIM, h * HEAD_DIM:(h + 1) * HEAD_DIM].set(w[h])
    return out


def _prep_weights(norm_w, w_in, diff_lambda, diff_subln_w, hgrn_lb_raw, hgrn_norm_w, rg_conv_w, rg_conv_b,
                  rg_w_a, rg_b_a, rg_w_x, rg_b_x, rg_lambda, ret_decay, ret_norm_w, w_out, w_ffn_in,
                  w_ffn_out, w_ple_gate, w_ple_proj):
    layers = []
    for l in range(DEPTH):
        wi = w_in[l]
        w_nat = jnp.concatenate([wi[:, GROUP_W:2 * GROUP_W], wi[:, 3 * GROUP_W:]], axis=1).astype(BF16)
        w_qv_t = jnp.concatenate([wi[:, :GROUP_W], wi[:, 2 * GROUP_W:3 * GROUP_W]], axis=1).T.astype(BF16)
        wg = [jnp.concatenate([_block_diag(rg_w_a[l, d]), _block_diag(rg_w_x[l, d])], axis=1).astype(BF16)
              for d in range(2)]
        bg = [jnp.concatenate([rg_b_a[l, d], rg_b_x[l, d]])[None, :] for d in range(2)]
        wfi = w_ffn_in[l].astype(BF16)
        layers.append(dict(
            nw=norm_w[l], nw0=norm_w[l, 0:1], w_nat=w_nat, w_qv_t=w_qv_t,
            dl=diff_lambda[l], sw=diff_subln_w[l][:, None],
            lb_raw=hgrn_lb_raw, hgrn_nw=hgrn_norm_w[l][None, :],
            conv_w=rg_conv_w[l], conv_b=rg_conv_b[l][None, :], wg=wg, bg=bg, lam=rg_lambda[l],
            dec=jnp.repeat(ret_decay[l], HEAD_DIM, axis=-1), ret_nw=ret_norm_w[l][None, :],
            wo=w_out[l].astype(BF16),
            wig=wfi[:, :D_FF].reshape(D_MODEL, N_FF_CHUNKS, FF_CHUNK).transpose(1, 0, 2),
            wiu=wfi[:, D_FF:].reshape(D_MODEL, N_FF_CHUNKS, FF_CHUNK).transpose(1, 0, 2),
            wfo=w_ffn_out[l].astype(BF16).reshape(N_FF_CHUNKS, FF_CHUNK, D_MODEL),
            wpg=w_ple_gate[l].astype(BF16), wpp=w_ple_proj[l].astype(BF16),
        ))
    return layers


def _trunk(x, p, layers):
    batch, seq, _ = x.shape
    tokens = batch * seq
    r = x.reshape(tokens, D_MODEL)
    p = p.reshape(DEPTH, tokens, PLE_DIM)
    for l, w in enumerate(layers):
        lambda_init = 0.8 - 0.6 * math.exp(-0.3 * l)
        pn, qt, vt = _inproj(r, w["nw0"], w["w_nat"], w["w_qv_t"], batch, seq)
        p3 = pn.reshape(batch, seq, N_PGROUPS * GROUP_W)
        oa = _attention(qt, p3, vt, w["dl"], w["sw"], batch, seq, lambda_init)
        ob = _hgrn(p3, w["lb_raw"], w["hgrn_nw"], batch, seq, l)
        oc = _rglru(p3, w["conv_w"], w["conv_b"], w["wg"], w["bg"], w["lam"], batch, seq)
        od = _retention(p3, w["dec"], w["ret_nw"], batch, seq)
        flat = lambda o: o.reshape(tokens, GROUP_W)
        r = _post(flat(oa), flat(ob), flat(oc), flat(od), r, p, l, w["nw"], w["wo"], w["wig"], w["wiu"],
                  w["wfo"], w["wpg"], w["wpp"])
    return r.reshape(batch, seq, D_MODEL)


def kernel(x_prompt, x_sample, p_prompt, p_sample, norm_w, w_in, diff_lambda, diff_subln_w, hgrn_lb_raw,
           hgrn_norm_w, rg_conv_w, rg_conv_b, rg_w_a, rg_b_a, rg_w_x, rg_b_x, rg_lambda, ret_decay,
           ret_norm_w, w_out, w_ffn_in, w_ffn_out, w_ple_gate, w_ple_proj):
    layers = _prep_weights(norm_w, w_in, diff_lambda, diff_subln_w, hgrn_lb_raw, hgrn_norm_w, rg_conv_w,
                           rg_conv_b, rg_w_a, rg_b_a, rg_w_x, rg_b_x, rg_lambda, ret_decay, ret_norm_w,
                           w_out, w_ffn_in, w_ffn_out, w_ple_gate, w_ple_proj)
    return (_trunk(x_prompt, p_prompt, layers), _trunk(x_sample, p_sample, layers))
```

```python
import functools
import math

import jax
import jax.numpy as jnp
from jax import lax
from jax.experimental import pallas as pl
from jax.experimental.pallas import tpu as pltpu

F32 = jnp.float32
BF16 = jnp.bfloat16

D_MODEL = 1024
DEPTH = 4
GROUP_W = 256
N_HEADS = 4
HEAD_DIM = 64
DIFF_HALF = 32
D_FF = 2816
FF_CHUNK = 256
N_FF_CHUNKS = D_FF // FF_CHUNK
PLE_DIM = 256
CONV_W = 4
CONV_PAD_L = 2
RG_C = 8.0
EPS = 1e-6
LOG2E = 1.4426950408889634

(P_AK, P_BQ, P_BF_FW, P_BF_BW, P_BI, P_BG, P_CX, P_CG, P_DQ, P_DK, P_DV, P_DG) = range(12)
N_PGROUPS = 12

VMEM_LIMIT_V7X = 56 * 1024 * 1024

TM_INPROJ = 512
TM_POST = 512
TQ_ATTN = 256
TK_ATTN = 256
ATTN_SKEW = 3
ATTN_SLOTS = ATTN_SKEW + 1
ATTN_UNROLL = 2
HGRN_C = 64
HGRN_MID = HGRN_C // 2
HGRN_SAFE_EXP = 80.0
RET_C = 256
RG_TB = 256
RG_HALO = 16
SUBLANES = 8


def _cparams(n_axes):
    return pltpu.CompilerParams(dimension_semantics=("arbitrary",) * n_axes,
                                vmem_limit_bytes=VMEM_LIMIT_V7X)


def _const_spec(shape):
    nd = len(shape)
    return pl.BlockSpec(shape, lambda *_: (0,) * nd, pipeline_mode=pl.Buffered(1))


def _rms(x, w):
    return x * lax.rsqrt(jnp.mean(x * x, axis=-1, keepdims=True) + EPS) * w


def _sigmoid(x):
    return 1.0 / (1.0 + jnp.exp(-x))


def _silu(x):
    return x * _sigmoid(x)


def _softplus(x):
    return jnp.maximum(x, 0.0) + jnp.log(1.0 + jnp.exp(-jnp.abs(x)))


def _dot(a, b):
    return jnp.dot(a, b, preferred_element_type=F32)


def _dot_nt(a, b):
    return lax.dot_general(a, b, (((1,), (1,)), ((), ())), preferred_element_type=F32)


def _dot_tn(a, b):
    return lax.dot_general(a, b, (((0,), (0,)), ((), ())), preferred_element_type=F32)


def _inproj_kernel(x_ref, nw_ref, wn_ref, wt_ref, p_ref, qt_ref, vt_ref):
    x = x_ref[...]
    y = _rms(x, nw_ref[...]).astype(BF16)
    for g in range(N_PGROUPS):
        cols = slice(g * GROUP_W, (g + 1) * GROUP_W)
        p_ref[:, cols] = _dot(y, wn_ref[:, cols]).astype(BF16)
    t = _dot_nt(wt_ref[...], y)
    qt_ref[0] = (t[:GROUP_W] * (DIFF_HALF ** -0.5 * LOG2E)).astype(BF16)
    vt_ref[0] = t[GROUP_W:].astype(BF16)


def _inproj(r, nw, w_nat, w_qv_t, batch, seq):
    tokens = batch * seq
    tm = TM_INPROJ
    nsb = seq // tm
    return pl.pallas_call(
        _inproj_kernel,
        grid=(tokens // tm,),
        in_specs=[
            pl.BlockSpec((tm, D_MODEL), lambda i: (i, 0)),
            _const_spec((1, D_MODEL)),
            _const_spec((D_MODEL, N_PGROUPS * GROUP_W)),
            _const_spec((2 * GROUP_W, D_MODEL)),
        ],
        out_specs=[
            pl.BlockSpec((tm, N_PGROUPS * GROUP_W), lambda i: (i, 0)),
            pl.BlockSpec((1, GROUP_W, tm), lambda i: (i // nsb, 0, i % nsb)),
            pl.BlockSpec((1, GROUP_W, tm), lambda i: (i // nsb, 0, i % nsb)),
        ],
        out_shape=[
            jax.ShapeDtypeStruct((tokens, N_PGROUPS * GROUP_W), BF16),
            jax.ShapeDtypeStruct((batch, GROUP_W, seq), BF16),
            jax.ShapeDtypeStruct((batch, GROUP_W, seq), BF16),
        ],
        compiler_params=_cparams(1),
        name="inproj",
    )(r, nw, w_nat, w_qv_t)


def _attn_kernel(qt_ref, k_ref, vt_ref, dl_ref, sw_ref, o_ref, w_ref, m_ref, acc_ref, s_ref, t_ref,
                 *, seq, lambda_init):
    tq, tk = TQ_ATTN, TK_ATTN
    assert tq == tk
    q0 = pl.program_id(1) * tq
    nk = seq // tk
    n_chain = 2 * N_HEADS
    sub = lax.broadcasted_iota(jnp.int32, (2 * HEAD_DIM, tq), 0)
    for h in range(N_HEADS):
        half = h // 2
        qt_half = qt_ref[0, half * 2 * HEAD_DIM:(half + 1) * 2 * HEAD_DIM, :]
        for c in range(2):
            lo = (h % 2) * HEAD_DIM + c * DIFF_HALF
            w_ref[2 * h + c] = jnp.where((sub >= lo) & (sub < lo + DIFF_HALF), qt_half,
                                         jnp.zeros_like(qt_half))
    m_ref[...] = jnp.full(m_ref.shape, -1e30, F32)
    acc_ref[...] = jnp.zeros_like(acc_ref)
    ones = jnp.ones((2 * SUBLANES, tk), BF16)
    slopes2 = [2.0 ** (-8.0 * (h + 1) / N_HEADS) * LOG2E for h in range(N_HEADS)]

    @pl.when((pl.program_id(0) == 0) & (pl.program_id(1) == 0))
    def _():
        row = lax.broadcasted_iota(jnp.int32, (tk, tq), 0)
        col = lax.broadcasted_iota(jnp.int32, (tk, tq), 1)
        d = (row - col).astype(F32)
        for h in range(N_HEADS):
            t_ref[h, 0] = slopes2[h] * d
            t_ref[h, 1] = -slopes2[h] * jnp.abs(d)
            t_ref[h, 2] = -slopes2[h] * d

    def scores(jj, i):
        k0 = pl.multiple_of(jj * tk, tk)
        half = i // 4
        kt = k_ref[0, pl.ds(k0, tk), half * 2 * HEAD_DIM:(half + 1) * 2 * HEAD_DIM]
        s_ref[i % ATTN_SLOTS] = _dot(kt, w_ref[i])

    def consume(j, i, sel, off):
        h = i // 2
        k0 = pl.multiple_of(j * tk, tk)
        u = s_ref[i % ATTN_SLOTS] + t_ref[h, sel]
        m = m_ref[i]
        m_new = jnp.maximum(m, jnp.max(u, axis=0, keepdims=True) - off[h])
        p = jnp.exp2(u - (m_new + off[h])).astype(BF16)
        vt = jnp.concatenate([vt_ref[0, h * HEAD_DIM:(h + 1) * HEAD_DIM, pl.ds(k0, tk)], ones], axis=0)
        acc_ref[i] = acc_ref[i] * jnp.exp2(m - m_new) + _dot(vt, p)
        m_ref[i] = m_new

    for i in range(ATTN_SKEW):
        scores(0, i)

    def body(jo, carry):
        qi = pl.program_id(1)
        for ju in range(ATTN_UNROLL):
            j = jo * ATTN_UNROLL + ju
            jn = jnp.minimum(j + 1, nk - 1)
            sel = (j >= qi).astype(jnp.int32) + (j > qi).astype(jnp.int32)
            gap = jnp.abs(q0 - j * tk).astype(F32)
            off = [slopes2[h] * gap for h in range(N_HEADS)]
            for i in range(n_chain):
                a = i + ATTN_SKEW
                if a < n_chain:
                    scores(j, a)
                else:
                    scores(jn, a - n_chain)
                consume(j, i, sel, off)
        return carry

    lax.fori_loop(0, nk // ATTN_UNROLL, body, 0)
    dl = dl_ref[...]
    lam = (jnp.exp(jnp.sum(dl[0:1] * dl[1:2], axis=1, keepdims=True))
           - jnp.exp(jnp.sum(dl[2:3] * dl[3:4], axis=1, keepdims=True)) + lambda_init)
    outs = []
    for h in range(N_HEADS):
        a1 = acc_ref[2 * h]
        a2 = acc_ref[2 * h + 1]
        o = (a1[:HEAD_DIM] / a1[HEAD_DIM:HEAD_DIM + 1]
             - lam * (a2[:HEAD_DIM] / a2[HEAD_DIM:HEAD_DIM + 1]))
        ms = jnp.mean(o * o, axis=0, keepdims=True)
        outs.append(o * lax.rsqrt(ms + EPS) * sw_ref[...] * (1.0 - lambda_init))
    o_ref[0] = jnp.concatenate(outs, axis=0).T.astype(BF16)


def _attention(qt, p3, vt, dl, sw, batch, seq, lambda_init):
    tq = TQ_ATTN
    n_chain = 2 * N_HEADS
    scratch = [
        pltpu.VMEM((n_chain, 2 * HEAD_DIM, tq), BF16),
        pltpu.VMEM((n_chain, 1, tq), F32),
        pltpu.VMEM((n_chain, HEAD_DIM + 2 * SUBLANES, tq), F32),
        pltpu.VMEM((ATTN_SLOTS, TK_ATTN, tq), F32),
        pltpu.VMEM((N_HEADS, 3, TK_ATTN, tq), F32),
    ]
    return pl.pallas_call(
        functools.partial(_attn_kernel, seq=seq, lambda_init=lambda_init),
        grid=(batch, seq // tq),
        in_specs=[
            pl.BlockSpec((1, GROUP_W, tq), lambda b, i: (b, 0, i)),
            pl.BlockSpec((1, seq, GROUP_W), lambda b, i: (b, 0, P_AK), pipeline_mode=pl.Buffered(1)),
            pl.BlockSpec((1, GROUP_W, seq), lambda b, i: (b, 0, 0), pipeline_mode=pl.Buffered(1)),
            _const_spec((4, DIFF_HALF)),
            _const_spec((HEAD_DIM, 1)),
        ],
        out_specs=pl.BlockSpec((1, tq, GROUP_W), lambda b, i: (b, i, 0)),
        out_shape=jax.ShapeDtypeStruct((batch, seq, GROUP_W), BF16),
        scratch_shapes=scratch,
        compiler_params=_cparams(2),
        name="diff_attn",
    )(qt, p3, vt, dl, sw)


def _hgrn_kernel(*refs, layer, rev):
    if rev:
        (q_ref, f_ref, v_ref, lbraw_ref, part_ref, g_ref, nw_ref, o_ref,
         st_ref, qs_ref, ks_ref, fs_ref, vs_ref, os_ref) = refs
    else:
        (q_ref, f_ref, v_ref, lbraw_ref, o_ref,
         st_ref, qs_ref, ks_ref, fs_ref, vs_ref, os_ref) = refs
    C = HGRN_C

    @pl.when(pl.program_id(1) == 0)
    def _():
        st_ref[...] = jnp.zeros_like(st_ref)

    raw = lbraw_ref[...]
    e = jnp.exp(raw - jnp.max(raw, axis=0, keepdims=True))
    soft = e / jnp.sum(e, axis=0, keepdims=True)
    lb = jnp.zeros((1, GROUP_W), F32)
    for i in range(1, layer + 1):
        lb = lb + soft[i:i + 1]

    xf = f_ref[0].astype(F32)
    fg = lb + (1.0 - lb) * _sigmoid(xf)
    kk = (1.0 - lb) * _sigmoid(-xf)
    q = _silu(q_ref[0].astype(F32))
    v = v_ref[0]
    lf = jnp.log(fg)

    row = lax.broadcasted_iota(jnp.int32, (C, C), 0)
    col = lax.broadcasted_iota(jnp.int32, (C, C), 1)
    tri = (row <= col) if rev else (row >= col)
    tri_b = jnp.where(tri, 1.0, 0.0).astype(BF16)
    hi = lf.astype(BF16)
    lo = (lf - hi.astype(F32)).astype(BF16)
    cum = _dot(tri_b, hi) + _dot(tri_b, lo)
    cm = cum[HGRN_MID:HGRN_MID + 1]
    last = cum[0:1] if rev else cum[C - 1:C]
    worst = jnp.max(jnp.abs(cum - cm))

    @pl.when(worst <= HGRN_SAFE_EXP)
    def _fast():
        qt = (q * jnp.exp(cum - cm)).astype(BF16)
        kt = (kk * jnp.exp(cm - cum)).astype(BF16)
        qi = (q * jnp.exp(cum)).astype(BF16)
        ki = (kk * jnp.exp(last - cum)).astype(BF16)
        dl = jnp.exp(last)
        for h in range(N_HEADS):
            hs = slice(h * HEAD_DIM, (h + 1) * HEAD_DIM)
            a = jnp.where(tri, _dot_nt(qt[:, hs], kt[:, hs]), 0.0).astype(BF16)
            st = st_ref[h]
            os_ref[:, hs] = _dot(a, v[:, hs]) + _dot_nt(qi[:, hs], st.astype(BF16))
            st_ref[h] = st * dl[:, hs] + _dot_tn(v[:, hs], ki[:, hs])

    @pl.when(worst > HGRN_SAFE_EXP)
    def _slow():
        qs_ref[...] = q
        ks_ref[...] = kk
        fs_ref[...] = fg
        vs_ref[...] = v.astype(F32)

        def step(i, carry):
            t = (C - 1 - i) if rev else i
            qr = qs_ref[pl.ds(t, 1), :]
            kr = ks_ref[pl.ds(t, 1), :]
            fr = fs_ref[pl.ds(t, 1), :]
            vr = vs_ref[pl.ds(t, 1), :]
            outs = []
            for h in range(N_HEADS):
                hs = slice(h * HEAD_DIM, (h + 1) * HEAD_DIM)
                k8 = jnp.broadcast_to(kr[:, hs], (SUBLANES, HEAD_DIM))
                v8 = jnp.broadcast_to(vr[:, hs], (SUBLANES, HEAD_DIM)) * (1.0 / SUBLANES)
                st = st_ref[h] * fr[:, hs] + _dot_tn(v8, k8)
                st_ref[h] = st
                q8 = jnp.broadcast_to(qr[:, hs], (SUBLANES, HEAD_DIM))
                outs.append(_dot_nt(q8, st)[0:1])
            os_ref[pl.ds(t, 1), :] = jnp.concatenate(outs, axis=1)
            return carry

        lax.fori_loop(0, C, step, 0)

    if rev:
        o = os_ref[...] + part_ref[0]
        g = g_ref[0].astype(F32)
        nw = nw_ref[...]
        outs = []
        for h in range(N_HEADS):
            hs = slice(h * HEAD_DIM, (h + 1) * HEAD_DIM)
            outs.append(_rms(o[:, hs], nw))
        o_ref[0] = (jnp.concatenate(outs, axis=1) * _silu(g)).astype(BF16)
    else:
        o_ref[0] = os_ref[...]


def _hgrn(p3, lb_raw, norm_w, batch, seq, layer):
    C = HGRN_C
    nc = seq // C
    scratch = [pltpu.VMEM((N_HEADS, HEAD_DIM, HEAD_DIM), F32)] + [pltpu.VMEM((C, GROUP_W), F32)] * 5

    def grp(g, rev):
        if rev:
            return pl.BlockSpec((1, C, GROUP_W), lambda b, j: (b, nc - 1 - j, g))
        return pl.BlockSpec((1, C, GROUP_W), lambda b, j: (b, j, g))

    part = pl.pallas_call(
        functools.partial(_hgrn_kernel, layer=layer, rev=False),
        grid=(batch, nc),
        in_specs=[grp(P_BQ, False), grp(P_BF_FW, False), grp(P_BI, False), _const_spec((DEPTH, GROUP_W))],
        out_specs=pl.BlockSpec((1, C, GROUP_W), lambda b, j: (b, j, 0)),
        out_shape=jax.ShapeDtypeStruct((batch, seq, GROUP_W), F32),
        scratch_shapes=scratch,
        compiler_params=_cparams(2),
        name="hgrn_fw",
    )(p3, p3, p3, lb_raw[:, 0])
    return pl.pallas_call(
        functools.partial(_hgrn_kernel, layer=layer, rev=True),
        grid=(batch, nc),
        in_specs=[grp(P_BQ, True), grp(P_BF_BW, True), grp(P_BI, True), _const_spec((DEPTH, GROUP_W)),
                  pl.BlockSpec((1, C, GROUP_W), lambda b, j: (b, nc - 1 - j, 0)),
                  grp(P_BG, True), _const_spec((1, HEAD_DIM))],
        out_specs=pl.BlockSpec((1, C, GROUP_W), lambda b, j: (b, nc - 1 - j, 0)),
        out_shape=jax.ShapeDtypeStruct((batch, seq, GROUP_W), BF16),
        scratch_shapes=scratch,
        compiler_params=_cparams(2),
        name="hgrn_bw",
    )(p3, p3, p3, lb_raw[:, 1], part, p3, norm_w)


def _log_sigmoid(x):
    return jnp.minimum(x, 0.0) - jnp.log(1.0 + jnp.exp(-jnp.abs(x)))


def _ret_kernel(*refs, rev):
    if rev:
        q_ref, k_ref, v_ref, dec_ref, part_ref, g_ref, nw_ref, o_ref, st_ref = refs
    else:
        q_ref, k_ref, v_ref, dec_ref, o_ref, st_ref = refs
    C = RET_C

    @pl.when(pl.program_id(1) == 0)
    def _():
        st_ref[...] = jnp.zeros_like(st_ref)

    lg_f = _log_sigmoid(dec_ref[0:1, :])
    lg_b = _log_sigmoid(dec_ref[1:2, :])
    lg = lg_b if rev else lg_f
    t = lax.broadcasted_iota(jnp.int32, (C, 1), 0).astype(F32)
    q = q_ref[0].astype(F32)
    k = k_ref[0].astype(F32) * HEAD_DIM ** -0.5
    v = v_ref[0]
    if rev:
        xi = jnp.exp((C - t) * lg)
        zeta = jnp.exp(t * lg)
    else:
        xi = jnp.exp((t + 1.0) * lg)
        zeta = jnp.exp((C - 1.0 - t) * lg)
    cd = jnp.exp(C * lg)
    qx = (q * xi).astype(BF16)
    kz = (k * zeta).astype(BF16)
    if not rev:
        qb = q.astype(BF16)
        kb = k.astype(BF16)
        row = lax.broadcasted_iota(jnp.int32, (C, C), 0)
        col = lax.broadcasted_iota(jnp.int32, (C, C), 1)
        d_ts = (row - col).astype(F32)
    outs = []
    for h in range(N_HEADS):
        hs = slice(h * HEAD_DIM, (h + 1) * HEAD_DIM)
        st = st_ref[h]
        o = _dot(qx[:, hs], st.astype(BF16))
        if not rev:
            lf1 = lg_f[:, h * HEAD_DIM:h * HEAD_DIM + 1]
            lb1 = lg_b[:, h * HEAD_DIM:h * HEAD_DIM + 1]
            dm = (jnp.where(d_ts >= 0, jnp.exp(jnp.maximum(d_ts, 0.0) * lf1), 0.0)
                  + jnp.where(d_ts <= 0, jnp.exp(jnp.maximum(-d_ts, 0.0) * lb1), 0.0))
            sc = (_dot_nt(qb[:, hs], kb[:, hs]) * dm).astype(BF16)
            o = o + _dot(sc, v[:, hs])
        st_ref[h] = st * cd[:, h * HEAD_DIM:h * HEAD_DIM + 1] + _dot_tn(kz[:, hs], v[:, hs])
        outs.append(o)
    if rev:
        part = part_ref[0]
        nw = nw_ref[...]
        g = g_ref[0].astype(F32)
        normed = [_rms(outs[h] + part[:, h * HEAD_DIM:(h + 1) * HEAD_DIM], nw) for h in range(N_HEADS)]
        o_ref[0] = (jnp.concatenate(normed, axis=1) * _silu(g)).astype(BF16)
    else:
        o_ref[0] = jnp.concatenate(outs, axis=1)


def _retention(p3, dec_lanes, norm_w, batch, seq):
    C = RET_C
    nc = seq // C
    scratch = [pltpu.VMEM((N_HEADS, HEAD_DIM, HEAD_DIM), F32)]

    def grp(g, rev):
        if rev:
            return pl.BlockSpec((1, C, GROUP_W), lambda b, j: (b, nc - 1 - j, g))
        return pl.BlockSpec((1, C, GROUP_W), lambda b, j: (b, j, g))

    part = pl.pallas_call(
        functools.partial(_ret_kernel, rev=False),
        grid=(batch, nc),
        in_specs=[grp(P_DQ, False), grp(P_DK, False), grp(P_DV, False), _const_spec((2, GROUP_W))],
        out_specs=pl.BlockSpec((1, C, GROUP_W), lambda b, j: (b, j, 0)),
        out_shape=jax.ShapeDtypeStruct((batch, seq, GROUP_W), F32),
        scratch_shapes=scratch,
        compiler_params=_cparams(2),
        name="ret_fw",
    )(p3, p3, p3, dec_lanes)
    return pl.pallas_call(
        functools.partial(_ret_kernel, rev=True),
        grid=(batch, nc),
        in_specs=[grp(P_DQ, True), grp(P_DK, True), grp(P_DV, True), _const_spec((2, GROUP_W)),
                  pl.BlockSpec((1, C, GROUP_W), lambda b, j: (b, nc - 1 - j, 0)),
                  grp(P_DG, True), _const_spec((1, HEAD_DIM))],
        out_specs=pl.BlockSpec((1, C, GROUP_W), lambda b, j: (b, nc - 1 - j, 0)),
        out_shape=jax.ShapeDtypeStruct((batch, seq, GROUP_W), BF16),
        scratch_shapes=scratch,
        compiler_params=_cparams(2),
        name="ret_bw",
    )(p3, p3, p3, dec_lanes, part, p3, norm_w)


def _rglru_kernel(*refs, rev, nblk):
    if rev:
        (x_ref, xp_ref, xn_ref, cw_ref, cb_ref, wg_ref, bg_ref, lam_ref, part_ref, gate_ref,
         o_ref, carry_ref, a_ref, u_ref, h_ref) = refs
    else:
        (x_ref, xp_ref, xn_ref, cw_ref, cb_ref, wg_ref, bg_ref, lam_ref,
         o_ref, carry_ref, a_ref, u_ref, h_ref) = refs
    TB = RG_TB
    j = pl.program_id(1)
    blk = (nblk - 1 - j) if rev else j

    @pl.when(j == 0)
    def _():
        carry_ref[...] = jnp.zeros_like(carry_ref)

    prev = jnp.where(blk > 0, xp_ref[0].astype(F32), 0.0)
    nxt = jnp.where(blk < nblk - 1, xn_ref[0].astype(F32), 0.0)
    xe = jnp.concatenate([prev, x_ref[0].astype(F32), nxt], axis=0)
    cw = cw_ref[...]
    xc = cb_ref[...] + sum(
        xe[RG_HALO - CONV_PAD_L + w:RG_HALO - CONV_PAD_L + w + TB] * cw[w:w + 1] for w in range(CONV_W))
    gates = _dot(xc.astype(BF16), wg_ref[...]) + bg_ref[...]
    r = _sigmoid(gates[:, :GROUP_W])
    ig = _sigmoid(gates[:, GROUP_W:])
    log_a = -RG_C * r * _softplus(-lam_ref[...])
    a = jnp.exp(log_a)
    u = jnp.sqrt(jnp.maximum(1.0 - jnp.exp(2.0 * log_a), 0.0)) * (ig * xc)

    pos = lax.broadcasted_iota(jnp.int32, (TB, GROUP_W), 0) % SUBLANES
    for d in (1, 2, 4):
        if rev:
            keep = pos < SUBLANES - d
            shift = TB - d
        else:
            keep = pos >= d
            shift = d
        a_s = jnp.where(keep, pltpu.roll(a, shift, 0), 1.0)
        u_s = jnp.where(keep, pltpu.roll(u, shift, 0), 0.0)
        u = a * u_s + u
        a = a * a_s
    a_ref[...] = a
    u_ref[...] = u
    ntile = TB // SUBLANES

    def tile_step(i, hprev):
        ti = (ntile - 1 - i) if rev else i
        r0 = pl.multiple_of(ti * SUBLANES, SUBLANES)
        h = a_ref[pl.ds(r0, SUBLANES), :] * hprev + u_ref[pl.ds(r0, SUBLANES), :]
        h_ref[pl.ds(r0, SUBLANES), :] = h
        edge = h[0:1] if rev else h[SUBLANES - 1:SUBLANES]
        return jnp.broadcast_to(edge, (SUBLANES, GROUP_W))

    carry_ref[...] = lax.fori_loop(0, ntile, tile_step, carry_ref[...])
    if rev:
        hs = h_ref[...] + part_ref[0]
        o_ref[0] = (hs * jax.nn.gelu(gate_ref[0].astype(F32), approximate=True)).astype(BF16)
    else:
        o_ref[0] = h_ref[...]


def _rglru(p3, conv_w, conv_b, wg, bg, lam, batch, seq):
    TB = RG_TB
    nblk = seq // TB
    hpb = TB // RG_HALO
    nhalo = seq // RG_HALO
    scratch = [pltpu.VMEM((SUBLANES, GROUP_W), F32)] + [pltpu.VMEM((TB, GROUP_W), F32)] * 3

    def specs(rev):
        def blk(j):
            return (nblk - 1 - j) if rev else j
        return [
            pl.BlockSpec((1, TB, GROUP_W), lambda b, j: (b, blk(j), P_CX)),
            pl.BlockSpec((1, RG_HALO, GROUP_W), lambda b, j: (b, jnp.maximum(blk(j) * hpb - 1, 0), P_CX)),
            pl.BlockSpec((1, RG_HALO, GROUP_W),
                         lambda b, j: (b, jnp.minimum((blk(j) + 1) * hpb, nhalo - 1), P_CX)),
            _const_spec((CONV_W, GROUP_W)),
            _const_spec((1, GROUP_W)),
            _const_spec((GROUP_W, 2 * GROUP_W)),
            _const_spec((1, 2 * GROUP_W)),
            _const_spec((1, GROUP_W)),
        ]

    part = pl.pallas_call(
        functools.partial(_rglru_kernel, rev=False, nblk=nblk),
        grid=(batch, nblk),
        in_specs=specs(False),
        out_specs=pl.BlockSpec((1, TB, GROUP_W), lambda b, j: (b, j, 0)),
        out_shape=jax.ShapeDtypeStruct((batch, seq, GROUP_W), F32),
        scratch_shapes=scratch,
        compiler_params=_cparams(2),
        name="rglru_fw",
    )(p3, p3, p3, conv_w, conv_b, wg[0], bg[0], lam[0:1])
    return pl.pallas_call(
        functools.partial(_rglru_kernel, rev=True, nblk=nblk),
        grid=(batch, nblk),
        in_specs=specs(True) + [
            pl.BlockSpec((1, TB, GROUP_W), lambda b, j: (b, nblk - 1 - j, 0)),
            pl.BlockSpec((1, TB, GROUP_W), lambda b, j: (b, nblk - 1 - j, P_CG)),
        ],
        out_specs=pl.BlockSpec((1, TB, GROUP_W), lambda b, j: (b, nblk - 1 - j, 0)),
        out_shape=jax.ShapeDtypeStruct((batch, seq, GROUP_W), BF16),
        scratch_shapes=scratch,
        compiler_params=_cparams(2),
        name="rglru_bw",
    )(p3, p3, p3, conv_w, conv_b, wg[1], bg[1], lam[1:2], part, p3)


def _post_kernel(oa_ref, ob_ref, oc_ref, od_ref, r_ref, p_ref, nw_ref, wo_ref, wig_ref, wiu_ref, wfo_ref,
                 wpg_ref, wpp_ref, out_ref):
    nw = nw_ref[...]
    mixed_in = jnp.concatenate([oa_ref[...], ob_ref[...], oc_ref[...], od_ref[...]], axis=1)
    r = r_ref[...] + _rms(_dot(mixed_in, wo_ref[...]), nw[1:2])
    h2 = _rms(r, nw[2:3]).astype(BF16)
    ff = jnp.zeros_like(r)
    for c in range(N_FF_CHUNKS):
        act = _silu(_dot(h2, wig_ref[c])) * _dot(h2, wiu_ref[c])
        ff = ff + _dot(act.astype(BF16), wfo_ref[c])
    r = r + _rms(ff, nw[3:4])
    gate = _sigmoid(_dot(r.astype(BF16), wpg_ref[...]))
    out_ref[...] = r + gate * _dot(p_ref[0].astype(BF16), wpp_ref[...])


def _post(oa, ob, oc, od, r, p, layer, nw, wo, wig, wiu, wfo, wpg, wpp):
    tokens = r.shape[0]
    tm = TM_POST
    mix_spec = pl.BlockSpec((tm, GROUP_W), lambda i: (i, 0))
    return pl.pallas_call(
        _post_kernel,
        grid=(tokens // tm,),
        in_specs=[
            mix_spec, mix_spec, mix_spec, mix_spec,
            pl.BlockSpec((tm, D_MODEL), lambda i: (i, 0)),
            pl.BlockSpec((1, tm, PLE_DIM), lambda i: (layer, i, 0)),
            _const_spec((4, D_MODEL)),
            _const_spec((D_MODEL, D_MODEL)),
            _const_spec((N_FF_CHUNKS, D_MODEL, FF_CHUNK)),
            _const_spec((N_FF_CHUNKS, D_MODEL, FF_CHUNK)),
            _const_spec((N_FF_CHUNKS, FF_CHUNK, D_MODEL)),
            _const_spec((D_MODEL, D_MODEL)),
            _const_spec((PLE_DIM, D_MODEL)),
        ],
        out_specs=pl.BlockSpec((tm, D_MODEL), lambda i: (i, 0)),
        out_shape=jax.ShapeDtypeStruct((tokens, D_MODEL), F32),
        compiler_params=_cparams(1),
        name="post",
    )(oa, ob, oc, od, r, p, nw, wo, wig, wiu, wfo, wpg, wpp)


def _block_diag(w):
    out = jnp.zeros((GROUP_W, GROUP_W), w.dtype)
    for h in range(N_HEADS):
        out = out.at[h * HEAD_DIM:(h + 1) * HEAD_DIM, h * HEAD_DIM:(h + 1) * HEAD_DIM].set(w[h])
    return out


def _prep_weights(norm_w, w_in, diff_lambda, diff_subln_w, hgrn_lb_raw, hgrn_norm_w, rg_conv_w, rg_conv_b,
                  rg_w_a, rg_b_a, rg_w_x, rg_b_x, rg_lambda, ret_decay, ret_norm_w, w_out, w_ffn_in,
                  w_ffn_out, w_ple_gate, w_ple_proj):
    layers = []
    for l in range(DEPTH):
        wi = w_in[l]
        w_nat = jnp.concatenate([wi[:, GROUP_W:2 * GROUP_W], wi[:, 3 * GROUP_W:]], axis=1).astype(BF16)
        w_qv_t = jnp.concatenate([wi[:, :GROUP_W], wi[:, 2 * GROUP_W:3 * GROUP_W]], axis=1).T.astype(BF16)
        wg = [jnp.concatenate([_block_diag(rg_w_a[l, d]), _block_diag(rg_w_x[l, d])], axis=1).astype(BF16)
              for d in range(2)]
        bg = [jnp.concatenate([rg_b_a[l, d], rg_b_x[l, d]])[None, :] for d in range(2)]
        wfi = w_ffn_in[l].astype(BF16)
        layers.append(dict(
            nw=norm_w[l], nw0=norm_w[l, 0:1], w_nat=w_nat, w_qv_t=w_qv_t,
            dl=diff_lambda[l], sw=diff_subln_w[l][:, None],
            lb_raw=hgrn_lb_raw, hgrn_nw=hgrn_norm_w[l][None, :],
            conv_w=rg_conv_w[l], conv_b=rg_conv_b[l][None, :], wg=wg, bg=bg, lam=rg_lambda[l],
            dec=jnp.repeat(ret_decay[l], HEAD_DIM, axis=-1), ret_nw=ret_norm_w[l][None, :],
            wo=w_out[l].astype(BF16),
            wig=wfi[:, :D_FF].reshape(D_MODEL, N_FF_CHUNKS, FF_CHUNK).transpose(1, 0, 2),
            wiu=wfi[:, D_FF:].reshape(D_MODEL, N_FF_CHUNKS, FF_CHUNK).transpose(1, 0, 2),
            wfo=w_ffn_out[l].astype(BF16).reshape(N_FF_CHUNKS, FF_CHUNK, D_MODEL),
            wpg=w_ple_gate[l].astype(BF16), wpp=w_ple_proj[l].astype(BF16),
        ))
    return layers


def _trunk(x, p, layers):
    batch, seq, _ = x.shape
    tokens = batch * seq
    r = x.reshape(tokens, D_MODEL)
    p = p.reshape(DEPTH, tokens, PLE_DIM)
    for l, w in enumerate(layers):
        lambda_init = 0.8 - 0.6 * math.exp(-0.3 * l)
        pn, qt, vt = _inproj(r, w["nw0"], w["w_nat"], w["w_qv_t"], batch, seq)
        p3 = pn.reshape(batch, seq, N_PGROUPS * GROUP_W)
        oa = _attention(qt, p3, vt, w["dl"], w["sw"], batch, seq, lambda_init)
        ob = _hgrn(p3, w["lb_raw"], w["hgrn_nw"], batch, seq, l)
        oc = _rglru(p3, w["conv_w"], w["conv_b"], w["wg"], w["bg"], w["lam"], batch, seq)
        od = _retention(p3, w["dec"], w["ret_nw"], batch, seq)
        flat = lambda o: o.reshape(tokens, GROUP_W)
        r = _post(flat(oa), flat(ob), flat(oc), flat(od), r, p, l, w["nw"], w["wo"], w["wig"], w["wiu"],
                  w["wfo"], w["wpg"], w["wpp"])
    return r.reshape(batch, seq, D_MODEL)


def kernel(x_prompt, x_sample, p_prompt, p_sample, norm_w, w_in, diff_lambda, diff_subln_w, hgrn_lb_raw,
           hgrn_norm_w, rg_conv_w, rg_conv_b, rg_w_a, rg_b_a, rg_w_x, rg_b_x, rg_lambda, ret_decay,
           ret_norm_w, w_out, w_ffn_in, w_ffn_out, w_ple_gate, w_ple_proj):
    layers = _prep_weights(norm_w, w_in, diff_lambda, diff_subln_w, hgrn_lb_raw, hgrn_norm_w, rg_conv_w,
                           rg_conv_b, rg_w_a, rg_b_a, rg_w_x, rg_b_x, rg_lambda, ret_decay, ret_norm_w,
                           w_out, w_ffn_in, w_ffn_out, w_ple_gate, w_ple_proj)
    return (_trunk(x_prompt, p_prompt, layers), _trunk(x_sample, p_sample, layers))
```

```python
import functools
import math

import jax
import jax.numpy as jnp
from jax import lax
from jax.experimental import pallas as pl
from jax.experimental.pallas import tpu as pltpu

F32 = jnp.float32
BF16 = jnp.bfloat16

D_MODEL = 1024
DEPTH = 4
GROUP_W = 256
N_HEADS = 4
HEAD_DIM = 64
DIFF_HALF = 32
D_FF = 2816
FF_CHUNK = 256
N_FF_CHUNKS = D_FF // FF_CHUNK
PLE_DIM = 256
CONV_W = 4
CONV_PAD_L = 2
RG_C = 8.0
EPS = 1e-6
LOG2E = 1.4426950408889634

(P_AK, P_BQ, P_BF_FW, P_BF_BW, P_BI, P_BG, P_CX, P_CG, P_DQ, P_DK, P_DV, P_DG) = range(12)
N_PGROUPS = 12

VMEM_LIMIT_V7X = 56 * 1024 * 1024

TM_INPROJ = 512
TM_POST = 512
TQ_ATTN = 256
TK_ATTN = 256
ATTN_SKEW = 3
ATTN_SLOTS = ATTN_SKEW + 1
ATTN_UNROLL = 4
HGRN_C = 64
HGRN_NB = 4
HGRN_MID = HGRN_C // 2
HGRN_SAFE_EXP = 80.0
RET_C = 256
RG_TB = 256
RG_HALO = 16
SUBLANES = 8


def _cparams(n_axes):
    return pltpu.CompilerParams(dimension_semantics=("arbitrary",) * n_axes,
                                vmem_limit_bytes=VMEM_LIMIT_V7X)


def _const_spec(shape):
    nd = len(shape)
    return pl.BlockSpec(shape, lambda *_: (0,) * nd, pipeline_mode=pl.Buffered(1))


def _rms(x, w):
    return x * lax.rsqrt(jnp.mean(x * x, axis=-1, keepdims=True) + EPS) * w


def _sigmoid(x):
    return 1.0 / (1.0 + jnp.exp(-x))


def _silu(x):
    return x * _sigmoid(x)


def _softplus(x):
    return jnp.maximum(x, 0.0) + jnp.log(1.0 + jnp.exp(-jnp.abs(x)))


def _dot(a, b):
    return jnp.dot(a, b, preferred_element_type=F32)


def _dot_nt(a, b):
    return lax.dot_general(a, b, (((1,), (1,)), ((), ())), preferred_element_type=F32)


def _dot_tn(a, b):
    return lax.dot_general(a, b, (((0,), (0,)), ((), ())), preferred_element_type=F32)


def _inproj_kernel(x_ref, nw_ref, wn_ref, wt_ref, p_ref, qt_ref, vt_ref):
    x = x_ref[...]
    y = _rms(x, nw_ref[...]).astype(BF16)
    for g in range(N_PGROUPS):
        cols = slice(g * GROUP_W, (g + 1) * GROUP_W)
        p_ref[:, cols] = _dot(y, wn_ref[:, cols]).astype(BF16)
    t = _dot_nt(wt_ref[...], y)
    qt_ref[0] = (t[:GROUP_W] * (DIFF_HALF ** -0.5 * LOG2E)).astype(BF16)
    vt_ref[0] = t[GROUP_W:].astype(BF16)


def _inproj(r, nw, w_nat, w_qv_t, batch, seq):
    tokens = batch * seq
    tm = TM_INPROJ
    nsb = seq // tm
    return pl.pallas_call(
        _inproj_kernel,
        grid=(tokens // tm,),
        in_specs=[
            pl.BlockSpec((tm, D_MODEL), lambda i: (i, 0)),
            _const_spec((1, D_MODEL)),
            _const_spec((D_MODEL, N_PGROUPS * GROUP_W)),
            _const_spec((2 * GROUP_W, D_MODEL)),
        ],
        out_specs=[
            pl.BlockSpec((tm, N_PGROUPS * GROUP_W), lambda i: (i, 0)),
            pl.BlockSpec((1, GROUP_W, tm), lambda i: (i // nsb, 0, i % nsb)),
            pl.BlockSpec((1, GROUP_W, tm), lambda i: (i // nsb, 0, i % nsb)),
        ],
        out_shape=[
            jax.ShapeDtypeStruct((tokens, N_PGROUPS * GROUP_W), BF16),
            jax.ShapeDtypeStruct((batch, GROUP_W, seq), BF16),
            jax.ShapeDtypeStruct((batch, GROUP_W, seq), BF16),
        ],
        compiler_params=_cparams(1),
        name="inproj",
    )(r, nw, w_nat, w_qv_t)


def _attn_kernel(qt_ref, k_ref, vt_ref, dl_ref, sw_ref, o_ref, w_ref, m_ref, acc_ref, s_ref, t_ref,
                 *, seq, lambda_init):
    tq, tk = TQ_ATTN, TK_ATTN
    assert tq == tk
    q0 = pl.program_id(1) * tq
    nk = seq // tk
    n_chain = 2 * N_HEADS
    sub = lax.broadcasted_iota(jnp.int32, (2 * HEAD_DIM, tq), 0)
    for h in range(N_HEADS):
        half = h // 2
        qt_half = qt_ref[0, half * 2 * HEAD_DIM:(half + 1) * 2 * HEAD_DIM, :]
        for c in range(2):
            lo = (h % 2) * HEAD_DIM + c * DIFF_HALF
            w_ref[2 * h + c] = jnp.where((sub >= lo) & (sub < lo + DIFF_HALF), qt_half,
                                         jnp.zeros_like(qt_half))
    m_ref[...] = jnp.full(m_ref.shape, -1e30, F32)
    acc_ref[...] = jnp.zeros_like(acc_ref)
    ones = jnp.ones((2 * SUBLANES, tk), BF16)
    slopes2 = [2.0 ** (-8.0 * (h + 1) / N_HEADS) * LOG2E for h in range(N_HEADS)]

    @pl.when((pl.program_id(0) == 0) & (pl.program_id(1) == 0))
    def _():
        row = lax.broadcasted_iota(jnp.int32, (tk, tq), 0)
        col = lax.broadcasted_iota(jnp.int32, (tk, tq), 1)
        d = (row - col).astype(F32)
        for h in range(N_HEADS):
            t_ref[h, 0] = slopes2[h] * d
            t_ref[h, 1] = -slopes2[h] * jnp.abs(d)
            t_ref[h, 2] = -slopes2[h] * d

    def scores(jj, i):
        k0 = pl.multiple_of(jj * tk, tk)
        half = i // 4
        kt = k_ref[0, pl.ds(k0, tk), half * 2 * HEAD_DIM:(half + 1) * 2 * HEAD_DIM]
        s_ref[i % ATTN_SLOTS] = _dot(kt, w_ref[i])

    def consume(j, i, sel, off):
        h = i // 2
        k0 = pl.multiple_of(j * tk, tk)
        u = s_ref[i % ATTN_SLOTS] + t_ref[h, sel]
        m = m_ref[i]
        m_new = jnp.maximum(m, jnp.max(u, axis=0, keepdims=True) - off[h])
        p = jnp.exp2(u - (m_new + off[h])).astype(BF16)
        vt = jnp.concatenate([vt_ref[0, h * HEAD_DIM:(h + 1) * HEAD_DIM, pl.ds(k0, tk)], ones], axis=0)
        acc_ref[i] = acc_ref[i] * jnp.exp2(m - m_new) + _dot(vt, p)
        m_ref[i] = m_new

    for i in range(ATTN_SKEW):
        scores(0, i)

    def body(jo, carry):
        qi = pl.program_id(1)
        for ju in range(ATTN_UNROLL):
            j = jo * ATTN_UNROLL + ju
            jn = jnp.minimum(j + 1, nk - 1)
            sel = (j >= qi).astype(jnp.int32) + (j > qi).astype(jnp.int32)
            gap = jnp.abs(q0 - j * tk).astype(F32)
            off = [slopes2[h] * gap for h in range(N_HEADS)]
            for i in range(n_chain):
                a = i + ATTN_SKEW
                if a < n_chain:
                    scores(j, a)
                else:
                    scores(jn, a - n_chain)
                consume(j, i, sel, off)
        return carry

    lax.fori_loop(0, nk // ATTN_UNROLL, body, 0)
    dl = dl_ref[...]
    lam = (jnp.exp(jnp.sum(dl[0:1] * dl[1:2], axis=1, keepdims=True))
           - jnp.exp(jnp.sum(dl[2:3] * dl[3:4], axis=1, keepdims=True)) + lambda_init)
    outs = []
    for h in range(N_HEADS):
        a1 = acc_ref[2 * h]
        a2 = acc_ref[2 * h + 1]
        o = (a1[:HEAD_DIM] / a1[HEAD_DIM:HEAD_DIM + 1]
             - lam * (a2[:HEAD_DIM] / a2[HEAD_DIM:HEAD_DIM + 1]))
        ms = jnp.mean(o * o, axis=0, keepdims=True)
        outs.append(o * lax.rsqrt(ms + EPS) * sw_ref[...] * (1.0 - lambda_init))
    o_ref[0] = jnp.concatenate(outs, axis=0).T.astype(BF16)


def _attention(qt, p3, vt, dl, sw, batch, seq, lambda_init):
    tq = TQ_ATTN
    n_chain = 2 * N_HEADS
    scratch = [
        pltpu.VMEM((n_chain, 2 * HEAD_DIM, tq), BF16),
        pltpu.VMEM((n_chain, 1, tq), F32),
        pltpu.VMEM((n_chain, HEAD_DIM + 2 * SUBLANES, tq), F32),
        pltpu.VMEM((ATTN_SLOTS, TK_ATTN, tq), F32),
        pltpu.VMEM((N_HEADS, 3, TK_ATTN, tq), F32),
    ]
    return pl.pallas_call(
        functools.partial(_attn_kernel, seq=seq, lambda_init=lambda_init),
        grid=(batch, seq // tq),
        in_specs=[
            pl.BlockSpec((1, GROUP_W, tq), lambda b, i: (b, 0, i)),
            pl.BlockSpec((1, seq, GROUP_W), lambda b, i: (b, 0, P_AK), pipeline_mode=pl.Buffered(1)),
            pl.BlockSpec((1, GROUP_W, seq), lambda b, i: (b, 0, 0), pipeline_mode=pl.Buffered(1)),
            _const_spec((4, DIFF_HALF)),
            _const_spec((HEAD_DIM, 1)),
        ],
        out_specs=pl.BlockSpec((1, tq, GROUP_W), lambda b, i: (b, i, 0)),
        out_shape=jax.ShapeDtypeStruct((batch, seq, GROUP_W), BF16),
        scratch_shapes=scratch,
        compiler_params=_cparams(2),
        name="diff_attn",
    )(qt, p3, vt, dl, sw)


def _hgrn_kernel(*refs, layer, rev):
    if rev:
        (q_ref, f_ref, v_ref, lbraw_ref, part_ref, g_ref, nw_ref, o_ref,
         st_ref, qs_ref, ks_ref, fs_ref, vs_ref, os_ref) = refs
    else:
        (q_ref, f_ref, v_ref, lbraw_ref, o_ref,
         st_ref, qs_ref, ks_ref, fs_ref, vs_ref, os_ref) = refs
    C = HGRN_C
    T = HGRN_NB * C

    @pl.when(pl.program_id(1) == 0)
    def _():
        st_ref[...] = jnp.zeros_like(st_ref)

    raw = lbraw_ref[...]
    e = jnp.exp(raw - jnp.max(raw, axis=0, keepdims=True))
    soft = e / jnp.sum(e, axis=0, keepdims=True)
    lb = jnp.zeros((1, GROUP_W), F32)
    for i in range(1, layer + 1):
        lb = lb + soft[i:i + 1]

    xf = f_ref[0].astype(F32)
    fg = lb + (1.0 - lb) * _sigmoid(xf)
    kk = (1.0 - lb) * _sigmoid(-xf)
    q = _silu(q_ref[0].astype(F32))
    v = v_ref[0]
    lf = jnp.log(fg)

    row = lax.broadcasted_iota(jnp.int32, (T, T), 0)
    col = lax.broadcasted_iota(jnp.int32, (T, T), 1)
    same = (row // C) == (col // C)
    tri = same & ((row <= col) if rev else (row >= col))
    tri_b = jnp.where(tri, 1.0, 0.0).astype(BF16)
    hi = lf.astype(BF16)
    lo = (lf - hi.astype(F32)).astype(BF16)
    cum = _dot(tri_b, hi) + _dot(tri_b, lo)

    def chunk_rows(idx):
        return jnp.concatenate(
            [jnp.broadcast_to(cum[c * C + idx:c * C + idx + 1], (C, GROUP_W)) for c in range(HGRN_NB)], axis=0)

    cm = chunk_rows(HGRN_MID)
    last = chunk_rows(0 if rev else C - 1)
    worst = jnp.max(jnp.abs(cum - cm))

    @pl.when(worst <= HGRN_SAFE_EXP)
    def _fast():
        qt = (q * jnp.exp(cum - cm)).astype(BF16)
        kt = (kk * jnp.exp(cm - cum)).astype(BF16)
        qi = (q * jnp.exp(cum)).astype(BF16)
        ki = (kk * jnp.exp(last - cum)).astype(BF16)
        dl = jnp.exp(last)
        for h in range(N_HEADS):
            hs = slice(h * HEAD_DIM, (h + 1) * HEAD_DIM)
            a = jnp.where(tri, _dot_nt(qt[:, hs], kt[:, hs]), 0.0).astype(BF16)
            os_ref[:, hs] = _dot(a, v[:, hs])
        st = st_ref[...]
        for c in (reversed(range(HGRN_NB)) if rev else range(HGRN_NB)):
            rs = slice(c * C, (c + 1) * C)
            os_ref[rs, :] = os_ref[rs, :] + _dot_nt(qi[rs], st.astype(BF16))
            st = st * dl[c * C:c * C + 1] + jnp.where(same, _dot_tn(v[rs], ki[rs]), 0.0)
        st_ref[...] = st

    @pl.when(jnp.logical_not(worst <= HGRN_SAFE_EXP))
    def _slow():
        qs_ref[...] = q
        ks_ref[...] = kk
        fs_ref[...] = fg
        vs_ref[...] = v.astype(F32)

        def step(i, carry):
            t = (T - 1 - i) if rev else i
            k8 = jnp.broadcast_to(ks_ref[pl.ds(t, 1), :], (SUBLANES, GROUP_W))
            v8 = jnp.broadcast_to(vs_ref[pl.ds(t, 1), :], (SUBLANES, GROUP_W)) * (1.0 / SUBLANES)
            st = st_ref[...] * fs_ref[pl.ds(t, 1), :] + jnp.where(same, _dot_tn(v8, k8), 0.0)
            st_ref[...] = st
            q8 = jnp.broadcast_to(qs_ref[pl.ds(t, 1), :], (SUBLANES, GROUP_W))
            os_ref[pl.ds(t, 1), :] = _dot_nt(q8, st)[0:1]
            return carry

        lax.fori_loop(0, T, step, 0)

    if rev:
        o = os_ref[...] + part_ref[0]
        g = g_ref[0].astype(F32)
        nw = nw_ref[...]
        outs = []
        for h in range(N_HEADS):
            hs = slice(h * HEAD_DIM, (h + 1) * HEAD_DIM)
            outs.append(_rms(o[:, hs], nw))
        o_ref[0] = (jnp.concatenate(outs, axis=1) * _silu(g)).astype(BF16)
    else:
        o_ref[0] = os_ref[...]


def _hgrn(p3, lb_raw, norm_w, batch, seq, layer):
    C = HGRN_NB * HGRN_C
    assert C == GROUP_W
    nc = seq // C
    scratch = [pltpu.VMEM((GROUP_W, GROUP_W), F32)] + [pltpu.VMEM((C, GROUP_W), F32)] * 5

    def grp(g, rev):
        if rev:
            return pl.BlockSpec((1, C, GROUP_W), lambda b, j: (b, nc - 1 - j, g))
        return pl.BlockSpec((1, C, GROUP_W), lambda b, j: (b, j, g))

    part = pl.pallas_call(
        functools.partial(_hgrn_kernel, layer=layer, rev=False),
        grid=(batch, nc),
        in_specs=[grp(P_BQ, False), grp(P_BF_FW, False), grp(P_BI, False), _const_spec((DEPTH, GROUP_W))],
        out_specs=pl.BlockSpec((1, C, GROUP_W), lambda b, j: (b, j, 0)),
        out_shape=jax.ShapeDtypeStruct((batch, seq, GROUP_W), F32),
        scratch_shapes=scratch,
        compiler_params=_cparams(2),
        name="hgrn_fw",
    )(p3, p3, p3, lb_raw[:, 0])
    return pl.pallas_call(
        functools.partial(_hgrn_kernel, layer=layer, rev=True),
        grid=(batch, nc),
        in_specs=[grp(P_BQ, True), grp(P_BF_BW, True), grp(P_BI, True), _const_spec((DEPTH, GROUP_W)),
                  pl.BlockSpec((1, C, GROUP_W), lambda b, j: (b, nc - 1 - j, 0)),
                  grp(P_BG, True), _const_spec((1, HEAD_DIM))],
        out_specs=pl.BlockSpec((1, C, GROUP_W), lambda b, j: (b, nc - 1 - j, 0)),
        out_shape=jax.ShapeDtypeStruct((batch, seq, GROUP_W), BF16),
        scratch_shapes=scratch,
        compiler_params=_cparams(2),
        name="hgrn_bw",
    )(p3, p3, p3, lb_raw[:, 1], part, p3, norm_w)


def _log_sigmoid(x):
    return jnp.minimum(x, 0.0) - jnp.log(1.0 + jnp.exp(-jnp.abs(x)))


def _ret_kernel(*refs, rev):
    if rev:
        q_ref, k_ref, v_ref, dec_ref, part_ref, g_ref, nw_ref, o_ref, st_ref = refs
    else:
        q_ref, k_ref, v_ref, dec_ref, o_ref, st_ref, dm_ref = refs
    C = RET_C

    @pl.when(pl.program_id(1) == 0)
    def _():
        st_ref[...] = jnp.zeros_like(st_ref)

    lg_f = _log_sigmoid(dec_ref[0:1, :])
    lg_b = _log_sigmoid(dec_ref[1:2, :])
    lg = lg_b if rev else lg_f
    if not rev:
        @pl.when((pl.program_id(0) == 0) & (pl.program_id(1) == 0))
        def _():
            row = lax.broadcasted_iota(jnp.int32, (C, C), 0)
            col = lax.broadcasted_iota(jnp.int32, (C, C), 1)
            d_ts = (row - col).astype(F32)
            for h in range(N_HEADS):
                lf1 = lg_f[:, h * HEAD_DIM:h * HEAD_DIM + 1]
                lb1 = lg_b[:, h * HEAD_DIM:h * HEAD_DIM + 1]
                dm_ref[h] = (jnp.where(d_ts >= 0, jnp.exp(jnp.maximum(d_ts, 0.0) * lf1), 0.0)
                             + jnp.where(d_ts <= 0, jnp.exp(jnp.maximum(-d_ts, 0.0) * lb1), 0.0))
    t = lax.broadcasted_iota(jnp.int32, (C, 1), 0).astype(F32)
    q = q_ref[0].astype(F32)
    k = k_ref[0].astype(F32) * HEAD_DIM ** -0.5
    v = v_ref[0]
    if rev:
        xi = jnp.exp((C - t) * lg)
        zeta = jnp.exp(t * lg)
    else:
        xi = jnp.exp((t + 1.0) * lg)
        zeta = jnp.exp((C - 1.0 - t) * lg)
    cd = jnp.exp(C * lg)
    qx = (q * xi).astype(BF16)
    kz = (k * zeta).astype(BF16)
    if not rev:
        qb = q.astype(BF16)
        kb = k.astype(BF16)
    outs = []
    for h in range(N_HEADS):
        hs = slice(h * HEAD_DIM, (h + 1) * HEAD_DIM)
        st = st_ref[h]
        o = _dot(qx[:, hs], st.astype(BF16))
        if not rev:
            sc = (_dot_nt(qb[:, hs], kb[:, hs]) * dm_ref[h]).astype(BF16)
            o = o + _dot(sc, v[:, hs])
        st_ref[h] = st * cd[:, h * HEAD_DIM:h * HEAD_DIM + 1] + _dot_tn(kz[:, hs], v[:, hs])
        outs.append(o)
    if rev:
        part = part_ref[0]
        nw = nw_ref[...]
        g = g_ref[0].astype(F32)
        normed = [_rms(outs[h] + part[:, h * HEAD_DIM:(h + 1) * HEAD_DIM], nw) for h in range(N_HEADS)]
        o_ref[0] = (jnp.concatenate(normed, axis=1) * _silu(g)).astype(BF16)
    else:
        o_ref[0] = jnp.concatenate(outs, axis=1)


def _retention(p3, dec_lanes, norm_w, batch, seq):
    C = RET_C
    nc = seq // C
    scratch = [pltpu.VMEM((N_HEADS, HEAD_DIM, HEAD_DIM), F32)]

    def grp(g, rev):
        if rev:
            return pl.BlockSpec((1, C, GROUP_W), lambda b, j: (b, nc - 1 - j, g))
        return pl.BlockSpec((1, C, GROUP_W), lambda b, j: (b, j, g))

    part = pl.pallas_call(
        functools.partial(_ret_kernel, rev=False),
        grid=(batch, nc),
        in_specs=[grp(P_DQ, False), grp(P_DK, False), grp(P_DV, False), _const_spec((2, GROUP_W))],
        out_specs=pl.BlockSpec((1, C, GROUP_W), lambda b, j: (b, j, 0)),
        out_shape=jax.ShapeDtypeStruct((batch, seq, GROUP_W), F32),
        scratch_shapes=scratch + [pltpu.VMEM((N_HEADS, C, C), F32)],
        compiler_params=_cparams(2),
        name="ret_fw",
    )(p3, p3, p3, dec_lanes)
    return pl.pallas_call(
        functools.partial(_ret_kernel, rev=True),
        grid=(batch, nc),
        in_specs=[grp(P_DQ, True), grp(P_DK, True), grp(P_DV, True), _const_spec((2, GROUP_W)),
                  pl.BlockSpec((1, C, GROUP_W), lambda b, j: (b, nc - 1 - j, 0)),
                  grp(P_DG, True), _const_spec((1, HEAD_DIM))],
        out_specs=pl.BlockSpec((1, C, GROUP_W), lambda b, j: (b, nc - 1 - j, 0)),
        out_shape=jax.ShapeDtypeStruct((batch, seq, GROUP_W), BF16),
        scratch_shapes=scratch,
        compiler_params=_cparams(2),
        name="ret_bw",
    )(p3, p3, p3, dec_lanes, part, p3, norm_w)


def _rglru_kernel(*refs, rev, nblk):
    if rev:
        (x_ref, xp_ref, xn_ref, cw_ref, cb_ref, wg_ref, bg_ref, lam_ref, part_ref, gate_ref,
         o_ref, carry_ref, a_ref, u_ref, h_ref) = refs
    else:
        (x_ref, xp_ref, xn_ref, cw_ref, cb_ref, wg_ref, bg_ref, lam_ref,
         o_ref, carry_ref, a_ref, u_ref, h_ref) = refs
    TB = RG_TB
    j = pl.program_id(1)
    blk = (nblk - 1 - j) if rev else j

    @pl.when(j == 0)
    def _():
        carry_ref[...] = jnp.zeros_like(carry_ref)

    prev = jnp.where(blk > 0, xp_ref[0].astype(F32), 0.0)
    nxt = jnp.where(blk < nblk - 1, xn_ref[0].astype(F32), 0.0)
    xe = jnp.concatenate([prev, x_ref[0].astype(F32), nxt], axis=0)
    cw = cw_ref[...]
    xc = cb_ref[...] + sum(
        xe[RG_HALO - CONV_PAD_L + w:RG_HALO - CONV_PAD_L + w + TB] * cw[w:w + 1] for w in range(CONV_W))
    gates = _dot(xc.astype(BF16), wg_ref[...]) + bg_ref[...]
    r = _sigmoid(gates[:, :GROUP_W])
    ig = _sigmoid(gates[:, GROUP_W:])
    log_a = -RG_C * r * _softplus(-lam_ref[...])
    a = jnp.exp(log_a)
    u = jnp.sqrt(jnp.maximum(1.0 - jnp.exp(2.0 * log_a), 0.0)) * (ig * xc)

    pos = lax.broadcasted_iota(jnp.int32, (TB, GROUP_W), 0) % SUBLANES
    for d in (1, 2, 4):
        if rev:
            keep = pos < SUBLANES - d
            shift = TB - d
        else:
            keep = pos >= d
            shift = d
        a_s = jnp.where(keep, pltpu.roll(a, shift, 0), 1.0)
        u_s = jnp.where(keep, pltpu.roll(u, shift, 0), 0.0)
        u = a * u_s + u
        a = a * a_s
    a_ref[...] = a
    u_ref[...] = u
    ntile = TB // SUBLANES

    def tile_step(i, hprev):
        ti = (ntile - 1 - i) if rev else i
        r0 = pl.multiple_of(ti * SUBLANES, SUBLANES)
        h = a_ref[pl.ds(r0, SUBLANES), :] * hprev + u_ref[pl.ds(r0, SUBLANES), :]
        h_ref[pl.ds(r0, SUBLANES), :] = h
        edge = h[0:1] if rev else h[SUBLANES - 1:SUBLANES]
        return jnp.broadcast_to(edge, (SUBLANES, GROUP_W))

    carry_ref[...] = lax.fori_loop(0, ntile, tile_step, carry_ref[...])
    if rev:
        hs = h_ref[...] + part_ref[0]
        o_ref[0] = (hs * jax.nn.gelu(gate_ref[0].astype(F32), approximate=True)).astype(BF16)
    else:
        o_ref[0] = h_ref[...]


def _rglru(p3, conv_w, conv_b, wg, bg, lam, batch, seq):
    TB = RG_TB
    nblk = seq // TB
    hpb = TB // RG_HALO
    nhalo = seq // RG_HALO
    scratch = [pltpu.VMEM((SUBLANES, GROUP_W), F32)] + [pltpu.VMEM((TB, GROUP_W), F32)] * 3

    def specs(rev):
        def blk(j):
            return (nblk - 1 - j) if rev else j
        return [
            pl.BlockSpec((1, TB, GROUP_W), lambda b, j: (b, blk(j), P_CX)),
            pl.BlockSpec((1, RG_HALO, GROUP_W), lambda b, j: (b, jnp.maximum(blk(j) * hpb - 1, 0), P_CX)),
            pl.BlockSpec((1, RG_HALO, GROUP_W),
                         lambda b, j: (b, jnp.minimum((blk(j) + 1) * hpb, nhalo - 1), P_CX)),
            _const_spec((CONV_W, GROUP_W)),
            _const_spec((1, GROUP_W)),
            _const_spec((GROUP_W, 2 * GROUP_W)),
            _const_spec((1, 2 * GROUP_W)),
            _const_spec((1, GROUP_W)),
        ]

    part = pl.pallas_call(
        functools.partial(_rglru_kernel, rev=False, nblk=nblk),
        grid=(batch, nblk),
        in_specs=specs(False),
        out_specs=pl.BlockSpec((1, TB, GROUP_W), lambda b, j: (b, j, 0)),
        out_shape=jax.ShapeDtypeStruct((batch, seq, GROUP_W), F32),
        scratch_shapes=scratch,
        compiler_params=_cparams(2),
        name="rglru_fw",
    )(p3, p3, p3, conv_w, conv_b, wg[0], bg[0], lam[0:1])
    return pl.pallas_call(
        functools.partial(_rglru_kernel, rev=True, nblk=nblk),
        grid=(batch, nblk),
        in_specs=specs(True) + [
            pl.BlockSpec((1, TB, GROUP_W), lambda b, j: (b, nblk - 1 - j, 0)),
            pl.BlockSpec((1, TB, GROUP_W), lambda b, j: (b, nblk - 1 - j, P_CG)),
        ],
        out_specs=pl.BlockSpec((1, TB, GROUP_W), lambda b, j: (b, nblk - 1 - j, 0)),
        out_shape=jax.ShapeDtypeStruct((batch, seq, GROUP_W), BF16),
        scratch_shapes=scratch,
        compiler_params=_cparams(2),
        name="rglru_bw",
    )(p3, p3, p3, conv_w, conv_b, wg[1], bg[1], lam[1:2], part, p3)


def _post_kernel(oa_ref, ob_ref, oc_ref, od_ref, r_ref, p_ref, nw_ref, wo_ref, wig_ref, wiu_ref, wfo_ref,
                 wpg_ref, wpp_ref, out_ref):
    nw = nw_ref[...]
    mixed_in = jnp.concatenate([oa_ref[...], ob_ref[...], oc_ref[...], od_ref[...]], axis=1)
    r = r_ref[...] + _rms(_dot(mixed_in, wo_ref[...]), nw[1:2])
    h2 = _rms(r, nw[2:3]).astype(BF16)
    ff = jnp.zeros_like(r)
    for c in range(N_FF_CHUNKS):
        act = _silu(_dot(h2, wig_ref[c])) * _dot(h2, wiu_ref[c])
        ff = ff + _dot(act.astype(BF16), wfo_ref[c])
    r = r + _rms(ff, nw[3:4])
    gate = _sigmoid(_dot(r.astype(BF16), wpg_ref[...]))
    out_ref[...] = r + gate * _dot(p_ref[0].astype(BF16), wpp_ref[...])


def _post(oa, ob, oc, od, r, p, layer, nw, wo, wig, wiu, wfo, wpg, wpp):
    tokens = r.shape[0]
    tm = TM_POST
    mix_spec = pl.BlockSpec((tm, GROUP_W), lambda i: (i, 0))
    return pl.pallas_call(
        _post_kernel,
        grid=(tokens // tm,),
        in_specs=[
            mix_spec, mix_spec, mix_spec, mix_spec,
            pl.BlockSpec((tm, D_MODEL), lambda i: (i, 0)),
            pl.BlockSpec((1, tm, PLE_DIM), lambda i: (layer, i, 0)),
            _const_spec((4, D_MODEL)),
            _const_spec((D_MODEL, D_MODEL)),
            _const_spec((N_FF_CHUNKS, D_MODEL, FF_CHUNK)),
            _const_spec((N_FF_CHUNKS, D_MODEL, FF_CHUNK)),
            _const_spec((N_FF_CHUNKS, FF_CHUNK, D_MODEL)),
            _const_spec((D_MODEL, D_MODEL)),
            _const_spec((PLE_DIM, D_MODEL)),
        ],
        out_specs=pl.BlockSpec((tm, D_MODEL), lambda i: (i, 0)),
        out_shape=jax.ShapeDtypeStruct((tokens, D_MODEL), F32),
        compiler_params=_cparams(1),
        name="post",
    )(oa, ob, oc, od, r, p, nw, wo, wig, wiu, wfo, wpg, wpp)


def _block_diag(w):
    out = jnp.zeros((GROUP_W, GROUP_W), w.dtype)
    for h in range(N_HEADS):
        out = out.at[h * HEAD_DIM:(h + 1) * HEAD_DIM, h * HEAD_DIM:(h + 1) * HEAD_DIM].set(w[h])
    return out


def _prep_weights(norm_w, w_in, diff_lambda, diff_subln_w, hgrn_lb_raw, hgrn_norm_w, rg_conv_w, rg_conv_b,
                  rg_w_a, rg_b_a, rg_w_x, rg_b_x, rg_lambda, ret_decay, ret_norm_w, w_out, w_ffn_in,
                  w_ffn_out, w_ple_gate, w_ple_proj):
    layers = []
    for l in range(DEPTH):
        wi = w_in[l]
        w_nat = jnp.concatenate([wi[:, GROUP_W:2 * GROUP_W], wi[:, 3 * GROUP_W:]], axis=1).astype(BF16)
        w_qv_t = jnp.concatenate([wi[:, :GROUP_W], wi[:, 2 * GROUP_W:3 * GROUP_W]], axis=1).T.astype(BF16)
        wg = [jnp.concatenate([_block_diag(rg_w_a[l, d]), _block_diag(rg_w_x[l, d])], axis=1).astype(BF16)
              for d in range(2)]
        bg = [jnp.concatenate([rg_b_a[l, d], rg_b_x[l, d]])[None, :] for d in range(2)]
        wfi = w_ffn_in[l].astype(BF16)
        layers.append(dict(
            nw=norm_w[l], nw0=norm_w[l, 0:1], w_nat=w_nat, w_qv_t=w_qv_t,
            dl=diff_lambda[l], sw=diff_subln_w[l][:, None],
            lb_raw=hgrn_lb_raw, hgrn_nw=hgrn_norm_w[l][None, :],
            conv_w=rg_conv_w[l], conv_b=rg_conv_b[l][None, :], wg=wg, bg=bg, lam=rg_lambda[l],
            dec=jnp.repeat(ret_decay[l], HEAD_DIM, axis=-1), ret_nw=ret_norm_w[l][None, :],
            wo=w_out[l].astype(BF16),
            wig=wfi[:, :D_FF].reshape(D_MODEL, N_FF_CHUNKS, FF_CHUNK).transpose(1, 0, 2),
            wiu=wfi[:, D_FF:].reshape(D_MODEL, N_FF_CHUNKS, FF_CHUNK).transpose(1, 0, 2),
            wfo=w_ffn_out[l].astype(BF16).reshape(N_FF_CHUNKS, FF_CHUNK, D_MODEL),
            wpg=w_ple_gate[l].astype(BF16), wpp=w_ple_proj[l].astype(BF16),
        ))
    return layers


def _trunk(x, p, layers):
    batch, seq, _ = x.shape
    tokens = batch * seq
    r = x.reshape(tokens, D_MODEL)
    p = p.reshape(DEPTH, tokens, PLE_DIM)
    for l, w in enumerate(layers):
        lambda_init = 0.8 - 0.6 * math.exp(-0.3 * l)
        pn, qt, vt = _inproj(r, w["nw0"], w["w_nat"], w["w_qv_t"], batch, seq)
        p3 = pn.reshape(batch, seq, N_PGROUPS * GROUP_W)
        oa = _attention(qt, p3, vt, w["dl"], w["sw"], batch, seq, lambda_init)
        ob = _hgrn(p3, w["lb_raw"], w["hgrn_nw"], batch, seq, l)
        oc = _rglru(p3, w["conv_w"], w["conv_b"], w["wg"], w["bg"], w["lam"], batch, seq)
        od = _retention(p3, w["dec"], w["ret_nw"], batch, seq)
        flat = lambda o: o.reshape(tokens, GROUP_W)
        r = _post(flat(oa), flat(ob), flat(oc), flat(od), r, p, l, w["nw"], w["wo"], w["wig"], w["wiu"],
                  w["wfo"], w["wpg"], w["wpp"])
    return r.reshape(batch, seq, D_MODEL)


def kernel(x_prompt, x_sample, p_prompt, p_sample, norm_w, w_in, diff_lambda, diff_subln_w, hgrn_lb_raw,
           hgrn_norm_w, rg_conv_w, rg_conv_b, rg_w_a, rg_b_a, rg_w_x, rg_b_x, rg_lambda, ret_decay,
           ret_norm_w, w_out, w_ffn_in, w_ffn_out, w_ple_gate, w_ple_proj):
    layers = _prep_weights(norm_w, w_in, diff_lambda, diff_subln_w, hgrn_lb_raw, hgrn_norm_w, rg_conv_w,
                           rg_conv_b, rg_w_a, rg_b_a, rg_w_x, rg_b_x, rg_lambda, ret_decay, ret_norm_w,
                           w_out, w_ffn_in, w_ffn_out, w_ple_gate, w_ple_proj)
    return (_trunk(x_prompt, p_prompt, layers), _trunk(x_sample, p_sample, layers))
```

```python
import functools
import math

import jax
import jax.numpy as jnp
from jax import lax
from jax.experimental import pallas as pl
from jax.experimental.pallas import tpu as pltpu

F32 = jnp.float32
BF16 = jnp.bfloat16

D_MODEL = 1024
DEPTH = 4
GROUP_W = 256
N_HEADS = 4
HEAD_DIM = 64
DIFF_HALF = 32
D_FF = 2816
FF_CHUNK = 256
N_FF_CHUNKS = D_FF // FF_CHUNK
PLE_DIM = 256
CONV_W = 4
CONV_PAD_L = 2
RG_C = 8.0
EPS = 1e-6
LOG2E = 1.4426950408889634

(P_AK, P_BQ, P_BF_FW, P_BF_BW, P_BI, P_BG, P_CX, P_CG, P_DQ, P_DK, P_DV, P_DG) = range(12)
N_PGROUPS = 12

VMEM_LIMIT_V7X = 56 * 1024 * 1024

TM_INPROJ = 512
TM_POST = 512
TQ_ATTN = 256
TK_ATTN = 256
ATTN_SKEW = 3
ATTN_SLOTS = ATTN_SKEW + 1
ATTN_UNROLL = 4
HGRN_C = 64
HGRN_NB = 4
HGRN_MID = HGRN_C // 2
HGRN_SAFE_EXP = 80.0
RET_C = 256
RG_TB = 256
RG_HALO = 16
SUBLANES = 8


def _cparams(n_axes):
    return pltpu.CompilerParams(dimension_semantics=("arbitrary",) * n_axes,
                                vmem_limit_bytes=VMEM_LIMIT_V7X)


def _const_spec(shape):
    nd = len(shape)
    return pl.BlockSpec(shape, lambda *_: (0,) * nd, pipeline_mode=pl.Buffered(1))


def _rms(x, w):
    return x * lax.rsqrt(jnp.mean(x * x, axis=-1, keepdims=True) + EPS) * w


def _sigmoid(x):
    return 1.0 / (1.0 + jnp.exp(-x))


def _silu(x):
    return x * _sigmoid(x)


def _softplus(x):
    return jnp.maximum(x, 0.0) + jnp.log(1.0 + jnp.exp(-jnp.abs(x)))


def _dot(a, b):
    return jnp.dot(a, b, preferred_element_type=F32)


def _dot_nt(a, b):
    return lax.dot_general(a, b, (((1,), (1,)), ((), ())), preferred_element_type=F32)


def _dot_tn(a, b):
    return lax.dot_general(a, b, (((0,), (0,)), ((), ())), preferred_element_type=F32)


def _inproj_kernel(x_ref, nw_ref, wn_ref, wt_ref, p_ref, qt_ref, vt_ref):
    x = x_ref[...]
    y = _rms(x, nw_ref[...]).astype(BF16)
    for g in range(N_PGROUPS):
        cols = slice(g * GROUP_W, (g + 1) * GROUP_W)
        p_ref[:, cols] = _dot(y, wn_ref[:, cols]).astype(BF16)
    t = _dot_nt(wt_ref[...], y)
    qt_ref[0] = (t[:GROUP_W] * (DIFF_HALF ** -0.5 * LOG2E)).astype(BF16)
    vt_ref[0] = t[GROUP_W:].astype(BF16)


def _inproj(r, nw, w_nat, w_qv_t, batch, seq):
    tokens = batch * seq
    tm = TM_INPROJ
    nsb = seq // tm
    return pl.pallas_call(
        _inproj_kernel,
        grid=(tokens // tm,),
        in_specs=[
            pl.BlockSpec((tm, D_MODEL), lambda i: (i, 0)),
            _const_spec((1, D_MODEL)),
            _const_spec((D_MODEL, N_PGROUPS * GROUP_W)),
            _const_spec((2 * GROUP_W, D_MODEL)),
        ],
        out_specs=[
            pl.BlockSpec((tm, N_PGROUPS * GROUP_W), lambda i: (i, 0)),
            pl.BlockSpec((1, GROUP_W, tm), lambda i: (i // nsb, 0, i % nsb)),
            pl.BlockSpec((1, GROUP_W, tm), lambda i: (i // nsb, 0, i % nsb)),
        ],
        out_shape=[
            jax.ShapeDtypeStruct((tokens, N_PGROUPS * GROUP_W), BF16),
            jax.ShapeDtypeStruct((batch, GROUP_W, seq), BF16),
            jax.ShapeDtypeStruct((batch, GROUP_W, seq), BF16),
        ],
        compiler_params=_cparams(1),
        name="inproj",
    )(r, nw, w_nat, w_qv_t)


def _attn_kernel(qt_ref, k_ref, vt_ref, dl_ref, sw_ref, o_ref, w_ref, m_ref, acc_ref, s_ref, t_ref,
                 *, seq, lambda_init):
    tq, tk = TQ_ATTN, TK_ATTN
    assert tq == tk
    q0 = pl.program_id(1) * tq
    nk = seq // tk
    n_chain = 2 * N_HEADS
    sub = lax.broadcasted_iota(jnp.int32, (2 * HEAD_DIM, tq), 0)
    for h in range(N_HEADS):
        half = h // 2
        qt_half = qt_ref[0, half * 2 * HEAD_DIM:(half + 1) * 2 * HEAD_DIM, :]
        for c in range(2):
            lo = (h % 2) * HEAD_DIM + c * DIFF_HALF
            w_ref[2 * h + c] = jnp.where((sub >= lo) & (sub < lo + DIFF_HALF), qt_half,
                                         jnp.zeros_like(qt_half))
    ones = jnp.ones((2 * SUBLANES, tk), BF16)
    slopes2 = [2.0 ** (-8.0 * (h + 1) / N_HEADS) * LOG2E for h in range(N_HEADS)]

    @pl.when((pl.program_id(0) == 0) & (pl.program_id(1) == 0))
    def _():
        row = lax.broadcasted_iota(jnp.int32, (tk, tq), 0)
        col = lax.broadcasted_iota(jnp.int32, (tk, tq), 1)
        d = (row - col).astype(F32)
        for h in range(N_HEADS):
            t_ref[h, 0] = slopes2[h] * d
            t_ref[h, 1] = -slopes2[h] * jnp.abs(d)
            t_ref[h, 2] = -slopes2[h] * d

    def scores(jj, i):
        k0 = pl.multiple_of(jj * tk, tk)
        half = i // 4
        kt = k_ref[0, pl.ds(k0, tk), half * 2 * HEAD_DIM:(half + 1) * 2 * HEAD_DIM]
        s_ref[i % ATTN_SLOTS] = _dot(kt, w_ref[i])

    def consume(j, i, sel, off, online):
        h = i // 2
        k0 = pl.multiple_of(j * tk, tk)
        u = s_ref[i % ATTN_SLOTS] + t_ref[h, sel]
        vt = jnp.concatenate([vt_ref[0, h * HEAD_DIM:(h + 1) * HEAD_DIM, pl.ds(k0, tk)], ones], axis=0)
        m = m_ref[i]
        if online:
            m_new = jnp.maximum(m, jnp.max(u, axis=0, keepdims=True) - off[h])
            p = jnp.exp2(u - (m_new + off[h])).astype(BF16)
            acc_ref[i] = acc_ref[i] * jnp.exp2(m - m_new) + _dot(vt, p)
            m_ref[i] = m_new
        else:
            p = jnp.exp2(u - (m + off[h])).astype(BF16)
            acc_ref[i] = acc_ref[i] + _dot(vt, p)

    def key_sweep(unroll, online):
        assert nk % unroll == 0
        for i in range(ATTN_SKEW):
            scores(0, i)

        def body(jo, carry):
            qi = pl.program_id(1)
            for ju in range(unroll):
                j = jo * unroll + ju
                jn = jnp.minimum(j + 1, nk - 1)
                sel = (j >= qi).astype(jnp.int32) + (j > qi).astype(jnp.int32)
                gap = jnp.abs(q0 - j * tk).astype(F32)
                off = [slopes2[h] * gap for h in range(N_HEADS)]
                for i in range(n_chain):
                    a = i + ATTN_SKEW
                    if a < n_chain:
                        scores(j, a)
                    else:
                        scores(jn, a - n_chain)
                    consume(j, i, sel, off, online)
            return carry

        lax.fori_loop(0, nk // unroll, body, 0)

    for i in range(n_chain):
        h = i // 2
        half = i // 4
        kt = k_ref[0, pl.ds(pl.multiple_of(q0, tk), tk), half * 2 * HEAD_DIM:(half + 1) * 2 * HEAD_DIM]
        m_ref[i] = jnp.max(_dot(kt, w_ref[i]) + t_ref[h, 1], axis=0, keepdims=True)
    acc_ref[...] = jnp.zeros_like(acc_ref)
    key_sweep(ATTN_UNROLL, online=False)
    poisoned = jnp.sum(acc_ref[...] * 0.0) != 0.0

    @pl.when(poisoned)
    def _():
        m_ref[...] = jnp.full(m_ref.shape, -1e30, F32)
        acc_ref[...] = jnp.zeros_like(acc_ref)
        key_sweep(1, online=True)

    dl = dl_ref[...]
    lam = (jnp.exp(jnp.sum(dl[0:1] * dl[1:2], axis=1, keepdims=True))
           - jnp.exp(jnp.sum(dl[2:3] * dl[3:4], axis=1, keepdims=True)) + lambda_init)
    outs = []
    for h in range(N_HEADS):
        a1 = acc_ref[2 * h]
        a2 = acc_ref[2 * h + 1]
        o = (a1[:HEAD_DIM] / a1[HEAD_DIM:HEAD_DIM + 1]
             - lam * (a2[:HEAD_DIM] / a2[HEAD_DIM:HEAD_DIM + 1]))
        ms = jnp.mean(o * o, axis=0, keepdims=True)
        outs.append(o * lax.rsqrt(ms + EPS) * sw_ref[...] * (1.0 - lambda_init))
    o_ref[0] = jnp.concatenate(outs, axis=0).T.astype(BF16)


def _attention(qt, p3, vt, dl, sw, batch, seq, lambda_init):
    tq = TQ_ATTN
    n_chain = 2 * N_HEADS
    scratch = [
        pltpu.VMEM((n_chain, 2 * HEAD_DIM, tq), BF16),
        pltpu.VMEM((n_chain, 1, tq), F32),
        pltpu.VMEM((n_chain, HEAD_DIM + 2 * SUBLANES, tq), F32),
        pltpu.VMEM((ATTN_SLOTS, TK_ATTN, tq), F32),
        pltpu.VMEM((N_HEADS, 3, TK_ATTN, tq), F32),
    ]
    return pl.pallas_call(
        functools.partial(_attn_kernel, seq=seq, lambda_init=lambda_init),
        grid=(batch, seq // tq),
        in_specs=[
            pl.BlockSpec((1, GROUP_W, tq), lambda b, i: (b, 0, i)),
            pl.BlockSpec((1, seq, GROUP_W), lambda b, i: (b, 0, P_AK), pipeline_mode=pl.Buffered(1)),
            pl.BlockSpec((1, GROUP_W, seq), lambda b, i: (b, 0, 0), pipeline_mode=pl.Buffered(1)),
            _const_spec((4, DIFF_HALF)),
            _const_spec((HEAD_DIM, 1)),
        ],
        out_specs=pl.BlockSpec((1, tq, GROUP_W), lambda b, i: (b, i, 0)),
        out_shape=jax.ShapeDtypeStruct((batch, seq, GROUP_W), BF16),
        scratch_shapes=scratch,
        compiler_params=_cparams(2),
        name="diff_attn",
    )(qt, p3, vt, dl, sw)


def _hgrn_kernel(*refs, layer, rev):
    if rev:
        (q_ref, f_ref, v_ref, lbraw_ref, part_ref, g_ref, nw_ref, o_ref,
         st_ref, qs_ref, ks_ref, fs_ref, vs_ref, os_ref) = refs
    else:
        (q_ref, f_ref, v_ref, lbraw_ref, o_ref,
         st_ref, qs_ref, ks_ref, fs_ref, vs_ref, os_ref) = refs
    C = HGRN_C
    T = HGRN_NB * C

    @pl.when(pl.program_id(1) == 0)
    def _():
        st_ref[...] = jnp.zeros_like(st_ref)

    raw = lbraw_ref[...]
    e = jnp.exp(raw - jnp.max(raw, axis=0, keepdims=True))
    soft = e / jnp.sum(e, axis=0, keepdims=True)
    lb = jnp.zeros((1, GROUP_W), F32)
    for i in range(1, layer + 1):
        lb = lb + soft[i:i + 1]

    xf = f_ref[0].astype(F32)
    fg = lb + (1.0 - lb) * _sigmoid(xf)
    kk = (1.0 - lb) * _sigmoid(-xf)
    q = _silu(q_ref[0].astype(F32))
    v = v_ref[0]
    lf = jnp.log(fg)

    row = lax.broadcasted_iota(jnp.int32, (T, T), 0)
    col = lax.broadcasted_iota(jnp.int32, (T, T), 1)
    same = (row // C) == (col // C)
    tri = same & ((row <= col) if rev else (row >= col))
    tri_b = jnp.where(tri, 1.0, 0.0).astype(BF16)
    hi = lf.astype(BF16)
    lo = (lf - hi.astype(F32)).astype(BF16)
    cum = _dot(tri_b, hi) + _dot(tri_b, lo)

    def chunk_rows(idx):
        return jnp.concatenate(
            [jnp.broadcast_to(cum[c * C + idx:c * C + idx + 1], (C, GROUP_W)) for c in range(HGRN_NB)], axis=0)

    cm = chunk_rows(HGRN_MID)
    last = chunk_rows(0 if rev else C - 1)
    worst = jnp.max(jnp.abs(cum - cm))

    @pl.when(worst <= HGRN_SAFE_EXP)
    def _fast():
        qt = (q * jnp.exp(cum - cm)).astype(BF16)
        kt = (kk * jnp.exp(cm - cum)).astype(BF16)
        qi = (q * jnp.exp(cum)).astype(BF16)
        ki = (kk * jnp.exp(last - cum)).astype(BF16)
        dl = jnp.exp(last)
        for h in range(N_HEADS):
            hs = slice(h * HEAD_DIM, (h + 1) * HEAD_DIM)
            a = jnp.where(tri, _dot_nt(qt[:, hs], kt[:, hs]), 0.0).astype(BF16)
            os_ref[:, hs] = _dot(a, v[:, hs])
        st = st_ref[...]
        for c in (reversed(range(HGRN_NB)) if rev else range(HGRN_NB)):
            rs = slice(c * C, (c + 1) * C)
            os_ref[rs, :] = os_ref[rs, :] + _dot_nt(qi[rs], st.astype(BF16))
            st = st * dl[c * C:c * C + 1] + jnp.where(same, _dot_tn(v[rs], ki[rs]), 0.0)
        st_ref[...] = st

    @pl.when(jnp.logical_not(worst <= HGRN_SAFE_EXP))
    def _slow():
        qs_ref[...] = q
        ks_ref[...] = kk
        fs_ref[...] = fg
        vs_ref[...] = v.astype(F32)

        def step(i, carry):
            t = (T - 1 - i) if rev else i
            k8 = jnp.broadcast_to(ks_ref[pl.ds(t, 1), :], (SUBLANES, GROUP_W))
            v8 = jnp.broadcast_to(vs_ref[pl.ds(t, 1), :], (SUBLANES, GROUP_W)) * (1.0 / SUBLANES)
            st = st_ref[...] * fs_ref[pl.ds(t, 1), :] + jnp.where(same, _dot_tn(v8, k8), 0.0)
            st_ref[...] = st
            q8 = jnp.broadcast_to(qs_ref[pl.ds(t, 1), :], (SUBLANES, GROUP_W))
            os_ref[pl.ds(t, 1), :] = _dot_nt(q8, st)[0:1]
            return carry

        lax.fori_loop(0, T, step, 0)

    if rev:
        o = os_ref[...] + part_ref[0]
        g = g_ref[0].astype(F32)
        nw = nw_ref[...]
        outs = []
        for h in range(N_HEADS):
            hs = slice(h * HEAD_DIM, (h + 1) * HEAD_DIM)
            outs.append(_rms(o[:, hs], nw))
        o_ref[0] = (jnp.concatenate(outs, axis=1) * _silu(g)).astype(BF16)
    else:
        o_ref[0] = os_ref[...]


def _hgrn(p3, lb_raw, norm_w, batch, seq, layer):
    C = HGRN_NB * HGRN_C
    assert C == GROUP_W
    nc = seq // C
    scratch = [pltpu.VMEM((GROUP_W, GROUP_W), F32)] + [pltpu.VMEM((C, GROUP_W), F32)] * 5

    def grp(g, rev):
        if rev:
            return pl.BlockSpec((1, C, GROUP_W), lambda b, j: (b, nc - 1 - j, g))
        return pl.BlockSpec((1, C, GROUP_W), lambda b, j: (b, j, g))

    part = pl.pallas_call(
        functools.partial(_hgrn_kernel, layer=layer, rev=False),
        grid=(batch, nc),
        in_specs=[grp(P_BQ, False), grp(P_BF_FW, False), grp(P_BI, False), _const_spec((DEPTH, GROUP_W))],
        out_specs=pl.BlockSpec((1, C, GROUP_W), lambda b, j: (b, j, 0)),
        out_shape=jax.ShapeDtypeStruct((batch, seq, GROUP_W), F32),
        scratch_shapes=scratch,
        compiler_params=_cparams(2),
        name="hgrn_fw",
    )(p3, p3, p3, lb_raw[:, 0])
    return pl.pallas_call(
        functools.partial(_hgrn_kernel, layer=layer, rev=True),
        grid=(batch, nc),
        in_specs=[grp(P_BQ, True), grp(P_BF_BW, True), grp(P_BI, True), _const_spec((DEPTH, GROUP_W)),
                  pl.BlockSpec((1, C, GROUP_W), lambda b, j: (b, nc - 1 - j, 0)),
                  grp(P_BG, True), _const_spec((1, HEAD_DIM))],
        out_specs=pl.BlockSpec((1, C, GROUP_W), lambda b, j: (b, nc - 1 - j, 0)),
        out_shape=jax.ShapeDtypeStruct((batch, seq, GROUP_W), BF16),
        scratch_shapes=scratch,
        compiler_params=_cparams(2),
        name="hgrn_bw",
    )(p3, p3, p3, lb_raw[:, 1], part, p3, norm_w)


def _log_sigmoid(x):
    return jnp.minimum(x, 0.0) - jnp.log(1.0 + jnp.exp(-jnp.abs(x)))


def _ret_kernel(*refs, rev):
    if rev:
        q_ref, k_ref, v_ref, dec_ref, part_ref, g_ref, nw_ref, o_ref, st_ref = refs
    else:
        q_ref, k_ref, v_ref, dec_ref, o_ref, st_ref, dm_ref = refs
    C = RET_C

    @pl.when(pl.program_id(1) == 0)
    def _():
        st_ref[...] = jnp.zeros_like(st_ref)

    lg_f = _log_sigmoid(dec_ref[0:1, :])
    lg_b = _log_sigmoid(dec_ref[1:2, :])
    lg = lg_b if rev else lg_f
    if not rev:
        @pl.when((pl.program_id(0) == 0) & (pl.program_id(1) == 0))
        def _():
            row = lax.broadcasted_iota(jnp.int32, (C, C), 0)
            col = lax.broadcasted_iota(jnp.int32, (C, C), 1)
            d_ts = (row - col).astype(F32)
            for h in range(N_HEADS):
                lf1 = lg_f[:, h * HEAD_DIM:h * HEAD_DIM + 1]
                lb1 = lg_b[:, h * HEAD_DIM:h * HEAD_DIM + 1]
                dm_ref[h] = (jnp.where(d_ts >= 0, jnp.exp(jnp.maximum(d_ts, 0.0) * lf1), 0.0)
                             + jnp.where(d_ts <= 0, jnp.exp(jnp.maximum(-d_ts, 0.0) * lb1), 0.0))
    t = lax.broadcasted_iota(jnp.int32, (C, 1), 0).astype(F32)
    q = q_ref[0].astype(F32)
    k = k_ref[0].astype(F32) * HEAD_DIM ** -0.5
    v = v_ref[0]
    if rev:
        xi = jnp.exp((C - t) * lg)
        zeta = jnp.exp(t * lg)
    else:
        xi = jnp.exp((t + 1.0) * lg)
        zeta = jnp.exp((C - 1.0 - t) * lg)
    cd = jnp.exp(C * lg)
    qx = (q * xi).astype(BF16)
    kz = (k * zeta).astype(BF16)
    if not rev:
        qb = q.astype(BF16)
        kb = k.astype(BF16)
    outs = []
    for h in range(N_HEADS):
        hs = slice(h * HEAD_DIM, (h + 1) * HEAD_DIM)
        st = st_ref[h]
        o = _dot(qx[:, hs], st.astype(BF16))
        if not rev:
            sc = (_dot_nt(qb[:, hs], kb[:, hs]) * dm_ref[h]).astype(BF16)
            o = o + _dot(sc, v[:, hs])
        st_ref[h] = st * cd[:, h * HEAD_DIM:h * HEAD_DIM + 1] + _dot_tn(kz[:, hs], v[:, hs])
        outs.append(o)
    if rev:
        part = part_ref[0]
        nw = nw_ref[...]
        g = g_ref[0].astype(F32)
        normed = [_rms(outs[h] + part[:, h * HEAD_DIM:(h + 1) * HEAD_DIM], nw) for h in range(N_HEADS)]
        o_ref[0] = (jnp.concatenate(normed, axis=1) * _silu(g)).astype(BF16)
    else:
        o_ref[0] = jnp.concatenate(outs, axis=1)


def _retention(p3, dec_lanes, norm_w, batch, seq):
    C = RET_C
    nc = seq // C
    scratch = [pltpu.VMEM((N_HEADS, HEAD_DIM, HEAD_DIM), F32)]

    def grp(g, rev):
        if rev:
            return pl.BlockSpec((1, C, GROUP_W), lambda b, j: (b, nc - 1 - j, g))
        return pl.BlockSpec((1, C, GROUP_W), lambda b, j: (b, j, g))

    part = pl.pallas_call(
        functools.partial(_ret_kernel, rev=False),
        grid=(batch, nc),
        in_specs=[grp(P_DQ, False), grp(P_DK, False), grp(P_DV, False), _const_spec((2, GROUP_W))],
        out_specs=pl.BlockSpec((1, C, GROUP_W), lambda b, j: (b, j, 0)),
        out_shape=jax.ShapeDtypeStruct((batch, seq, GROUP_W), F32),
        scratch_shapes=scratch + [pltpu.VMEM((N_HEADS, C, C), F32)],
        compiler_params=_cparams(2),
        name="ret_fw",
    )(p3, p3, p3, dec_lanes)
    return pl.pallas_call(
        functools.partial(_ret_kernel, rev=True),
        grid=(batch, nc),
        in_specs=[grp(P_DQ, True), grp(P_DK, True), grp(P_DV, True), _const_spec((2, GROUP_W)),
                  pl.BlockSpec((1, C, GROUP_W), lambda b, j: (b, nc - 1 - j, 0)),
                  grp(P_DG, True), _const_spec((1, HEAD_DIM))],
        out_specs=pl.BlockSpec((1, C, GROUP_W), lambda b, j: (b, nc - 1 - j, 0)),
        out_shape=jax.ShapeDtypeStruct((batch, seq, GROUP_W), BF16),
        scratch_shapes=scratch,
        compiler_params=_cparams(2),
        name="ret_bw",
    )(p3, p3, p3, dec_lanes, part, p3, norm_w)


def _rglru_kernel(*refs, rev, nblk):
    if rev:
        (x_ref, xp_ref, xn_ref, cw_ref, cb_ref, wg_ref, bg_ref, lam_ref, part_ref, gate_ref,
         o_ref, carry_ref, a_ref, u_ref, h_ref) = refs
    else:
        (x_ref, xp_ref, xn_ref, cw_ref, cb_ref, wg_ref, bg_ref, lam_ref,
         o_ref, carry_ref, a_ref, u_ref, h_ref) = refs
    TB = RG_TB
    j = pl.program_id(1)
    blk = (nblk - 1 - j) if rev else j

    @pl.when(j == 0)
    def _():
        carry_ref[...] = jnp.zeros_like(carry_ref)

    prev = jnp.where(blk > 0, xp_ref[0].astype(F32), 0.0)
    nxt = jnp.where(blk < nblk - 1, xn_ref[0].astype(F32), 0.0)
    xe = jnp.concatenate([prev, x_ref[0].astype(F32), nxt], axis=0)
    cw = cw_ref[...]
    xc = cb_ref[...] + sum(
        xe[RG_HALO - CONV_PAD_L + w:RG_HALO - CONV_PAD_L + w + TB] * cw[w:w + 1] for w in range(CONV_W))
    gates = _dot(xc.astype(BF16), wg_ref[...]) + bg_ref[...]
    r = _sigmoid(gates[:, :GROUP_W])
    ig = _sigmoid(gates[:, GROUP_W:])
    log_a = -RG_C * r * _softplus(-lam_ref[...])
    a = jnp.exp(log_a)
    u = jnp.sqrt(jnp.maximum(1.0 - jnp.exp(2.0 * log_a), 0.0)) * (ig * xc)

    pos = lax.broadcasted_iota(jnp.int32, (TB, GROUP_W), 0) % SUBLANES
    for d in (1, 2, 4):
        if rev:
            keep = pos < SUBLANES - d
            shift = TB - d
        else:
            keep = pos >= d
            shift = d
        a_s = jnp.where(keep, pltpu.roll(a, shift, 0), 1.0)
        u_s = jnp.where(keep, pltpu.roll(u, shift, 0), 0.0)
        u = a * u_s + u
        a = a * a_s
    a_ref[...] = a
    u_ref[...] = u
    ntile = TB // SUBLANES

    def tile_step(i, hprev):
        ti = (ntile - 1 - i) if rev else i
        r0 = pl.multiple_of(ti * SUBLANES, SUBLANES)
        h = a_ref[pl.ds(r0, SUBLANES), :] * hprev + u_ref[pl.ds(r0, SUBLANES), :]
        h_ref[pl.ds(r0, SUBLANES), :] = h
        edge = h[0:1] if rev else h[SUBLANES - 1:SUBLANES]
        return jnp.broadcast_to(edge, (SUBLANES, GROUP_W))

    carry_ref[...] = lax.fori_loop(0, ntile, tile_step, carry_ref[...])
    if rev:
        hs = h_ref[...] + part_ref[0]
        o_ref[0] = (hs * jax.nn.gelu(gate_ref[0].astype(F32), approximate=True)).astype(BF16)
    else:
        o_ref[0] = h_ref[...]


def _rglru(p3, conv_w, conv_b, wg, bg, lam, batch, seq):
    TB = RG_TB
    nblk = seq // TB
    hpb = TB // RG_HALO
    nhalo = seq // RG_HALO
    scratch = [pltpu.VMEM((SUBLANES, GROUP_W), F32)] + [pltpu.VMEM((TB, GROUP_W), F32)] * 3

    def specs(rev):
        def blk(j):
            return (nblk - 1 - j) if rev else j
        return [
            pl.BlockSpec((1, TB, GROUP_W), lambda b, j: (b, blk(j), P_CX)),
            pl.BlockSpec((1, RG_HALO, GROUP_W), lambda b, j: (b, jnp.maximum(blk(j) * hpb - 1, 0), P_CX)),
            pl.BlockSpec((1, RG_HALO, GROUP_W),
                         lambda b, j: (b, jnp.minimum((blk(j) + 1) * hpb, nhalo - 1), P_CX)),
            _const_spec((CONV_W, GROUP_W)),
            _const_spec((1, GROUP_W)),
            _const_spec((GROUP_W, 2 * GROUP_W)),
            _const_spec((1, 2 * GROUP_W)),
            _const_spec((1, GROUP_W)),
        ]

    part = pl.pallas_call(
        functools.partial(_rglru_kernel, rev=False, nblk=nblk),
        grid=(batch, nblk),
        in_specs=specs(False),
        out_specs=pl.BlockSpec((1, TB, GROUP_W), lambda b, j: (b, j, 0)),
        out_shape=jax.ShapeDtypeStruct((batch, seq, GROUP_W), F32),
        scratch_shapes=scratch,
        compiler_params=_cparams(2),
        name="rglru_fw",
    )(p3, p3, p3, conv_w, conv_b, wg[0], bg[0], lam[0:1])
    return pl.pallas_call(
        functools.partial(_rglru_kernel, rev=True, nblk=nblk),
        grid=(batch, nblk),
        in_specs=specs(True) + [
            pl.BlockSpec((1, TB, GROUP_W), lambda b, j: (b, nblk - 1 - j, 0)),
            pl.BlockSpec((1, TB, GROUP_W), lambda b, j: (b, nblk - 1 - j, P_CG)),
        ],
        out_specs=pl.BlockSpec((1, TB, GROUP_W), lambda b, j: (b, nblk - 1 - j, 0)),
        out_shape=jax.ShapeDtypeStruct((batch, seq, GROUP_W), BF16),
        scratch_shapes=scratch,
        compiler_params=_cparams(2),
        name="rglru_bw",
    )(p3, p3, p3, conv_w, conv_b, wg[1], bg[1], lam[1:2], part, p3)


def _post_kernel(oa_ref, ob_ref, oc_ref, od_ref, r_ref, p_ref, nw_ref, wo_ref, wig_ref, wiu_ref, wfo_ref,
                 wpg_ref, wpp_ref, out_ref):
    nw = nw_ref[...]
    mixed_in = jnp.concatenate([oa_ref[...], ob_ref[...], oc_ref[...], od_ref[...]], axis=1)
    r = r_ref[...] + _rms(_dot(mixed_in, wo_ref[...]), nw[1:2])
    h2 = _rms(r, nw[2:3]).astype(BF16)
    ff = jnp.zeros_like(r)
    for c in range(N_FF_CHUNKS):
        act = _silu(_dot(h2, wig_ref[c])) * _dot(h2, wiu_ref[c])
        ff = ff + _dot(act.astype(BF16), wfo_ref[c])
    r = r + _rms(ff, nw[3:4])
    gate = _sigmoid(_dot(r.astype(BF16), wpg_ref[...]))
    out_ref[...] = r + gate * _dot(p_ref[0].astype(BF16), wpp_ref[...])


def _post(oa, ob, oc, od, r, p, layer, nw, wo, wig, wiu, wfo, wpg, wpp):
    tokens = r.shape[0]
    tm = TM_POST
    mix_spec = pl.BlockSpec((tm, GROUP_W), lambda i: (i, 0))
    return pl.pallas_call(
        _post_kernel,
        grid=(tokens // tm,),
        in_specs=[
            mix_spec, mix_spec, mix_spec, mix_spec,
            pl.BlockSpec((tm, D_MODEL), lambda i: (i, 0)),
            pl.BlockSpec((1, tm, PLE_DIM), lambda i: (layer, i, 0)),
            _const_spec((4, D_MODEL)),
            _const_spec((D_MODEL, D_MODEL)),
            _const_spec((N_FF_CHUNKS, D_MODEL, FF_CHUNK)),
            _const_spec((N_FF_CHUNKS, D_MODEL, FF_CHUNK)),
            _const_spec((N_FF_CHUNKS, FF_CHUNK, D_MODEL)),
            _const_spec((D_MODEL, D_MODEL)),
            _const_spec((PLE_DIM, D_MODEL)),
        ],
        out_specs=pl.BlockSpec((tm, D_MODEL), lambda i: (i, 0)),
        out_shape=jax.ShapeDtypeStruct((tokens, D_MODEL), F32),
        compiler_params=_cparams(1),
        name="post",
    )(oa, ob, oc, od, r, p, nw, wo, wig, wiu, wfo, wpg, wpp)


def _block_diag(w):
    out = jnp.zeros((GROUP_W, GROUP_W), w.dtype)
    for h in range(N_HEADS):
        out = out.at[h * HEAD_DIM:(h + 1) * HEAD_DIM, h * HEAD_DIM:(h + 1) * HEAD_DIM].set(w[h])
    return out


def _prep_weights(norm_w, w_in, diff_lambda, diff_subln_w, hgrn_lb_raw, hgrn_norm_w, rg_conv_w, rg_conv_b,
                  rg_w_a, rg_b_a, rg_w_x, rg_b_x, rg_lambda, ret_decay, ret_norm_w, w_out, w_ffn_in,
                  w_ffn_out, w_ple_gate, w_ple_proj):
    layers = []
    for l in range(DEPTH):
        wi = w_in[l]
        w_nat = jnp.concatenate([wi[:, GROUP_W:2 * GROUP_W], wi[:, 3 * GROUP_W:]], axis=1).astype(BF16)
        w_qv_t = jnp.concatenate([wi[:, :GROUP_W], wi[:, 2 * GROUP_W:3 * GROUP_W]], axis=1).T.astype(BF16)
        wg = [jnp.concatenate([_block_diag(rg_w_a[l, d]), _block_diag(rg_w_x[l, d])], axis=1).astype(BF16)
              for d in range(2)]
        bg = [jnp.concatenate([rg_b_a[l, d], rg_b_x[l, d]])[None, :] for d in range(2)]
        wfi = w_ffn_in[l].astype(BF16)
        layers.append(dict(
            nw=norm_w[l], nw0=norm_w[l, 0:1], w_nat=w_nat, w_qv_t=w_qv_t,
            dl=diff_lambda[l], sw=diff_subln_w[l][:, None],
            lb_raw=hgrn_lb_raw, hgrn_nw=hgrn_norm_w[l][None, :],
            conv_w=rg_conv_w[l], conv_b=rg_conv_b[l][None, :], wg=wg, bg=bg, lam=rg_lambda[l],
            dec=jnp.repeat(ret_decay[l], HEAD_DIM, axis=-1), ret_nw=ret_norm_w[l][None, :],
            wo=w_out[l].astype(BF16),
            wig=wfi[:, :D_FF].reshape(D_MODEL, N_FF_CHUNKS, FF_CHUNK).transpose(1, 0, 2),
            wiu=wfi[:, D_FF:].reshape(D_MODEL, N_FF_CHUNKS, FF_CHUNK).transpose(1, 0, 2),
            wfo=w_ffn_out[l].astype(BF16).reshape(N_FF_CHUNKS, FF_CHUNK, D_MODEL),
            wpg=w_ple_gate[l].astype(BF16), wpp=w_ple_proj[l].astype(BF16),
        ))
    return layers


def _trunk(x, p, layers):
    batch, seq, _ = x.shape
    tokens = batch * seq
    r = x.reshape(tokens, D_MODEL)
    p = p.reshape(DEPTH, tokens, PLE_DIM)
    for l, w in enumerate(layers):
        lambda_init = 0.8 - 0.6 * math.exp(-0.3 * l)
        pn, qt, vt = _inproj(r, w["nw0"], w["w_nat"], w["w_qv_t"], batch, seq)
        p3 = pn.reshape(batch, seq, N_PGROUPS * GROUP_W)
        oa = _attention(qt, p3, vt, w["dl"], w["sw"], batch, seq, lambda_init)
        ob = _hgrn(p3, w["lb_raw"], w["hgrn_nw"], batch, seq, l)
        oc = _rglru(p3, w["conv_w"], w["conv_b"], w["wg"], w["bg"], w["lam"], batch, seq)
        od = _retention(p3, w["dec"], w["ret_nw"], batch, seq)
        flat = lambda o: o.reshape(tokens, GROUP_W)
        r = _post(flat(oa), flat(ob), flat(oc), flat(od), r, p, l, w["nw"], w["wo"], w["wig"], w["wiu"],
                  w["wfo"], w["wpg"], w["wpp"])
    return r.reshape(batch, seq, D_MODEL)


def kernel(x_prompt, x_sample, p_prompt, p_sample, norm_w, w_in, diff_lambda, diff_subln_w, hgrn_lb_raw,
           hgrn_norm_w, rg_conv_w, rg_conv_b, rg_w_a, rg_b_a, rg_w_x, rg_b_x, rg_lambda, ret_decay,
           ret_norm_w, w_out, w_ffn_in, w_ffn_out, w_ple_gate, w_ple_proj):
    layers = _prep_weights(norm_w, w_in, diff_lambda, diff_subln_w, hgrn_lb_raw, hgrn_norm_w, rg_conv_w,
                           rg_conv_b, rg_w_a, rg_b_a, rg_w_x, rg_b_x, rg_lambda, ret_decay, ret_norm_w,
                           w_out, w_ffn_in, w_ffn_out, w_ple_gate, w_ple_proj)
    return (_trunk(x_prompt, p_prompt, layers), _trunk(x_sample, p_sample, layers))
```

```python
import functools
import math

import jax
import jax.numpy as jnp
from jax import lax
from jax.experimental import pallas as pl
from jax.experimental.pallas import tpu as pltpu

F32 = jnp.float32
BF16 = jnp.bfloat16

D_MODEL = 1024
DEPTH = 4
GROUP_W = 256
N_HEADS = 4
HEAD_DIM = 64
DIFF_HALF = 32
D_FF = 2816
FF_CHUNK = 256
N_FF_CHUNKS = D_FF // FF_CHUNK
PLE_DIM = 256
CONV_W = 4
CONV_PAD_L = 2
RG_C = 8.0
EPS = 1e-6
LOG2E = 1.4426950408889634

(P_AK, P_BQ, P_BF_FW, P_BF_BW, P_BI, P_BG, P_CX, P_CG, P_DQ, P_DK, P_DV, P_DG) = range(12)
N_PGROUPS = 12

VMEM_LIMIT_V7X = 56 * 1024 * 1024

TM_INPROJ = 512
TM_POST = 512
TQ_ATTN = 256
TK_ATTN = 256
ATTN_SKEW = 3
ATTN_SLOTS = ATTN_SKEW + 1
ATTN_UNROLL = 4
HGRN_C = 64
HGRN_NB = 4
HGRN_MID = HGRN_C // 2
HGRN_SAFE_EXP = 80.0
RET_C = 256
RG_TB = 256
RG_HALO = 16
SUBLANES = 8


def _cparams(n_axes):
    return pltpu.CompilerParams(dimension_semantics=("arbitrary",) * n_axes,
                                vmem_limit_bytes=VMEM_LIMIT_V7X)


def _const_spec(shape):
    nd = len(shape)
    return pl.BlockSpec(shape, lambda *_: (0,) * nd, pipeline_mode=pl.Buffered(1))


def _rms(x, w):
    return x * lax.rsqrt(jnp.mean(x * x, axis=-1, keepdims=True) + EPS) * w


def _rms_heads(x, w, head_mean):
    sq = x * x
    hi = sq.astype(BF16)
    lo = (sq - hi.astype(F32)).astype(BF16)
    ms = _dot(hi, head_mean) + _dot(lo, head_mean)
    return x * lax.rsqrt(ms + EPS) * w


def _sigmoid(x):
    return jax.nn.sigmoid(x)


def _silu(x):
    return x * _sigmoid(x)


def _softplus(x):
    return jnp.maximum(x, 0.0) + jnp.log(1.0 + jnp.exp(-jnp.abs(x)))


def _dot(a, b):
    return jnp.dot(a, b, preferred_element_type=F32)


def _dot_nt(a, b):
    return lax.dot_general(a, b, (((1,), (1,)), ((), ())), preferred_element_type=F32)


def _dot_tn(a, b):
    return lax.dot_general(a, b, (((0,), (0,)), ((), ())), preferred_element_type=F32)


def _inproj_kernel(x_ref, nw_ref, wn_ref, wt_ref, p_ref, qt_ref, vt_ref):
    x = x_ref[...]
    y = _rms(x, nw_ref[...]).astype(BF16)
    for g in range(N_PGROUPS):
        cols = slice(g * GROUP_W, (g + 1) * GROUP_W)
        p_ref[:, cols] = _dot(y, wn_ref[:, cols]).astype(BF16)
    t = _dot_nt(wt_ref[...], y)
    qt_ref[0] = (t[:GROUP_W] * (DIFF_HALF ** -0.5 * LOG2E)).astype(BF16)
    vt_ref[0] = t[GROUP_W:].astype(BF16)


def _inproj(r, nw, w_nat, w_qv_t, batch, seq):
    tokens = batch * seq
    tm = TM_INPROJ
    nsb = seq // tm
    return pl.pallas_call(
        _inproj_kernel,
        grid=(tokens // tm,),
        in_specs=[
            pl.BlockSpec((tm, D_MODEL), lambda i: (i, 0)),
            _const_spec((1, D_MODEL)),
            _const_spec((D_MODEL, N_PGROUPS * GROUP_W)),
            _const_spec((2 * GROUP_W, D_MODEL)),
        ],
        out_specs=[
            pl.BlockSpec((tm, N_PGROUPS * GROUP_W), lambda i: (i, 0)),
            pl.BlockSpec((1, GROUP_W, tm), lambda i: (i // nsb, 0, i % nsb)),
            pl.BlockSpec((1, GROUP_W, tm), lambda i: (i // nsb, 0, i % nsb)),
        ],
        out_shape=[
            jax.ShapeDtypeStruct((tokens, N_PGROUPS * GROUP_W), BF16),
            jax.ShapeDtypeStruct((batch, GROUP_W, seq), BF16),
            jax.ShapeDtypeStruct((batch, GROUP_W, seq), BF16),
        ],
        compiler_params=_cparams(1),
        name="inproj",
    )(r, nw, w_nat, w_qv_t)


def _attn_kernel(qt_ref, k_ref, vt_ref, dl_ref, sw_ref, o_ref, w_ref, m_ref, l_ref, acc_ref, s_ref, t_ref,
                 *, seq, lambda_init):
    tq, tk = TQ_ATTN, TK_ATTN
    assert tq == tk
    q0 = pl.program_id(1) * tq
    nk = seq // tk
    n_chain = 2 * N_HEADS
    sub = lax.broadcasted_iota(jnp.int32, (2 * HEAD_DIM, tq), 0)
    for h in range(N_HEADS):
        half = h // 2
        qt_half = qt_ref[0, half * 2 * HEAD_DIM:(half + 1) * 2 * HEAD_DIM, :]
        for c in range(2):
            lo = (h % 2) * HEAD_DIM + c * DIFF_HALF
            w_ref[2 * h + c] = jnp.where((sub >= lo) & (sub < lo + DIFF_HALF), qt_half,
                                         jnp.zeros_like(qt_half))
    slopes2 = [2.0 ** (-8.0 * (h + 1) / N_HEADS) * LOG2E for h in range(N_HEADS)]

    @pl.when((pl.program_id(0) == 0) & (pl.program_id(1) == 0))
    def _():
        row = lax.broadcasted_iota(jnp.int32, (tk, tq), 0)
        col = lax.broadcasted_iota(jnp.int32, (tk, tq), 1)
        d = (row - col).astype(F32)
        for h in range(N_HEADS):
            t_ref[h, 0] = slopes2[h] * d
            t_ref[h, 1] = -slopes2[h] * jnp.abs(d)
            t_ref[h, 2] = -slopes2[h] * d

    def scores(jj, i):
        k0 = pl.multiple_of(jj * tk, tk)
        half = i // 4
        kt = k_ref[0, pl.ds(k0, tk), half * 2 * HEAD_DIM:(half + 1) * 2 * HEAD_DIM]
        s_ref[i % ATTN_SLOTS] = _dot(kt, w_ref[i])

    def consume(j, i, sel, off, online):
        h = i // 2
        k0 = pl.multiple_of(j * tk, tk)
        u = s_ref[i % ATTN_SLOTS] + t_ref[h, sel]
        vt = vt_ref[0, h * HEAD_DIM:(h + 1) * HEAD_DIM, pl.ds(k0, tk)]
        m = m_ref[i]
        if online:
            m_new = jnp.maximum(m, jnp.max(u, axis=0, keepdims=True) - off[h])
            p = jnp.exp2(u - (m_new + off[h]))
            alpha = jnp.exp2(m - m_new)
            acc_ref[i] = acc_ref[i] * alpha + _dot(vt, p.astype(BF16))
            l_ref[i] = l_ref[i] * alpha + jnp.sum(p, axis=0, keepdims=True)
            m_ref[i] = m_new
        else:
            p = jnp.exp2(u - (m + off[h]))
            acc_ref[i] = acc_ref[i] + _dot(vt, p.astype(BF16))
            l_ref[i] = l_ref[i] + jnp.sum(p, axis=0, keepdims=True)

    def key_sweep(unroll, online):
        assert nk % unroll == 0
        for i in range(ATTN_SKEW):
            scores(0, i)

        def body(jo, carry):
            qi = pl.program_id(1)
            for ju in range(unroll):
                j = jo * unroll + ju
                jn = jnp.minimum(j + 1, nk - 1)
                sel = (j >= qi).astype(jnp.int32) + (j > qi).astype(jnp.int32)
                gap = jnp.abs(q0 - j * tk).astype(F32)
                off = [slopes2[h] * gap for h in range(N_HEADS)]
                for i in range(n_chain):
                    a = i + ATTN_SKEW
                    if a < n_chain:
                        scores(j, a)
                    else:
                        scores(jn, a - n_chain)
                    consume(j, i, sel, off, online)
            return carry

        lax.fori_loop(0, nk // unroll, body, 0)

    for i in range(n_chain):
        h = i // 2
        half = i // 4
        kt = k_ref[0, pl.ds(pl.multiple_of(q0, tk), tk), half * 2 * HEAD_DIM:(half + 1) * 2 * HEAD_DIM]
        m_ref[i] = jnp.max(_dot(kt, w_ref[i]) + t_ref[h, 1], axis=0, keepdims=True)
    acc_ref[...] = jnp.zeros_like(acc_ref)
    l_ref[...] = jnp.zeros_like(l_ref)
    key_sweep(ATTN_UNROLL, online=False)
    poisoned = (jnp.sum(acc_ref[...] * 0.0) + jnp.sum(l_ref[...] * 0.0)) != 0.0

    @pl.when(poisoned)
    def _():
        m_ref[...] = jnp.full(m_ref.shape, -1e30, F32)
        acc_ref[...] = jnp.zeros_like(acc_ref)
        l_ref[...] = jnp.zeros_like(l_ref)
        key_sweep(1, online=True)

    dl = dl_ref[...]
    lam = (jnp.exp(jnp.sum(dl[0:1] * dl[1:2], axis=1, keepdims=True))
           - jnp.exp(jnp.sum(dl[2:3] * dl[3:4], axis=1, keepdims=True)) + lambda_init)
    outs = []
    for h in range(N_HEADS):
        o = (acc_ref[2 * h] / l_ref[2 * h] - lam * (acc_ref[2 * h + 1] / l_ref[2 * h + 1]))
        ms = jnp.mean(o * o, axis=0, keepdims=True)
        outs.append(o * lax.rsqrt(ms + EPS) * sw_ref[...] * (1.0 - lambda_init))
    o_ref[0] = jnp.concatenate(outs, axis=0).T.astype(BF16)


def _attention(qt, p3, vt, dl, sw, batch, seq, lambda_init):
    tq = TQ_ATTN
    n_chain = 2 * N_HEADS
    scratch = [
        pltpu.VMEM((n_chain, 2 * HEAD_DIM, tq), BF16),
        pltpu.VMEM((n_chain, 1, tq), F32),
        pltpu.VMEM((n_chain, 1, tq), F32),
        pltpu.VMEM((n_chain, HEAD_DIM, tq), F32),
        pltpu.VMEM((ATTN_SLOTS, TK_ATTN, tq), F32),
        pltpu.VMEM((N_HEADS, 3, TK_ATTN, tq), F32),
    ]
    return pl.pallas_call(
        functools.partial(_attn_kernel, seq=seq, lambda_init=lambda_init),
        grid=(batch, seq // tq),
        in_specs=[
            pl.BlockSpec((1, GROUP_W, tq), lambda b, i: (b, 0, i)),
            pl.BlockSpec((1, seq, GROUP_W), lambda b, i: (b, 0, P_AK), pipeline_mode=pl.Buffered(1)),
            pl.BlockSpec((1, GROUP_W, seq), lambda b, i: (b, 0, 0), pipeline_mode=pl.Buffered(1)),
            _const_spec((4, DIFF_HALF)),
            _const_spec((HEAD_DIM, 1)),
        ],
        out_specs=pl.BlockSpec((1, tq, GROUP_W), lambda b, i: (b, i, 0)),
        out_shape=jax.ShapeDtypeStruct((batch, seq, GROUP_W), BF16),
        scratch_shapes=scratch,
        compiler_params=_cparams(2),
        name="diff_attn",
    )(qt, p3, vt, dl, sw)


def _hgrn_kernel(*refs, layer, rev):
    if rev:
        (q_ref, f_ref, v_ref, lbraw_ref, part_ref, g_ref, nw_ref, o_ref,
         st_ref, qs_ref, ks_ref, fs_ref, vs_ref, os_ref) = refs
    else:
        (q_ref, f_ref, v_ref, lbraw_ref, o_ref,
         st_ref, qs_ref, ks_ref, fs_ref, vs_ref, os_ref) = refs
    C = HGRN_C
    T = HGRN_NB * C

    @pl.when(pl.program_id(1) == 0)
    def _():
        st_ref[...] = jnp.zeros_like(st_ref)

    raw = lbraw_ref[...]
    e = jnp.exp(raw - jnp.max(raw, axis=0, keepdims=True))
    soft = e / jnp.sum(e, axis=0, keepdims=True)
    lb = jnp.zeros((1, GROUP_W), F32)
    for i in range(1, layer + 1):
        lb = lb + soft[i:i + 1]

    xf = f_ref[0].astype(F32)
    sg = _sigmoid(xf)
    fg = lb + (1.0 - lb) * sg
    kk = (1.0 - lb) * (1.0 - sg)
    q = _silu(q_ref[0].astype(F32))
    v = v_ref[0]
    lf = jnp.log(fg)

    row = lax.broadcasted_iota(jnp.int32, (T, T), 0)
    col = lax.broadcasted_iota(jnp.int32, (T, T), 1)
    same = (row // C) == (col // C)
    tri = same & ((row <= col) if rev else (row >= col))
    tri_b = jnp.where(tri, 1.0, 0.0).astype(BF16)
    hi = lf.astype(BF16)
    lo = (lf - hi.astype(F32)).astype(BF16)
    cum = _dot(tri_b, hi) + _dot(tri_b, lo)

    def chunk_rows(idx):
        return jnp.concatenate(
            [jnp.broadcast_to(cum[c * C + idx:c * C + idx + 1], (C, GROUP_W)) for c in range(HGRN_NB)], axis=0)

    cm = chunk_rows(HGRN_MID)
    last = chunk_rows(0 if rev else C - 1)
    worst = jnp.max(jnp.abs(cum - cm))

    qt = (q * jnp.exp(cum - cm)).astype(BF16)
    kt = (kk * jnp.exp(cm - cum)).astype(BF16)
    qi = (q * jnp.exp(cum)).astype(BF16)
    ki = (kk * jnp.exp(last - cum)).astype(BF16)
    dl = jnp.exp(last)
    for h in range(N_HEADS):
        hs = slice(h * HEAD_DIM, (h + 1) * HEAD_DIM)
        a = jnp.where(tri, _dot_nt(qt[:, hs], kt[:, hs]), 0.0).astype(BF16)
        os_ref[:, hs] = _dot(a, v[:, hs])
    st_in = st_ref[...]
    st = st_in
    for c in (reversed(range(HGRN_NB)) if rev else range(HGRN_NB)):
        rs = slice(c * C, (c + 1) * C)
        os_ref[rs, :] = os_ref[rs, :] + _dot_nt(qi[rs], st.astype(BF16))
        st = st * dl[c * C:c * C + 1] + jnp.where(same, _dot_tn(v[rs], ki[rs]), 0.0)
    st_ref[...] = st

    @pl.when(jnp.logical_not(worst <= HGRN_SAFE_EXP))
    def _slow():
        st_ref[...] = st_in
        qs_ref[...] = q
        ks_ref[...] = kk
        fs_ref[...] = fg
        vs_ref[...] = v.astype(F32)

        def step(i, carry):
            t = (T - 1 - i) if rev else i
            k8 = jnp.broadcast_to(ks_ref[pl.ds(t, 1), :], (SUBLANES, GROUP_W))
            v8 = jnp.broadcast_to(vs_ref[pl.ds(t, 1), :], (SUBLANES, GROUP_W)) * (1.0 / SUBLANES)
            st = st_ref[...] * fs_ref[pl.ds(t, 1), :] + jnp.where(same, _dot_tn(v8, k8), 0.0)
            st_ref[...] = st
            q8 = jnp.broadcast_to(qs_ref[pl.ds(t, 1), :], (SUBLANES, GROUP_W))
            os_ref[pl.ds(t, 1), :] = _dot_nt(q8, st)[0:1]
            return carry

        lax.fori_loop(0, T, step, 0)

    if rev:
        o = os_ref[...] + part_ref[0]
        g = g_ref[0].astype(F32)
        head_mean = jnp.where(same, 1.0 / HEAD_DIM, 0.0).astype(BF16)
        o_ref[0] = (_rms_heads(o, nw_ref[...], head_mean) * _silu(g)).astype(BF16)
    else:
        o_ref[0] = os_ref[...]


def _hgrn(p3, lb_raw, norm_w, batch, seq, layer):
    C = HGRN_NB * HGRN_C
    assert C == GROUP_W
    nc = seq // C
    scratch = [pltpu.VMEM((GROUP_W, GROUP_W), F32)] + [pltpu.VMEM((C, GROUP_W), F32)] * 5

    def grp(g, rev):
        if rev:
            return pl.BlockSpec((1, C, GROUP_W), lambda b, j: (b, nc - 1 - j, g))
        return pl.BlockSpec((1, C, GROUP_W), lambda b, j: (b, j, g))

    part = pl.pallas_call(
        functools.partial(_hgrn_kernel, layer=layer, rev=False),
        grid=(batch, nc),
        in_specs=[grp(P_BQ, False), grp(P_BF_FW, False), grp(P_BI, False), _const_spec((DEPTH, GROUP_W))],
        out_specs=pl.BlockSpec((1, C, GROUP_W), lambda b, j: (b, j, 0)),
        out_shape=jax.ShapeDtypeStruct((batch, seq, GROUP_W), F32),
        scratch_shapes=scratch,
        compiler_params=_cparams(2),
        name="hgrn_fw",
    )(p3, p3, p3, lb_raw[:, 0])
    return pl.pallas_call(
        functools.partial(_hgrn_kernel, layer=layer, rev=True),
        grid=(batch, nc),
        in_specs=[grp(P_BQ, True), grp(P_BF_BW, True), grp(P_BI, True), _const_spec((DEPTH, GROUP_W)),
                  pl.BlockSpec((1, C, GROUP_W), lambda b, j: (b, nc - 1 - j, 0)),
                  grp(P_BG, True), _const_spec((1, GROUP_W))],
        out_specs=pl.BlockSpec((1, C, GROUP_W), lambda b, j: (b, nc - 1 - j, 0)),
        out_shape=jax.ShapeDtypeStruct((batch, seq, GROUP_W), BF16),
        scratch_shapes=scratch,
        compiler_params=_cparams(2),
        name="hgrn_bw",
    )(p3, p3, p3, lb_raw[:, 1], part, p3, norm_w)


def _log_sigmoid(x):
    return jnp.minimum(x, 0.0) - jnp.log(1.0 + jnp.exp(-jnp.abs(x)))


def _ret_kernel(*refs, rev):
    if rev:
        q_ref, k_ref, v_ref, dec_ref, part_ref, g_ref, nw_ref, o_ref, st_ref, xi_ref, zeta_ref = refs
    else:
        q_ref, k_ref, v_ref, dec_ref, o_ref, st_ref, xi_ref, zeta_ref, dm_ref = refs
    C = RET_C

    @pl.when(pl.program_id(1) == 0)
    def _():
        st_ref[...] = jnp.zeros_like(st_ref)

    lg_f = _log_sigmoid(dec_ref[0:1, :])
    lg_b = _log_sigmoid(dec_ref[1:2, :])
    lg = lg_b if rev else lg_f
    if not rev:
        @pl.when((pl.program_id(0) == 0) & (pl.program_id(1) == 0))
        def _():
            row = lax.broadcasted_iota(jnp.int32, (C, C), 0)
            col = lax.broadcasted_iota(jnp.int32, (C, C), 1)
            d_ts = (row - col).astype(F32)
            for h in range(N_HEADS):
                lf1 = lg_f[:, h * HEAD_DIM:h * HEAD_DIM + 1]
                lb1 = lg_b[:, h * HEAD_DIM:h * HEAD_DIM + 1]
                dm_ref[h] = (jnp.where(d_ts >= 0, jnp.exp(jnp.maximum(d_ts, 0.0) * lf1), 0.0)
                             + jnp.where(d_ts <= 0, jnp.exp(jnp.maximum(-d_ts, 0.0) * lb1), 0.0))

    @pl.when((pl.program_id(0) == 0) & (pl.program_id(1) == 0))
    def _():
        t = lax.broadcasted_iota(jnp.int32, (C, 1), 0).astype(F32)
        if rev:
            xi_ref[...] = jnp.exp((C - t) * lg)
            zeta_ref[...] = jnp.exp(t * lg)
        else:
            xi_ref[...] = jnp.exp((t + 1.0) * lg)
            zeta_ref[...] = jnp.exp((C - 1.0 - t) * lg)

    q = q_ref[0].astype(F32)
    k = k_ref[0].astype(F32) * HEAD_DIM ** -0.5
    v = v_ref[0]
    cd = jnp.exp(C * lg)
    qx = (q * xi_ref[...]).astype(BF16)
    kz = (k * zeta_ref[...]).astype(BF16)
    if not rev:
        qb = q.astype(BF16)
        kb = k.astype(BF16)
    outs = []
    for h in range(N_HEADS):
        hs = slice(h * HEAD_DIM, (h + 1) * HEAD_DIM)
        st = st_ref[h]
        o = _dot(qx[:, hs], st.astype(BF16))
        if not rev:
            sc = (_dot_nt(qb[:, hs], kb[:, hs]) * dm_ref[h]).astype(BF16)
            o = o + _dot(sc, v[:, hs])
        st_ref[h] = st * cd[:, h * HEAD_DIM:h * HEAD_DIM + 1] + _dot_tn(kz[:, hs], v[:, hs])
        outs.append(o)
    if rev:
        g = g_ref[0].astype(F32)
        row = lax.broadcasted_iota(jnp.int32, (GROUP_W, GROUP_W), 0)
        col = lax.broadcasted_iota(jnp.int32, (GROUP_W, GROUP_W), 1)
        head_mean = jnp.where(row // HEAD_DIM == col // HEAD_DIM, 1.0 / HEAD_DIM, 0.0).astype(BF16)
        o = jnp.concatenate(outs, axis=1) + part_ref[0]
        o_ref[0] = (_rms_heads(o, nw_ref[...], head_mean) * _silu(g)).astype(BF16)
    else:
        o_ref[0] = jnp.concatenate(outs, axis=1)


def _retention(p3, dec_lanes, norm_w, batch, seq):
    C = RET_C
    nc = seq // C
    scratch = [pltpu.VMEM((N_HEADS, HEAD_DIM, HEAD_DIM), F32)] + [pltpu.VMEM((C, GROUP_W), F32)] * 2

    def grp(g, rev):
        if rev:
            return pl.BlockSpec((1, C, GROUP_W), lambda b, j: (b, nc - 1 - j, g))
        return pl.BlockSpec((1, C, GROUP_W), lambda b, j: (b, j, g))

    part = pl.pallas_call(
        functools.partial(_ret_kernel, rev=False),
        grid=(batch, nc),
        in_specs=[grp(P_DQ, False), grp(P_DK, False), grp(P_DV, False), _const_spec((2, GROUP_W))],
        out_specs=pl.BlockSpec((1, C, GROUP_W), lambda b, j: (b, j, 0)),
        out_shape=jax.ShapeDtypeStruct((batch, seq, GROUP_W), F32),
        scratch_shapes=scratch + [pltpu.VMEM((N_HEADS, C, C), F32)],
        compiler_params=_cparams(2),
        name="ret_fw",
    )(p3, p3, p3, dec_lanes)
    return pl.pallas_call(
        functools.partial(_ret_kernel, rev=True),
        grid=(batch, nc),
        in_specs=[grp(P_DQ, True), grp(P_DK, True), grp(P_DV, True), _const_spec((2, GROUP_W)),
                  pl.BlockSpec((1, C, GROUP_W), lambda b, j: (b, nc - 1 - j, 0)),
                  grp(P_DG, True), _const_spec((1, GROUP_W))],
        out_specs=pl.BlockSpec((1, C, GROUP_W), lambda b, j: (b, nc - 1 - j, 0)),
        out_shape=jax.ShapeDtypeStruct((batch, seq, GROUP_W), BF16),
        scratch_shapes=scratch,
        compiler_params=_cparams(2),
        name="ret_bw",
    )(p3, p3, p3, dec_lanes, part, p3, norm_w)


def _rglru_kernel(*refs, rev, nblk):
    if rev:
        (x_ref, xp_ref, xn_ref, cw_ref, cb_ref, wg_ref, bg_ref, lam_ref, part_ref, gate_ref,
         o_ref, carry_ref, a_ref, u_ref, h_ref) = refs
    else:
        (x_ref, xp_ref, xn_ref, cw_ref, cb_ref, wg_ref, bg_ref, lam_ref,
         o_ref, carry_ref, a_ref, u_ref, h_ref) = refs
    TB = RG_TB
    j = pl.program_id(1)
    blk = (nblk - 1 - j) if rev else j

    @pl.when(j == 0)
    def _():
        carry_ref[...] = jnp.zeros_like(carry_ref)

    prev = jnp.where(blk > 0, xp_ref[0].astype(F32), 0.0)
    nxt = jnp.where(blk < nblk - 1, xn_ref[0].astype(F32), 0.0)
    xe = jnp.concatenate([prev, x_ref[0].astype(F32), nxt], axis=0)
    cw = cw_ref[...]
    xc = cb_ref[...] + sum(
        xe[RG_HALO - CONV_PAD_L + w:RG_HALO - CONV_PAD_L + w + TB] * cw[w:w + 1] for w in range(CONV_W))
    gates = _dot(xc.astype(BF16), wg_ref[...]) + bg_ref[...]
    r = _sigmoid(gates[:, :GROUP_W])
    ig = _sigmoid(gates[:, GROUP_W:])
    log_a = -RG_C * r * _softplus(-lam_ref[...])
    a = jnp.exp(log_a)
    one_m_a2 = 1.0 - jnp.exp(2.0 * log_a)
    root = jnp.where(one_m_a2 > 0.0, one_m_a2 * lax.rsqrt(one_m_a2), 0.0)
    u = root * (ig * xc)

    pos = lax.broadcasted_iota(jnp.int32, (TB, GROUP_W), 0) % SUBLANES
    for d in (1, 2, 4):
        if rev:
            keep = pos < SUBLANES - d
            shift = TB - d
        else:
            keep = pos >= d
            shift = d
        a_s = jnp.where(keep, pltpu.roll(a, shift, 0), 1.0)
        u_s = jnp.where(keep, pltpu.roll(u, shift, 0), 0.0)
        u = a * u_s + u
        a = a * a_s
    a_ref[...] = a
    u_ref[...] = u
    ntile = TB // SUBLANES

    def tile_step(i, hprev):
        ti = (ntile - 1 - i) if rev else i
        r0 = pl.multiple_of(ti * SUBLANES, SUBLANES)
        h = a_ref[pl.ds(r0, SUBLANES), :] * hprev + u_ref[pl.ds(r0, SUBLANES), :]
        h_ref[pl.ds(r0, SUBLANES), :] = h
        edge = h[0:1] if rev else h[SUBLANES - 1:SUBLANES]
        return jnp.broadcast_to(edge, (SUBLANES, GROUP_W))

    carry_ref[...] = lax.fori_loop(0, ntile, tile_step, carry_ref[...])
    if rev:
        hs = h_ref[...] + part_ref[0]
        o_ref[0] = (hs * jax.nn.gelu(gate_ref[0].astype(F32), approximate=True)).astype(BF16)
    else:
        o_ref[0] = h_ref[...]


def _rglru(p3, conv_w, conv_b, wg, bg, lam, batch, seq):
    TB = RG_TB
    nblk = seq // TB
    hpb = TB // RG_HALO
    nhalo = seq // RG_HALO
    scratch = [pltpu.VMEM((SUBLANES, GROUP_W), F32)] + [pltpu.VMEM((TB, GROUP_W), F32)] * 3

    def specs(rev):
        def blk(j):
            return (nblk - 1 - j) if rev else j
        return [
            pl.BlockSpec((1, TB, GROUP_W), lambda b, j: (b, blk(j), P_CX)),
            pl.BlockSpec((1, RG_HALO, GROUP_W), lambda b, j: (b, jnp.maximum(blk(j) * hpb - 1, 0), P_CX)),
            pl.BlockSpec((1, RG_HALO, GROUP_W),
                         lambda b, j: (b, jnp.minimum((blk(j) + 1) * hpb, nhalo - 1), P_CX)),
            _const_spec((CONV_W, GROUP_W)),
            _const_spec((1, GROUP_W)),
            _const_spec((GROUP_W, 2 * GROUP_W)),
            _const_spec((1, 2 * GROUP_W)),
            _const_spec((1, GROUP_W)),
        ]

    part = pl.pallas_call(
        functools.partial(_rglru_kernel, rev=False, nblk=nblk),
        grid=(batch, nblk),
        in_specs=specs(False),
        out_specs=pl.BlockSpec((1, TB, GROUP_W), lambda b, j: (b, j, 0)),
        out_shape=jax.ShapeDtypeStruct((batch, seq, GROUP_W), F32),
        scratch_shapes=scratch,
        compiler_params=_cparams(2),
        name="rglru_fw",
    )(p3, p3, p3, conv_w, conv_b, wg[0], bg[0], lam[0:1])
    return pl.pallas_call(
        functools.partial(_rglru_kernel, rev=True, nblk=nblk),
        grid=(batch, nblk),
        in_specs=specs(True) + [
            pl.BlockSpec((1, TB, GROUP_W), lambda b, j: (b, nblk - 1 - j, 0)),
            pl.BlockSpec((1, TB, GROUP_W), lambda b, j: (b, nblk - 1 - j, P_CG)),
        ],
        out_specs=pl.BlockSpec((1, TB, GROUP_W), lambda b, j: (b, nblk - 1 - j, 0)),
        out_shape=jax.ShapeDtypeStruct((batch, seq, GROUP_W), BF16),
        scratch_shapes=scratch,
        compiler_params=_cparams(2),
        name="rglru_bw",
    )(p3, p3, p3, conv_w, conv_b, wg[1], bg[1], lam[1:2], part, p3)


def _post_kernel(oa_ref, ob_ref, oc_ref, od_ref, r_ref, p_ref, nw_ref, wo_ref, wig_ref, wiu_ref, wfo_ref,
                 wpg_ref, wpp_ref, out_ref):
    nw = nw_ref[...]
    mixed_in = jnp.concatenate([oa_ref[...], ob_ref[...], oc_ref[...], od_ref[...]], axis=1)
    r = r_ref[...] + _rms(_dot(mixed_in, wo_ref[...]), nw[1:2])
    h2 = _rms(r, nw[2:3]).astype(BF16)
    ff = jnp.zeros_like(r)
    for c in range(N_FF_CHUNKS):
        act = _silu(_dot(h2, wig_ref[c])) * _dot(h2, wiu_ref[c])
        ff = ff + _dot(act.astype(BF16), wfo_ref[c])
    r = r + _rms(ff, nw[3:4])
    gate = _sigmoid(_dot(r.astype(BF16), wpg_ref[...]))
    out_ref[...] = r + gate * _dot(p_ref[0].astype(BF16), wpp_ref[...])


def _post(oa, ob, oc, od, r, p, layer, nw, wo, wig, wiu, wfo, wpg, wpp):
    tokens = r.shape[0]
    tm = TM_POST
    mix_spec = pl.BlockSpec((tm, GROUP_W), lambda i: (i, 0))
    return pl.pallas_call(
        _post_kernel,
        grid=(tokens // tm,),
        in_specs=[
            mix_spec, mix_spec, mix_spec, mix_spec,
            pl.BlockSpec((tm, D_MODEL), lambda i: (i, 0)),
            pl.BlockSpec((1, tm, PLE_DIM), lambda i: (layer, i, 0)),
            _const_spec((4, D_MODEL)),
            _const_spec((D_MODEL, D_MODEL)),
            _const_spec((N_FF_CHUNKS, D_MODEL, FF_CHUNK)),
            _const_spec((N_FF_CHUNKS, D_MODEL, FF_CHUNK)),
            _const_spec((N_FF_CHUNKS, FF_CHUNK, D_MODEL)),
            _const_spec((D_MODEL, D_MODEL)),
            _const_spec((PLE_DIM, D_MODEL)),
        ],
        out_specs=pl.BlockSpec((tm, D_MODEL), lambda i: (i, 0)),
        out_shape=jax.ShapeDtypeStruct((tokens, D_MODEL), F32),
        compiler_params=_cparams(1),
        name="post",
    )(oa, ob, oc, od, r, p, nw, wo, wig, wiu, wfo, wpg, wpp)


def _block_diag(w):
    out = jnp.zeros((GROUP_W, GROUP_W), w.dtype)
    for h in range(N_HEADS):
        out = out.at[h * HEAD_DIM:(h + 1) * HEAD_DIM, h * HEAD_DIM:(h + 1) * HEAD_DIM].set(w[h])
    return out


def _prep_weights(norm_w, w_in, diff_lambda, diff_subln_w, hgrn_lb_raw, hgrn_norm_w, rg_conv_w, rg_conv_b,
                  rg_w_a, rg_b_a, rg_w_x, rg_b_x, rg_lambda, ret_decay, ret_norm_w, w_out, w_ffn_in,
                  w_ffn_out, w_ple_gate, w_ple_proj):
    layers = []
    for l in range(DEPTH):
        wi = w_in[l]
        w_nat = jnp.concatenate([wi[:, GROUP_W:2 * GROUP_W], wi[:, 3 * GROUP_W:]], axis=1).astype(BF16)
        w_qv_t = jnp.concatenate([wi[:, :GROUP_W], wi[:, 2 * GROUP_W:3 * GROUP_W]], axis=1).T.astype(BF16)
        wg = [jnp.concatenate([_block_diag(rg_w_a[l, d]), _block_diag(rg_w_x[l, d])], axis=1).astype(BF16)
              for d in range(2)]
        bg = [jnp.concatenate([rg_b_a[l, d], rg_b_x[l, d]])[None, :] for d in range(2)]
        wfi = w_ffn_in[l].astype(BF16)
        layers.append(dict(
            nw=norm_w[l], nw0=norm_w[l, 0:1], w_nat=w_nat, w_qv_t=w_qv_t,
            dl=diff_lambda[l], sw=diff_subln_w[l][:, None],
            lb_raw=hgrn_lb_raw, hgrn_nw=jnp.tile(hgrn_norm_w[l], N_HEADS)[None, :],
            conv_w=rg_conv_w[l], conv_b=rg_conv_b[l][None, :], wg=wg, bg=bg, lam=rg_lambda[l],
            dec=jnp.repeat(ret_decay[l], HEAD_DIM, axis=-1), ret_nw=jnp.tile(ret_norm_w[l], N_HEADS)[None, :],
            wo=w_out[l].astype(BF16),
            wig=wfi[:, :D_FF].reshape(D_MODEL, N_FF_CHUNKS, FF_CHUNK).transpose(1, 0, 2),
            wiu=wfi[:, D_FF:].reshape(D_MODEL, N_FF_CHUNKS, FF_CHUNK).transpose(1, 0, 2),
            wfo=w_ffn_out[l].astype(BF16).reshape(N_FF_CHUNKS, FF_CHUNK, D_MODEL),
            wpg=w_ple_gate[l].astype(BF16), wpp=w_ple_proj[l].astype(BF16),
        ))
    return layers


def _trunk(x, p, layers):
    batch, seq, _ = x.shape
    tokens = batch * seq
    r = x.reshape(tokens, D_MODEL)
    p = p.reshape(DEPTH, tokens, PLE_DIM)
    for l, w in enumerate(layers):
        lambda_init = 0.8 - 0.6 * math.exp(-0.3 * l)
        pn, qt, vt = _inproj(r, w["nw0"], w["w_nat"], w["w_qv_t"], batch, seq)
        p3 = pn.reshape(batch, seq, N_PGROUPS * GROUP_W)
        oa = _attention(qt, p3, vt, w["dl"], w["sw"], batch, seq, lambda_init)
        ob = _hgrn(p3, w["lb_raw"], w["hgrn_nw"], batch, seq, l)
        oc = _rglru(p3, w["conv_w"], w["conv_b"], w["wg"], w["bg"], w["lam"], batch, seq)
        od = _retention(p3, w["dec"], w["ret_nw"], batch, seq)
        flat = lambda o: o.reshape(tokens, GROUP_W)
        r = _post(flat(oa), flat(ob), flat(oc), flat(od), r, p, l, w["nw"], w["wo"], w["wig"], w["wiu"],
                  w["wfo"], w["wpg"], w["wpp"])
    return r.reshape(batch, seq, D_MODEL)


def kernel(x_prompt, x_sample, p_prompt, p_sample, norm_w, w_in, diff_lambda, diff_subln_w, hgrn_lb_raw,
           hgrn_norm_w, rg_conv_w, rg_conv_b, rg_w_a, rg_b_a, rg_w_x, rg_b_x, rg_lambda, ret_decay,
           ret_norm_w, w_out, w_ffn_in, w_ffn_out, w_ple_gate, w_ple_proj):
    layers = _prep_weights(norm_w, w_in, diff_lambda, diff_subln_w, hgrn_lb_raw, hgrn_norm_w, rg_conv_w,
                           rg_conv_b, rg_w_a, rg_b_a, rg_w_x, rg_b_x, rg_lambda, ret_decay, ret_norm_w,
                           w_out, w_ffn_in, w_ffn_out, w_ple_gate, w_ple_proj)
    return (_trunk(x_prompt, p_prompt, layers), _trunk(x_sample, p_sample, layers))
```

```python
import functools
import math

import jax
import jax.numpy as jnp
from jax import lax
from jax.experimental import pallas as pl
from jax.experimental.pallas import tpu as pltpu

F32 = jnp.float32
BF16 = jnp.bfloat16

D_MODEL = 1024
DEPTH = 4
GROUP_W = 256
N_HEADS = 4
HEAD_DIM = 64
DIFF_HALF = 32
D_FF = 2816
FF_CHUNK = 256
N_FF_CHUNKS = D_FF // FF_CHUNK
PLE_DIM = 256
CONV_W = 4
CONV_PAD_L = 2
RG_C = 8.0
EPS = 1e-6
LOG2E = 1.4426950408889634

(P_AK, P_BQ, P_BF_FW, P_BF_BW, P_BI, P_BG, P_CX, P_CG, P_DQ, P_DK, P_DV, P_DG) = range(12)
N_PGROUPS = 12

VMEM_LIMIT_V7X = 56 * 1024 * 1024

TM_INPROJ = 512
TM_POST = 512
TQ_ATTN = 256
TK_ATTN = 256
ATTN_SKEW = 3
ATTN_SLOTS = ATTN_SKEW + 1
ATTN_UNROLL = 8
HGRN_C = 64
HGRN_NB = 4
HGRN_MID = HGRN_C // 2
HGRN_SAFE_EXP = 80.0
RET_C = 256
RG_TB = 256
RG_HALO = 16
SUBLANES = 8


def _cparams(n_axes):
    return pltpu.CompilerParams(dimension_semantics=("arbitrary",) * n_axes,
                                vmem_limit_bytes=VMEM_LIMIT_V7X)


def _const_spec(shape):
    nd = len(shape)
    return pl.BlockSpec(shape, lambda *_: (0,) * nd, pipeline_mode=pl.Buffered(1))


def _rms(x, w):
    return x * lax.rsqrt(jnp.mean(x * x, axis=-1, keepdims=True) + EPS) * w


def _rms_heads(x, w, head_mean):
    sq = x * x
    hi = sq.astype(BF16)
    lo = (sq - hi.astype(F32)).astype(BF16)
    ms = _dot(hi, head_mean) + _dot(lo, head_mean)
    return x * lax.rsqrt(ms + EPS) * w


def _sigmoid(x):
    return jax.nn.sigmoid(x)


def _silu(x):
    return x * _sigmoid(x)


def _softplus(x):
    return jnp.maximum(x, 0.0) + jnp.log(1.0 + jnp.exp(-jnp.abs(x)))


def _dot(a, b):
    return jnp.dot(a, b, preferred_element_type=F32)


def _dot_nt(a, b):
    return lax.dot_general(a, b, (((1,), (1,)), ((), ())), preferred_element_type=F32)


def _dot_tn(a, b):
    return lax.dot_general(a, b, (((0,), (0,)), ((), ())), preferred_element_type=F32)


def _inproj_kernel(x_ref, nw_ref, wn_ref, wt_ref, p_ref, qt_ref, vt_ref):
    x = x_ref[...]
    y = _rms(x, nw_ref[...]).astype(BF16)
    for g in range(N_PGROUPS):
        cols = slice(g * GROUP_W, (g + 1) * GROUP_W)
        p_ref[:, cols] = _dot(y, wn_ref[:, cols]).astype(BF16)
    t = _dot_nt(wt_ref[...], y)
    qt_ref[0] = (t[:GROUP_W] * (DIFF_HALF ** -0.5 * LOG2E)).astype(BF16)
    vt_ref[0] = t[GROUP_W:].astype(BF16)


def _inproj(r, nw, w_nat, w_qv_t, batch, seq):
    tokens = batch * seq
    tm = TM_INPROJ
    nsb = seq // tm
    return pl.pallas_call(
        _inproj_kernel,
        grid=(tokens // tm,),
        in_specs=[
            pl.BlockSpec((tm, D_MODEL), lambda i: (i, 0)),
            _const_spec((1, D_MODEL)),
            _const_spec((D_MODEL, N_PGROUPS * GROUP_W)),
            _const_spec((2 * GROUP_W, D_MODEL)),
        ],
        out_specs=[
            pl.BlockSpec((tm, N_PGROUPS * GROUP_W), lambda i: (i, 0)),
            pl.BlockSpec((1, GROUP_W, tm), lambda i: (i // nsb, 0, i % nsb)),
            pl.BlockSpec((1, GROUP_W, tm), lambda i: (i // nsb, 0, i % nsb)),
        ],
        out_shape=[
            jax.ShapeDtypeStruct((tokens, N_PGROUPS * GROUP_W), BF16),
            jax.ShapeDtypeStruct((batch, GROUP_W, seq), BF16),
            jax.ShapeDtypeStruct((batch, GROUP_W, seq), BF16),
        ],
        compiler_params=_cparams(1),
        name="inproj",
    )(r, nw, w_nat, w_qv_t)


def _attn_kernel(qt_ref, k_ref, vt_ref, dl_ref, sw_ref, o_ref, w_ref, m_ref, l_ref, acc_ref, s_ref, t_ref,
                 *, seq, lambda_init):
    tq, tk = TQ_ATTN, TK_ATTN
    assert tq == tk
    q0 = pl.program_id(1) * tq
    nk = seq // tk
    n_chain = 2 * N_HEADS
    sub = lax.broadcasted_iota(jnp.int32, (2 * HEAD_DIM, tq), 0)
    for h in range(N_HEADS):
        half = h // 2
        qt_half = qt_ref[0, half * 2 * HEAD_DIM:(half + 1) * 2 * HEAD_DIM, :]
        for c in range(2):
            lo = (h % 2) * HEAD_DIM + c * DIFF_HALF
            w_ref[2 * h + c] = jnp.where((sub >= lo) & (sub < lo + DIFF_HALF), qt_half,
                                         jnp.zeros_like(qt_half))
    slopes2 = [2.0 ** (-8.0 * (h + 1) / N_HEADS) * LOG2E for h in range(N_HEADS)]

    @pl.when((pl.program_id(0) == 0) & (pl.program_id(1) == 0))
    def _():
        row = lax.broadcasted_iota(jnp.int32, (tk, tq), 0)
        col = lax.broadcasted_iota(jnp.int32, (tk, tq), 1)
        d = (row - col).astype(F32)
        for h in range(N_HEADS):
            t_ref[h, 0] = slopes2[h] * d
            t_ref[h, 1] = -slopes2[h] * jnp.abs(d)
            t_ref[h, 2] = -slopes2[h] * d

    def scores(jj, i):
        k0 = pl.multiple_of(jj * tk, tk)
        half = i // 4
        kt = k_ref[0, pl.ds(k0, tk), half * 2 * HEAD_DIM:(half + 1) * 2 * HEAD_DIM]
        s_ref[i % ATTN_SLOTS] = _dot(kt, w_ref[i])

    def consume(j, i, sel, off, online):
        h = i // 2
        k0 = pl.multiple_of(j * tk, tk)
        u = s_ref[i % ATTN_SLOTS] + t_ref[h, sel]
        vt = vt_ref[0, h * HEAD_DIM:(h + 1) * HEAD_DIM, pl.ds(k0, tk)]
        m = m_ref[i]
        if online:
            m_new = jnp.maximum(m, jnp.max(u, axis=0, keepdims=True) - off[h])
            p = jnp.exp2(u - (m_new + off[h]))
            alpha = jnp.exp2(m - m_new)
            acc_ref[i] = acc_ref[i] * alpha + _dot(vt, p.astype(BF16))
            l_ref[i] = l_ref[i] * alpha + jnp.sum(p, axis=0, keepdims=True)
            m_ref[i] = m_new
        else:
            p = jnp.exp2(u - (m + off[h]))
            acc_ref[i] = acc_ref[i] + _dot(vt, p.astype(BF16))
            l_ref[i] = l_ref[i] + jnp.sum(p, axis=0, keepdims=True)

    def key_sweep(unroll, online):
        assert nk % unroll == 0
        for i in range(ATTN_SKEW):
            scores(0, i)

        def body(jo, carry):
            qi = pl.program_id(1)
            for ju in range(unroll):
                j = jo * unroll + ju
                jn = jnp.minimum(j + 1, nk - 1)
                sel = (j >= qi).astype(jnp.int32) + (j > qi).astype(jnp.int32)
                gap = jnp.abs(q0 - j * tk).astype(F32)
                off = [slopes2[h] * gap for h in range(N_HEADS)]
                for i in range(n_chain):
                    a = i + ATTN_SKEW
                    if a < n_chain:
                        scores(j, a)
                    else:
                        scores(jn, a - n_chain)
                    consume(j, i, sel, off, online)
            return carry

        lax.fori_loop(0, nk // unroll, body, 0)

    for i in range(n_chain):
        h = i // 2
        half = i // 4
        kt = k_ref[0, pl.ds(pl.multiple_of(q0, tk), tk), half * 2 * HEAD_DIM:(half + 1) * 2 * HEAD_DIM]
        m_ref[i] = jnp.max(_dot(kt, w_ref[i]) + t_ref[h, 1], axis=0, keepdims=True)
    dl = dl_ref[...]
    lam = (jnp.exp(jnp.sum(dl[0:1] * dl[1:2], axis=1, keepdims=True))
           - jnp.exp(jnp.sum(dl[2:3] * dl[3:4], axis=1, keepdims=True)) + lambda_init)

    def finish():
        outs = []
        for h in range(N_HEADS):
            o = (acc_ref[2 * h] / l_ref[2 * h] - lam * (acc_ref[2 * h + 1] / l_ref[2 * h + 1]))
            ms = jnp.mean(o * o, axis=0, keepdims=True)
            outs.append(o * lax.rsqrt(ms + EPS) * sw_ref[...] * (1.0 - lambda_init))
        o_ref[0] = jnp.concatenate(outs, axis=0).T.astype(BF16)

    acc_ref[...] = jnp.zeros_like(acc_ref)
    l_ref[...] = jnp.zeros_like(l_ref)
    key_sweep(ATTN_UNROLL, online=False)
    poisoned = (jnp.sum(acc_ref[...] * 0.0) + jnp.sum(l_ref[...] * 0.0)) != 0.0
    finish()

    @pl.when(poisoned)
    def _():
        m_ref[...] = jnp.full(m_ref.shape, -1e30, F32)
        acc_ref[...] = jnp.zeros_like(acc_ref)
        l_ref[...] = jnp.zeros_like(l_ref)
        key_sweep(1, online=True)
        finish()


def _attention(qt, p3, vt, dl, sw, batch, seq, lambda_init):
    tq = TQ_ATTN
    n_chain = 2 * N_HEADS
    scratch = [
        pltpu.VMEM((n_chain, 2 * HEAD_DIM, tq), BF16),
        pltpu.VMEM((n_chain, 1, tq), F32),
        pltpu.VMEM((n_chain, 1, tq), F32),
        pltpu.VMEM((n_chain, HEAD_DIM, tq), F32),
        pltpu.VMEM((ATTN_SLOTS, TK_ATTN, tq), F32),
        pltpu.VMEM((N_HEADS, 3, TK_ATTN, tq), F32),
    ]
    return pl.pallas_call(
        functools.partial(_attn_kernel, seq=seq, lambda_init=lambda_init),
        grid=(batch, seq // tq),
        in_specs=[
            pl.BlockSpec((1, GROUP_W, tq), lambda b, i: (b, 0, i)),
            pl.BlockSpec((1, seq, GROUP_W), lambda b, i: (b, 0, P_AK), pipeline_mode=pl.Buffered(1)),
            pl.BlockSpec((1, GROUP_W, seq), lambda b, i: (b, 0, 0), pipeline_mode=pl.Buffered(1)),
            _const_spec((4, DIFF_HALF)),
            _const_spec((HEAD_DIM, 1)),
        ],
        out_specs=pl.BlockSpec((1, tq, GROUP_W), lambda b, i: (b, i, 0)),
        out_shape=jax.ShapeDtypeStruct((batch, seq, GROUP_W), BF16),
        scratch_shapes=scratch,
        compiler_params=_cparams(2),
        name="diff_attn",
    )(qt, p3, vt, dl, sw)


def _hgrn_kernel(*refs, layer, rev):
    if rev:
        (q_ref, f_ref, v_ref, lbraw_ref, part_ref, g_ref, nw_ref, o_ref,
         st_ref, qs_ref, ks_ref, fs_ref, vs_ref, os_ref) = refs
    else:
        (q_ref, f_ref, v_ref, lbraw_ref, o_ref,
         st_ref, qs_ref, ks_ref, fs_ref, vs_ref, os_ref) = refs
    C = HGRN_C
    T = HGRN_NB * C

    @pl.when(pl.program_id(1) == 0)
    def _():
        st_ref[...] = jnp.zeros_like(st_ref)

    raw = lbraw_ref[...]
    e = jnp.exp(raw - jnp.max(raw, axis=0, keepdims=True))
    soft = e / jnp.sum(e, axis=0, keepdims=True)
    lb = jnp.zeros((1, GROUP_W), F32)
    for i in range(1, layer + 1):
        lb = lb + soft[i:i + 1]

    xf = f_ref[0].astype(F32)
    sg = _sigmoid(xf)
    fg = lb + (1.0 - lb) * sg
    kk = (1.0 - lb) * (1.0 - sg)
    q = _silu(q_ref[0].astype(F32))
    v = v_ref[0]
    lf = jnp.log(fg)

    row = lax.broadcasted_iota(jnp.int32, (T, T), 0)
    col = lax.broadcasted_iota(jnp.int32, (T, T), 1)
    same = (row // C) == (col // C)
    tri = same & ((row <= col) if rev else (row >= col))
    tri_b = jnp.where(tri, 1.0, 0.0).astype(BF16)
    hi = lf.astype(BF16)
    lo = (lf - hi.astype(F32)).astype(BF16)
    cum = _dot(tri_b, hi) + _dot(tri_b, lo)

    def chunk_rows(idx):
        return jnp.concatenate(
            [jnp.broadcast_to(cum[c * C + idx:c * C + idx + 1], (C, GROUP_W)) for c in range(HGRN_NB)], axis=0)

    cm = chunk_rows(HGRN_MID)
    last = chunk_rows(0 if rev else C - 1)
    worst = jnp.max(jnp.abs(cum - cm))

    qt = (q * jnp.exp(cum - cm)).astype(BF16)
    kt = (kk * jnp.exp(cm - cum)).astype(BF16)
    qi = (q * jnp.exp(cum)).astype(BF16)
    ki = (kk * jnp.exp(last - cum)).astype(BF16)
    dl = jnp.exp(last)
    for h in range(N_HEADS):
        hs = slice(h * HEAD_DIM, (h + 1) * HEAD_DIM)
        a = jnp.where(tri, _dot_nt(qt[:, hs], kt[:, hs]), 0.0).astype(BF16)
        os_ref[:, hs] = _dot(a, v[:, hs])
    st_in = st_ref[...]
    st = st_in
    for c in (reversed(range(HGRN_NB)) if rev else range(HGRN_NB)):
        rs = slice(c * C, (c + 1) * C)
        os_ref[rs, :] = os_ref[rs, :] + _dot_nt(qi[rs], st.astype(BF16))
        st = st * dl[c * C:c * C + 1] + jnp.where(same, _dot_tn(v[rs], ki[rs]), 0.0)
    st_ref[...] = st

    @pl.when(jnp.logical_not(worst <= HGRN_SAFE_EXP))
    def _slow():
        st_ref[...] = st_in
        qs_ref[...] = q
        ks_ref[...] = kk
        fs_ref[...] = fg
        vs_ref[...] = v.astype(F32)

        def step(i, carry):
            t = (T - 1 - i) if rev else i
            k8 = jnp.broadcast_to(ks_ref[pl.ds(t, 1), :], (SUBLANES, GROUP_W))
            v8 = jnp.broadcast_to(vs_ref[pl.ds(t, 1), :], (SUBLANES, GROUP_W)) * (1.0 / SUBLANES)
            st = st_ref[...] * fs_ref[pl.ds(t, 1), :] + jnp.where(same, _dot_tn(v8, k8), 0.0)
            st_ref[...] = st
            q8 = jnp.broadcast_to(qs_ref[pl.ds(t, 1), :], (SUBLANES, GROUP_W))
            os_ref[pl.ds(t, 1), :] = _dot_nt(q8, st)[0:1]
            return carry

        lax.fori_loop(0, T, step, 0)

    if rev:
        o = os_ref[...] + part_ref[0]
        g = g_ref[0].astype(F32)
        head_mean = jnp.where(same, 1.0 / HEAD_DIM, 0.0).astype(BF16)
        o_ref[0] = (_rms_heads(o, nw_ref[...], head_mean) * _silu(g)).astype(BF16)
    else:
        o_ref[0] = os_ref[...]


def _hgrn(p3, lb_raw, norm_w, batch, seq, layer):
    C = HGRN_NB * HGRN_C
    assert C == GROUP_W
    nc = seq // C
    scratch = [pltpu.VMEM((GROUP_W, GROUP_W), F32)] + [pltpu.VMEM((C, GROUP_W), F32)] * 5

    def grp(g, rev):
        if rev:
            return pl.BlockSpec((1, C, GROUP_W), lambda b, j: (b, nc - 1 - j, g))
        return pl.BlockSpec((1, C, GROUP_W), lambda b, j: (b, j, g))

    part = pl.pallas_call(
        functools.partial(_hgrn_kernel, layer=layer, rev=False),
        grid=(batch, nc),
        in_specs=[grp(P_BQ, False), grp(P_BF_FW, False), grp(P_BI, False), _const_spec((DEPTH, GROUP_W))],
        out_specs=pl.BlockSpec((1, C, GROUP_W), lambda b, j: (b, j, 0)),
        out_shape=jax.ShapeDtypeStruct((batch, seq, GROUP_W), F32),
        scratch_shapes=scratch,
        compiler_params=_cparams(2),
        name="hgrn_fw",
    )(p3, p3, p3, lb_raw[:, 0])
    return pl.pallas_call(
        functools.partial(_hgrn_kernel, layer=layer, rev=True),
        grid=(batch, nc),
        in_specs=[grp(P_BQ, True), grp(P_BF_BW, True), grp(P_BI, True), _const_spec((DEPTH, GROUP_W)),
                  pl.BlockSpec((1, C, GROUP_W), lambda b, j: (b, nc - 1 - j, 0)),
                  grp(P_BG, True), _const_spec((1, GROUP_W))],
        out_specs=pl.BlockSpec((1, C, GROUP_W), lambda b, j: (b, nc - 1 - j, 0)),
        out_shape=jax.ShapeDtypeStruct((batch, seq, GROUP_W), BF16),
        scratch_shapes=scratch,
        compiler_params=_cparams(2),
        name="hgrn_bw",
    )(p3, p3, p3, lb_raw[:, 1], part, p3, norm_w)


def _log_sigmoid(x):
    return jnp.minimum(x, 0.0) - jnp.log(1.0 + jnp.exp(-jnp.abs(x)))


def _ret_kernel(*refs, rev):
    if rev:
        q_ref, k_ref, v_ref, dec_ref, part_ref, g_ref, nw_ref, o_ref, st_ref, xi_ref, zeta_ref = refs
    else:
        q_ref, k_ref, v_ref, dec_ref, o_ref, st_ref, xi_ref, zeta_ref, dm_ref = refs
    C = RET_C

    @pl.when(pl.program_id(1) == 0)
    def _():
        st_ref[...] = jnp.zeros_like(st_ref)

    lg_f = _log_sigmoid(dec_ref[0:1, :])
    lg_b = _log_sigmoid(dec_ref[1:2, :])
    lg = lg_b if rev else lg_f
    if not rev:
        @pl.when((pl.program_id(0) == 0) & (pl.program_id(1) == 0))
        def _():
            row = lax.broadcasted_iota(jnp.int32, (C, C), 0)
            col = lax.broadcasted_iota(jnp.int32, (C, C), 1)
            d_ts = (row - col).astype(F32)
            for h in range(N_HEADS):
                lf1 = lg_f[:, h * HEAD_DIM:h * HEAD_DIM + 1]
                lb1 = lg_b[:, h * HEAD_DIM:h * HEAD_DIM + 1]
                dm_ref[h] = (jnp.where(d_ts >= 0, jnp.exp(jnp.maximum(d_ts, 0.0) * lf1), 0.0)
                             + jnp.where(d_ts <= 0, jnp.exp(jnp.maximum(-d_ts, 0.0) * lb1), 0.0))

    @pl.when((pl.program_id(0) == 0) & (pl.program_id(1) == 0))
    def _():
        t = lax.broadcasted_iota(jnp.int32, (C, 1), 0).astype(F32)
        if rev:
            xi_ref[...] = jnp.exp((C - t) * lg)
            zeta_ref[...] = jnp.exp(t * lg)
        else:
            xi_ref[...] = jnp.exp((t + 1.0) * lg)
            zeta_ref[...] = jnp.exp((C - 1.0 - t) * lg)

    q = q_ref[0].astype(F32)
    k = k_ref[0].astype(F32) * HEAD_DIM ** -0.5
    v = v_ref[0]
    cd = jnp.exp(C * lg)
    qx = (q * xi_ref[...]).astype(BF16)
    kz = (k * zeta_ref[...]).astype(BF16)
    if not rev:
        qb = q.astype(BF16)
        kb = k.astype(BF16)
    outs = []
    for h in range(N_HEADS):
        hs = slice(h * HEAD_DIM, (h + 1) * HEAD_DIM)
        st = st_ref[h]
        o = _dot(qx[:, hs], st.astype(BF16))
        if not rev:
            sc = (_dot_nt(qb[:, hs], kb[:, hs]) * dm_ref[h]).astype(BF16)
            o = o + _dot(sc, v[:, hs])
        st_ref[h] = st * cd[:, h * HEAD_DIM:h * HEAD_DIM + 1] + _dot_tn(kz[:, hs], v[:, hs])
        outs.append(o)
    if rev:
        g = g_ref[0].astype(F32)
        row = lax.broadcasted_iota(jnp.int32, (GROUP_W, GROUP_W), 0)
        col = lax.broadcasted_iota(jnp.int32, (GROUP_W, GROUP_W), 1)
        head_mean = jnp.where(row // HEAD_DIM == col // HEAD_DIM, 1.0 / HEAD_DIM, 0.0).astype(BF16)
        o = jnp.concatenate(outs, axis=1) + part_ref[0]
        o_ref[0] = (_rms_heads(o, nw_ref[...], head_mean) * _silu(g)).astype(BF16)
    else:
        o_ref[0] = jnp.concatenate(outs, axis=1)


def _retention(p3, dec_lanes, norm_w, batch, seq):
    C = RET_C
    nc = seq // C
    scratch = [pltpu.VMEM((N_HEADS, HEAD_DIM, HEAD_DIM), F32)] + [pltpu.VMEM((C, GROUP_W), F32)] * 2

    def grp(g, rev):
        if rev:
            return pl.BlockSpec((1, C, GROUP_W), lambda b, j: (b, nc - 1 - j, g))
        return pl.BlockSpec((1, C, GROUP_W), lambda b, j: (b, j, g))

    part = pl.pallas_call(
        functools.partial(_ret_kernel, rev=False),
        grid=(batch, nc),
        in_specs=[grp(P_DQ, False), grp(P_DK, False), grp(P_DV, False), _const_spec((2, GROUP_W))],
        out_specs=pl.BlockSpec((1, C, GROUP_W), lambda b, j: (b, j, 0)),
        out_shape=jax.ShapeDtypeStruct((batch, seq, GROUP_W), F32),
        scratch_shapes=scratch + [pltpu.VMEM((N_HEADS, C, C), F32)],
        compiler_params=_cparams(2),
        name="ret_fw",
    )(p3, p3, p3, dec_lanes)
    return pl.pallas_call(
        functools.partial(_ret_kernel, rev=True),
        grid=(batch, nc),
        in_specs=[grp(P_DQ, True), grp(P_DK, True), grp(P_DV, True), _const_spec((2, GROUP_W)),
                  pl.BlockSpec((1, C, GROUP_W), lambda b, j: (b, nc - 1 - j, 0)),
                  grp(P_DG, True), _const_spec((1, GROUP_W))],
        out_specs=pl.BlockSpec((1, C, GROUP_W), lambda b, j: (b, nc - 1 - j, 0)),
        out_shape=jax.ShapeDtypeStruct((batch, seq, GROUP_W), BF16),
        scratch_shapes=scratch,
        compiler_params=_cparams(2),
        name="ret_bw",
    )(p3, p3, p3, dec_lanes, part, p3, norm_w)


def _rglru_kernel(*refs, rev, nblk):
    if rev:
        (x_ref, xp_ref, xn_ref, cw_ref, cb_ref, wg_ref, bg_ref, lam_ref, part_ref, gate_ref,
         o_ref, carry_ref, a_ref, u_ref, h_ref) = refs
    else:
        (x_ref, xp_ref, xn_ref, cw_ref, cb_ref, wg_ref, bg_ref, lam_ref,
         o_ref, carry_ref, a_ref, u_ref, h_ref) = refs
    TB = RG_TB
    j = pl.program_id(1)
    blk = (nblk - 1 - j) if rev else j

    @pl.when(j == 0)
    def _():
        carry_ref[...] = jnp.zeros_like(carry_ref)

    prev = jnp.where(blk > 0, xp_ref[0].astype(F32), 0.0)
    nxt = jnp.where(blk < nblk - 1, xn_ref[0].astype(F32), 0.0)
    xe = jnp.concatenate([prev, x_ref[0].astype(F32), nxt], axis=0)
    cw = cw_ref[...]
    xc = cb_ref[...] + sum(
        xe[RG_HALO - CONV_PAD_L + w:RG_HALO - CONV_PAD_L + w + TB] * cw[w:w + 1] for w in range(CONV_W))
    gates = _dot(xc.astype(BF16), wg_ref[...]) + bg_ref[...]
    r = _sigmoid(gates[:, :GROUP_W])
    ig = _sigmoid(gates[:, GROUP_W:])
    log_a = -RG_C * r * _softplus(-lam_ref[...])
    a = jnp.exp(log_a)
    one_m_a2 = 1.0 - jnp.exp(2.0 * log_a)
    root = jnp.where(one_m_a2 > 0.0, one_m_a2 * lax.rsqrt(one_m_a2), 0.0)
    u = root * (ig * xc)

    pos = lax.broadcasted_iota(jnp.int32, (TB, GROUP_W), 0) % SUBLANES
    for d in (1, 2, 4):
        if rev:
            keep = pos < SUBLANES - d
            shift = TB - d
        else:
            keep = pos >= d
            shift = d
        a_s = jnp.where(keep, pltpu.roll(a, shift, 0), 1.0)
        u_s = jnp.where(keep, pltpu.roll(u, shift, 0), 0.0)
        u = a * u_s + u
        a = a * a_s
    a_ref[...] = a
    u_ref[...] = u
    ntile = TB // SUBLANES

    def tile_step(i, hprev):
        ti = (ntile - 1 - i) if rev else i
        r0 = pl.multiple_of(ti * SUBLANES, SUBLANES)
        h = a_ref[pl.ds(r0, SUBLANES), :] * hprev + u_ref[pl.ds(r0, SUBLANES), :]
        h_ref[pl.ds(r0, SUBLANES), :] = h
        edge = h[0:1] if rev else h[SUBLANES - 1:SUBLANES]
        return jnp.broadcast_to(edge, (SUBLANES, GROUP_W))

    carry_ref[...] = lax.fori_loop(0, ntile, tile_step, carry_ref[...])
    if rev:
        hs = h_ref[...] + part_ref[0]
        o_ref[0] = (hs * jax.nn.gelu(gate_ref[0].astype(F32), approximate=True)).astype(BF16)
    else:
        o_ref[0] = h_ref[...]


def _rglru(p3, conv_w, conv_b, wg, bg, lam, batch, seq):
    TB = RG_TB
    nblk = seq // TB
    hpb = TB // RG_HALO
    nhalo = seq // RG_HALO
    scratch = [pltpu.VMEM((SUBLANES, GROUP_W), F32)] + [pltpu.VMEM((TB, GROUP_W), F32)] * 3

    def specs(rev):
        def blk(j):
            return (nblk - 1 - j) if rev else j
        return [
            pl.BlockSpec((1, TB, GROUP_W), lambda b, j: (b, blk(j), P_CX)),
            pl.BlockSpec((1, RG_HALO, GROUP_W), lambda b, j: (b, jnp.maximum(blk(j) * hpb - 1, 0), P_CX)),
            pl.BlockSpec((1, RG_HALO, GROUP_W),
                         lambda b, j: (b, jnp.minimum((blk(j) + 1) * hpb, nhalo - 1), P_CX)),
            _const_spec((CONV_W, GROUP_W)),
            _const_spec((1, GROUP_W)),
            _const_spec((GROUP_W, 2 * GROUP_W)),
            _const_spec((1, 2 * GROUP_W)),
            _const_spec((1, GROUP_W)),
        ]

    part = pl.pallas_call(
        functools.partial(_rglru_kernel, rev=False, nblk=nblk),
        grid=(batch, nblk),
        in_specs=specs(False),
        out_specs=pl.BlockSpec((1, TB, GROUP_W), lambda b, j: (b, j, 0)),
        out_shape=jax.ShapeDtypeStruct((batch, seq, GROUP_W), F32),
        scratch_shapes=scratch,
        compiler_params=_cparams(2),
        name="rglru_fw",
    )(p3, p3, p3, conv_w, conv_b, wg[0], bg[0], lam[0:1])
    return pl.pallas_call(
        functools.partial(_rglru_kernel, rev=True, nblk=nblk),
        grid=(batch, nblk),
        in_specs=specs(True) + [
            pl.BlockSpec((1, TB, GROUP_W), lambda b, j: (b, nblk - 1 - j, 0)),
            pl.BlockSpec((1, TB, GROUP_W), lambda b, j: (b, nblk - 1 - j, P_CG)),
        ],
        out_specs=pl.BlockSpec((1, TB, GROUP_W), lambda b, j: (b, nblk - 1 - j, 0)),
        out_shape=jax.ShapeDtypeStruct((batch, seq, GROUP_W), BF16),
        scratch_shapes=scratch,
        compiler_params=_cparams(2),
        name="rglru_bw",
    )(p3, p3, p3, conv_w, conv_b, wg[1], bg[1], lam[1:2], part, p3)


def _post_kernel(oa_ref, ob_ref, oc_ref, od_ref, r_ref, p_ref, nw_ref, wo_ref, wig_ref, wiu_ref, wfo_ref,
                 wpg_ref, wpp_ref, out_ref):
    nw = nw_ref[...]
    mixed_in = jnp.concatenate([oa_ref[...], ob_ref[...], oc_ref[...], od_ref[...]], axis=1)
    r = r_ref[...] + _rms(_dot(mixed_in, wo_ref[...]), nw[1:2])
    h2 = _rms(r, nw[2:3]).astype(BF16)
    ff = jnp.zeros_like(r)
    for c in range(N_FF_CHUNKS):
        act = _silu(_dot(h2, wig_ref[c])) * _dot(h2, wiu_ref[c])
        ff = ff + _dot(act.astype(BF16), wfo_ref[c])
    r = r + _rms(ff, nw[3:4])
    gate = _sigmoid(_dot(r.astype(BF16), wpg_ref[...]))
    out_ref[...] = r + gate * _dot(p_ref[0].astype(BF16), wpp_ref[...])


def _post(oa, ob, oc, od, r, p, layer, nw, wo, wig, wiu, wfo, wpg, wpp):
    tokens = r.shape[0]
    tm = TM_POST
    mix_spec = pl.BlockSpec((tm, GROUP_W), lambda i: (i, 0))
    return pl.pallas_call(
        _post_kernel,
        grid=(tokens // tm,),
        in_specs=[
            mix_spec, mix_spec, mix_spec, mix_spec,
            pl.BlockSpec((tm, D_MODEL), lambda i: (i, 0)),
            pl.BlockSpec((1, tm, PLE_DIM), lambda i: (layer, i, 0)),
            _const_spec((4, D_MODEL)),
            _const_spec((D_MODEL, D_MODEL)),
            _const_spec((N_FF_CHUNKS, D_MODEL, FF_CHUNK)),
            _const_spec((N_FF_CHUNKS, D_MODEL, FF_CHUNK)),
            _const_spec((N_FF_CHUNKS, FF_CHUNK, D_MODEL)),
            _const_spec((D_MODEL, D_MODEL)),
            _const_spec((PLE_DIM, D_MODEL)),
        ],
        out_specs=pl.BlockSpec((tm, D_MODEL), lambda i: (i, 0)),
        out_shape=jax.ShapeDtypeStruct((tokens, D_MODEL), F32),
        compiler_params=_cparams(1),
        name="post",
    )(oa, ob, oc, od, r, p, nw, wo, wig, wiu, wfo, wpg, wpp)


def _block_diag(w):
    out = jnp.zeros((GROUP_W, GROUP_W), w.dtype)
    for h in range(N_HEADS):
        out = out.at[h * HEAD_DIM:(h + 1) * HEAD_DIM, h * HEAD_DIM:(h + 1) * HEAD_DIM].set(w[h])
    return out


def _prep_weights(norm_w, w_in, diff_lambda, diff_subln_w, hgrn_lb_raw, hgrn_norm_w, rg_conv_w, rg_conv_b,
                  rg_w_a, rg_b_a, rg_w_x, rg_b_x, rg_lambda, ret_decay, ret_norm_w, w_out, w_ffn_in,
                  w_ffn_out, w_ple_gate, w_ple_proj):
    layers = []
    for l in range(DEPTH):
        wi = w_in[l]
        w_nat = jnp.concatenate([wi[:, GROUP_W:2 * GROUP_W], wi[:, 3 * GROUP_W:]], axis=1).astype(BF16)
        w_qv_t = jnp.concatenate([wi[:, :GROUP_W], wi[:, 2 * GROUP_W:3 * GROUP_W]], axis=1).T.astype(BF16)
        wg = [jnp.concatenate([_block_diag(rg_w_a[l, d]), _block_diag(rg_w_x[l, d])], axis=1).astype(BF16)
              for d in range(2)]
        bg = [jnp.concatenate([rg_b_a[l, d], rg_b_x[l, d]])[None, :] for d in range(2)]
        wfi = w_ffn_in[l].astype(BF16)
        layers.append(dict(
            nw=norm_w[l], nw0=norm_w[l, 0:1], w_nat=w_nat, w_qv_t=w_qv_t,
            dl=diff_lambda[l], sw=diff_subln_w[l][:, None],
            lb_raw=hgrn_lb_raw, hgrn_nw=jnp.tile(hgrn_norm_w[l], N_HEADS)[None, :],
            conv_w=rg_conv_w[l], conv_b=rg_conv_b[l][None, :], wg=wg, bg=bg, lam=rg_lambda[l],
            dec=jnp.repeat(ret_decay[l], HEAD_DIM, axis=-1), ret_nw=jnp.tile(ret_norm_w[l], N_HEADS)[None, :],
            wo=w_out[l].astype(BF16),
            wig=wfi[:, :D_FF].reshape(D_MODEL, N_FF_CHUNKS, FF_CHUNK).transpose(1, 0, 2),
            wiu=wfi[:, D_FF:].reshape(D_MODEL, N_FF_CHUNKS, FF_CHUNK).transpose(1, 0, 2),
            wfo=w_ffn_out[l].astype(BF16).reshape(N_FF_CHUNKS, FF_CHUNK, D_MODEL),
            wpg=w_ple_gate[l].astype(BF16), wpp=w_ple_proj[l].astype(BF16),
        ))
    return layers


def _trunk(x, p, layers):
    batch, seq, _ = x.shape
    tokens = batch * seq
    r = x.reshape(tokens, D_MODEL)
    p = p.reshape(DEPTH, tokens, PLE_DIM)
    for l, w in enumerate(layers):
        lambda_init = 0.8 - 0.6 * math.exp(-0.3 * l)
        pn, qt, vt = _inproj(r, w["nw0"], w["w_nat"], w["w_qv_t"], batch, seq)
        p3 = pn.reshape(batch, seq, N_PGROUPS * GROUP_W)
        oa = _attention(qt, p3, vt, w["dl"], w["sw"], batch, seq, lambda_init)
        ob = _hgrn(p3, w["lb_raw"], w["hgrn_nw"], batch, seq, l)
        oc = _rglru(p3, w["conv_w"], w["conv_b"], w["wg"], w["bg"], w["lam"], batch, seq)
        od = _retention(p3, w["dec"], w["ret_nw"], batch, seq)
        flat = lambda o: o.reshape(tokens, GROUP_W)
        r = _post(flat(oa), flat(ob), flat(oc), flat(od), r, p, l, w["nw"], w["wo"], w["wig"], w["wiu"],
                  w["wfo"], w["wpg"], w["wpp"])
    return r.reshape(batch, seq, D_MODEL)


def kernel(x_prompt, x_sample, p_prompt, p_sample, norm_w, w_in, diff_lambda, diff_subln_w, hgrn_lb_raw,
           hgrn_norm_w, rg_conv_w, rg_conv_b, rg_w_a, rg_b_a, rg_w_x, rg_b_x, rg_lambda, ret_decay,
           ret_norm_w, w_out, w_ffn_in, w_ffn_out, w_ple_gate, w_ple_proj):
    layers = _prep_weights(norm_w, w_in, diff_lambda, diff_subln_w, hgrn_lb_raw, hgrn_norm_w, rg_conv_w,
                           rg_conv_b, rg_w_a, rg_b_a, rg_w_x, rg_b_x, rg_lambda, ret_decay, ret_norm_w,
                           w_out, w_ffn_in, w_ffn_out, w_ple_gate, w_ple_proj)
    return (_trunk(x_prompt, p_prompt, layers), _trunk(x_sample, p_sample, layers))
```

```python
import functools
import math

import jax
import jax.numpy as jnp
from jax import lax
from jax.experimental import pallas as pl
from jax.experimental.pallas import tpu as pltpu

F32 = jnp.float32
BF16 = jnp.bfloat16

D_MODEL = 1024
DEPTH = 4
GROUP_W = 256
N_HEADS = 4
HEAD_DIM = 64
DIFF_HALF = 32
D_FF = 2816
FF_CHUNK = 256
N_FF_CHUNKS = D_FF // FF_CHUNK
PLE_DIM = 256
CONV_W = 4
CONV_PAD_L = 2
RG_C = 8.0
EPS = 1e-6
LOG2E = 1.4426950408889634

(P_AK, P_BQ, P_BF_FW, P_BF_BW, P_BI, P_BG, P_CX, P_CG, P_DQ, P_DK, P_DV, P_DG) = range(12)
N_PGROUPS = 12

VMEM_LIMIT_V7X = 56 * 1024 * 1024

TM_INPROJ = 512
TM_POST = 512
TQ_ATTN = 256
TK_ATTN = 256
ATTN_SKEW = 3
ATTN_SLOTS = ATTN_SKEW + 1
ATTN_UNROLL = 8
HGRN_C = 64
HGRN_NB = 4
SEQS_PER_STEP = 4
HGRN_MID = HGRN_C // 2
HGRN_SAFE_EXP = 80.0
RET_C = 256
RG_TB = 256
RG_HALO = 16
SUBLANES = 8


def _cparams(n_axes):
    return pltpu.CompilerParams(dimension_semantics=("arbitrary",) * n_axes,
                                vmem_limit_bytes=VMEM_LIMIT_V7X)


def _const_spec(shape):
    nd = len(shape)
    return pl.BlockSpec(shape, lambda *_: (0,) * nd, pipeline_mode=pl.Buffered(1))


def _rms(x, w):
    return x * lax.rsqrt(jnp.mean(x * x, axis=-1, keepdims=True) + EPS) * w


def _rms_heads(x, w, head_mean):
    sq = x * x
    hi = sq.astype(BF16)
    lo = (sq - hi.astype(F32)).astype(BF16)
    ms = _dot(hi, head_mean) + _dot(lo, head_mean)
    return x * lax.rsqrt(ms + EPS) * w


def _sigmoid(x):
    return jax.nn.sigmoid(x)


def _silu(x):
    return x * _sigmoid(x)


def _softplus(x):
    return jnp.maximum(x, 0.0) + jnp.log(1.0 + jnp.exp(-jnp.abs(x)))


def _dot(a, b):
    return jnp.dot(a, b, preferred_element_type=F32)


def _dot_nt(a, b):
    return lax.dot_general(a, b, (((1,), (1,)), ((), ())), preferred_element_type=F32)


def _dot_tn(a, b):
    return lax.dot_general(a, b, (((0,), (0,)), ((), ())), preferred_element_type=F32)


def _inproj_kernel(x_ref, nw_ref, wn_ref, wt_ref, p_ref, qt_ref, vt_ref):
    x = x_ref[...]
    y = _rms(x, nw_ref[...]).astype(BF16)
    for g in range(N_PGROUPS):
        cols = slice(g * GROUP_W, (g + 1) * GROUP_W)
        p_ref[:, cols] = _dot(y, wn_ref[:, cols]).astype(BF16)
    t = _dot_nt(wt_ref[...], y)
    qt_ref[0] = (t[:GROUP_W] * (DIFF_HALF ** -0.5 * LOG2E)).astype(BF16)
    vt_ref[0] = t[GROUP_W:].astype(BF16)


def _inproj(r, nw, w_nat, w_qv_t, batch, seq):
    tokens = batch * seq
    tm = TM_INPROJ
    nsb = seq // tm
    return pl.pallas_call(
        _inproj_kernel,
        grid=(tokens // tm,),
        in_specs=[
            pl.BlockSpec((tm, D_MODEL), lambda i: (i, 0)),
            _const_spec((1, D_MODEL)),
            _const_spec((D_MODEL, N_PGROUPS * GROUP_W)),
            _const_spec((2 * GROUP_W, D_MODEL)),
        ],
        out_specs=[
            pl.BlockSpec((tm, N_PGROUPS * GROUP_W), lambda i: (i, 0)),
            pl.BlockSpec((1, GROUP_W, tm), lambda i: (i // nsb, 0, i % nsb)),
            pl.BlockSpec((1, GROUP_W, tm), lambda i: (i // nsb, 0, i % nsb)),
        ],
        out_shape=[
            jax.ShapeDtypeStruct((tokens, N_PGROUPS * GROUP_W), BF16),
            jax.ShapeDtypeStruct((batch, GROUP_W, seq), BF16),
            jax.ShapeDtypeStruct((batch, GROUP_W, seq), BF16),
        ],
        compiler_params=_cparams(1),
        name="inproj",
    )(r, nw, w_nat, w_qv_t)


def _attn_kernel(qt_ref, k_ref, vt_ref, dl_ref, sw_ref, o_ref, w_ref, m_ref, l_ref, acc_ref, s_ref, t_ref,
                 *, seq, lambda_init):
    tq, tk = TQ_ATTN, TK_ATTN
    assert tq == tk
    q0 = pl.program_id(1) * tq
    nk = seq // tk
    n_chain = 2 * N_HEADS
    sub = lax.broadcasted_iota(jnp.int32, (2 * HEAD_DIM, tq), 0)
    for h in range(N_HEADS):
        half = h // 2
        qt_half = qt_ref[0, half * 2 * HEAD_DIM:(half + 1) * 2 * HEAD_DIM, :]
        for c in range(2):
            lo = (h % 2) * HEAD_DIM + c * DIFF_HALF
            w_ref[2 * h + c] = jnp.where((sub >= lo) & (sub < lo + DIFF_HALF), qt_half,
                                         jnp.zeros_like(qt_half))
    slopes2 = [2.0 ** (-8.0 * (h + 1) / N_HEADS) * LOG2E for h in range(N_HEADS)]

    @pl.when((pl.program_id(0) == 0) & (pl.program_id(1) == 0))
    def _():
        row = lax.broadcasted_iota(jnp.int32, (tk, tq), 0)
        col = lax.broadcasted_iota(jnp.int32, (tk, tq), 1)
        d = (row - col).astype(F32)
        for h in range(N_HEADS):
            t_ref[h, 0] = slopes2[h] * d
            t_ref[h, 1] = -slopes2[h] * jnp.abs(d)
            t_ref[h, 2] = -slopes2[h] * d

    def scores(jj, i):
        k0 = pl.multiple_of(jj * tk, tk)
        half = i // 4
        kt = k_ref[0, pl.ds(k0, tk), half * 2 * HEAD_DIM:(half + 1) * 2 * HEAD_DIM]
        s_ref[i % ATTN_SLOTS] = _dot(kt, w_ref[i])

    def consume(j, i, sel, off, online):
        h = i // 2
        k0 = pl.multiple_of(j * tk, tk)
        u = s_ref[i % ATTN_SLOTS] + t_ref[h, sel]
        vt = vt_ref[0, h * HEAD_DIM:(h + 1) * HEAD_DIM, pl.ds(k0, tk)]
        m = m_ref[i]
        if online:
            m_new = jnp.maximum(m, jnp.max(u, axis=0, keepdims=True) - off[h])
            p = jnp.exp2(u - (m_new + off[h]))
            alpha = jnp.exp2(m - m_new)
            acc_ref[i] = acc_ref[i] * alpha + _dot(vt, p.astype(BF16))
            l_ref[i] = l_ref[i] * alpha + jnp.sum(p, axis=0, keepdims=True)
            m_ref[i] = m_new
        else:
            p = jnp.exp2(u - (m + off[h]))
            acc_ref[i] = acc_ref[i] + _dot(vt, p.astype(BF16))
            l_ref[i] = l_ref[i] + jnp.sum(p, axis=0, keepdims=True)

    def key_sweep(unroll, online):
        assert nk % unroll == 0
        for i in range(ATTN_SKEW):
            scores(0, i)

        def body(jo, carry):
            qi = pl.program_id(1)
            for ju in range(unroll):
                j = jo * unroll + ju
                jn = jnp.minimum(j + 1, nk - 1)
                sel = (j >= qi).astype(jnp.int32) + (j > qi).astype(jnp.int32)
                gap = jnp.abs(q0 - j * tk).astype(F32)
                off = [slopes2[h] * gap for h in range(N_HEADS)]
                for i in range(n_chain):
                    a = i + ATTN_SKEW
                    if a < n_chain:
                        scores(j, a)
                    else:
                        scores(jn, a - n_chain)
                    consume(j, i, sel, off, online)
            return carry

        lax.fori_loop(0, nk // unroll, body, 0)

    for i in range(n_chain):
        h = i // 2
        half = i // 4
        kt = k_ref[0, pl.ds(pl.multiple_of(q0, tk), tk), half * 2 * HEAD_DIM:(half + 1) * 2 * HEAD_DIM]
        m_ref[i] = jnp.max(_dot(kt, w_ref[i]) + t_ref[h, 1], axis=0, keepdims=True)
    dl = dl_ref[...]
    lam = (jnp.exp(jnp.sum(dl[0:1] * dl[1:2], axis=1, keepdims=True))
           - jnp.exp(jnp.sum(dl[2:3] * dl[3:4], axis=1, keepdims=True)) + lambda_init)

    def finish():
        outs = []
        for h in range(N_HEADS):
            o = (acc_ref[2 * h] / l_ref[2 * h] - lam * (acc_ref[2 * h + 1] / l_ref[2 * h + 1]))
            ms = jnp.mean(o * o, axis=0, keepdims=True)
            outs.append(o * lax.rsqrt(ms + EPS) * sw_ref[...] * (1.0 - lambda_init))
        o_ref[0] = jnp.concatenate(outs, axis=0).T.astype(BF16)

    acc_ref[...] = jnp.zeros_like(acc_ref)
    l_ref[...] = jnp.zeros_like(l_ref)
    key_sweep(ATTN_UNROLL, online=False)
    poisoned = (jnp.sum(acc_ref[...] * 0.0) + jnp.sum(l_ref[...] * 0.0)) != 0.0
    finish()

    @pl.when(poisoned)
    def _():
        m_ref[...] = jnp.full(m_ref.shape, -1e30, F32)
        acc_ref[...] = jnp.zeros_like(acc_ref)
        l_ref[...] = jnp.zeros_like(l_ref)
        key_sweep(1, online=True)
        finish()


def _attention(qt, p3, vt, dl, sw, batch, seq, lambda_init):
    tq = TQ_ATTN
    n_chain = 2 * N_HEADS
    scratch = [
        pltpu.VMEM((n_chain, 2 * HEAD_DIM, tq), BF16),
        pltpu.VMEM((n_chain, 1, tq), F32),
        pltpu.VMEM((n_chain, 1, tq), F32),
        pltpu.VMEM((n_chain, HEAD_DIM, tq), F32),
        pltpu.VMEM((ATTN_SLOTS, TK_ATTN, tq), F32),
        pltpu.VMEM((N_HEADS, 3, TK_ATTN, tq), F32),
    ]
    return pl.pallas_call(
        functools.partial(_attn_kernel, seq=seq, lambda_init=lambda_init),
        grid=(batch, seq // tq),
        in_specs=[
            pl.BlockSpec((1, GROUP_W, tq), lambda b, i: (b, 0, i)),
            pl.BlockSpec((1, seq, GROUP_W), lambda b, i: (b, 0, P_AK), pipeline_mode=pl.Buffered(1)),
            pl.BlockSpec((1, GROUP_W, seq), lambda b, i: (b, 0, 0), pipeline_mode=pl.Buffered(1)),
            _const_spec((4, DIFF_HALF)),
            _const_spec((HEAD_DIM, 1)),
        ],
        out_specs=pl.BlockSpec((1, tq, GROUP_W), lambda b, i: (b, i, 0)),
        out_shape=jax.ShapeDtypeStruct((batch, seq, GROUP_W), BF16),
        scratch_shapes=scratch,
        compiler_params=_cparams(2),
        name="diff_attn",
    )(qt, p3, vt, dl, sw)


def _hgrn_kernel(*refs, layer, rev):
    if rev:
        (q_ref, f_ref, v_ref, lbraw_ref, part_ref, g_ref, nw_ref, o_ref,
         st_ref, qs_ref, ks_ref, fs_ref, vs_ref, os_ref) = refs
    else:
        (q_ref, f_ref, v_ref, lbraw_ref, o_ref,
         st_ref, qs_ref, ks_ref, fs_ref, vs_ref, os_ref) = refs
    C = HGRN_C
    T = HGRN_NB * C
    nseq = q_ref.shape[0]

    @pl.when(pl.program_id(1) == 0)
    def _():
        st_ref[...] = jnp.zeros_like(st_ref)

    raw = lbraw_ref[...]
    e = jnp.exp(raw - jnp.max(raw, axis=0, keepdims=True))
    soft = e / jnp.sum(e, axis=0, keepdims=True)
    lb = jnp.zeros((1, GROUP_W), F32)
    for i in range(1, layer + 1):
        lb = lb + soft[i:i + 1]

    row = lax.broadcasted_iota(jnp.int32, (T, T), 0)
    col = lax.broadcasted_iota(jnp.int32, (T, T), 1)
    same = (row // C) == (col // C)
    tri = same & ((row <= col) if rev else (row >= col))
    tri_b = jnp.where(tri, 1.0, 0.0).astype(BF16)
    head_mean = jnp.where(same, 1.0 / HEAD_DIM, 0.0).astype(BF16)

    def finish(s):
        if rev:
            o = os_ref[s] + part_ref[s]
            g = g_ref[s].astype(F32)
            o_ref[s] = (_rms_heads(o, nw_ref[...], head_mean) * _silu(g)).astype(BF16)
        else:
            o_ref[s] = os_ref[s]

    def gates(s):
        xf = f_ref[s].astype(F32)
        sg = _sigmoid(xf)
        fg = lb + (1.0 - lb) * sg
        kk = (1.0 - lb) * (1.0 - sg)
        return _silu(q_ref[s].astype(F32)), kk, fg, v_ref[s]

    def factorised():
        seqs = range(nseq)
        pre = []
        for s in seqs:
            q, kk, fg, v = gates(s)
            lf = jnp.log(fg)
            hi = lf.astype(BF16)
            lo = (lf - hi.astype(F32)).astype(BF16)
            cum = _dot(tri_b, hi) + _dot(tri_b, lo)
            pre.append((q, kk, v, cum))
        mid = []
        for s in seqs:
            q, kk, v, cum = pre[s]

            def chunk_rows(idx, cum=cum):
                return jnp.concatenate(
                    [jnp.broadcast_to(cum[c * C + idx:c * C + idx + 1], (C, GROUP_W)) for c in range(HGRN_NB)],
                    axis=0)

            cm = chunk_rows(HGRN_MID)
            last = chunk_rows(0 if rev else C - 1)
            worst = jnp.max(jnp.abs(cum - cm))
            qt = (q * jnp.exp(cum - cm)).astype(BF16)
            kt = (kk * jnp.exp(cm - cum)).astype(BF16)
            qi = (q * jnp.exp(cum)).astype(BF16)
            ki = (kk * jnp.exp(last - cum)).astype(BF16)
            mid.append((worst, qt, kt, qi, ki, jnp.exp(last), v))
        for h in range(N_HEADS):
            hs = slice(h * HEAD_DIM, (h + 1) * HEAD_DIM)
            for s in seqs:
                _, qt, kt, _, _, _, v = mid[s]
                a = jnp.where(tri, _dot_nt(qt[:, hs], kt[:, hs]), 0.0).astype(BF16)
                os_ref[s, :, hs] = _dot(a, v[:, hs])
        st_in = [st_ref[s] for s in seqs]
        st = list(st_in)
        for c in (reversed(range(HGRN_NB)) if rev else range(HGRN_NB)):
            rs = slice(c * C, (c + 1) * C)
            for s in seqs:
                _, _, _, qi, ki, dl, v = mid[s]
                os_ref[s, rs, :] = os_ref[s, rs, :] + _dot_nt(qi[rs], st[s].astype(BF16))
                st[s] = st[s] * dl[c * C:c * C + 1] + jnp.where(same, _dot_tn(v[rs], ki[rs]), 0.0)
        for s in seqs:
            st_ref[s] = st[s]
        return [(mid[s][0], st_in[s]) for s in seqs]

    def token_by_token(s, st_in):
        q, kk, fg, v = gates(s)
        st_ref[s] = st_in
        qs_ref[...] = q
        ks_ref[...] = kk
        fs_ref[...] = fg
        vs_ref[...] = v.astype(F32)

        def step(i, carry):
            t = (T - 1 - i) if rev else i
            k8 = jnp.broadcast_to(ks_ref[pl.ds(t, 1), :], (SUBLANES, GROUP_W))
            v8 = jnp.broadcast_to(vs_ref[pl.ds(t, 1), :], (SUBLANES, GROUP_W)) * (1.0 / SUBLANES)
            st = st_ref[s] * fs_ref[pl.ds(t, 1), :] + jnp.where(same, _dot_tn(v8, k8), 0.0)
            st_ref[s] = st
            q8 = jnp.broadcast_to(qs_ref[pl.ds(t, 1), :], (SUBLANES, GROUP_W))
            os_ref[s, pl.ds(t, 1), :] = _dot_nt(q8, st)[0:1]
            return carry

        lax.fori_loop(0, T, step, 0)

    done = factorised()
    for s in range(nseq):
        finish(s)
    for s in range(nseq):
        worst, st_in = done[s]

        @pl.when(jnp.logical_not(worst <= HGRN_SAFE_EXP))
        def _(s=s, st_in=st_in):
            token_by_token(s, st_in)
            finish(s)


def _hgrn(p3, lb_raw, norm_w, batch, seq, layer):
    C = HGRN_NB * HGRN_C
    assert C == GROUP_W
    nc = seq // C
    ns = SEQS_PER_STEP if batch % SEQS_PER_STEP == 0 else 1
    scratch = ([pltpu.VMEM((ns, GROUP_W, GROUP_W), F32)] + [pltpu.VMEM((C, GROUP_W), F32)] * 4
               + [pltpu.VMEM((ns, C, GROUP_W), F32)])

    def grp(g, rev):
        if rev:
            return pl.BlockSpec((ns, C, GROUP_W), lambda b, j: (b, nc - 1 - j, g))
        return pl.BlockSpec((ns, C, GROUP_W), lambda b, j: (b, j, g))

    part = pl.pallas_call(
        functools.partial(_hgrn_kernel, layer=layer, rev=False),
        grid=(batch // ns, nc),
        in_specs=[grp(P_BQ, False), grp(P_BF_FW, False), grp(P_BI, False), _const_spec((DEPTH, GROUP_W))],
        out_specs=pl.BlockSpec((ns, C, GROUP_W), lambda b, j: (b, j, 0)),
        out_shape=jax.ShapeDtypeStruct((batch, seq, GROUP_W), F32),
        scratch_shapes=scratch,
        compiler_params=_cparams(2),
        name="hgrn_fw",
    )(p3, p3, p3, lb_raw[:, 0])
    return pl.pallas_call(
        functools.partial(_hgrn_kernel, layer=layer, rev=True),
        grid=(batch // ns, nc),
        in_specs=[grp(P_BQ, True), grp(P_BF_BW, True), grp(P_BI, True), _const_spec((DEPTH, GROUP_W)),
                  pl.BlockSpec((ns, C, GROUP_W), lambda b, j: (b, nc - 1 - j, 0)),
                  grp(P_BG, True), _const_spec((1, GROUP_W))],
        out_specs=pl.BlockSpec((ns, C, GROUP_W), lambda b, j: (b, nc - 1 - j, 0)),
        out_shape=jax.ShapeDtypeStruct((batch, seq, GROUP_W), BF16),
        scratch_shapes=scratch,
        compiler_params=_cparams(2),
        name="hgrn_bw",
    )(p3, p3, p3, lb_raw[:, 1], part, p3, norm_w)


def _log_sigmoid(x):
    return jnp.minimum(x, 0.0) - jnp.log(1.0 + jnp.exp(-jnp.abs(x)))


def _ret_kernel(*refs, rev):
    if rev:
        q_ref, k_ref, v_ref, dec_ref, part_ref, g_ref, nw_ref, o_ref, st_ref, xi_ref, zeta_ref = refs
    else:
        q_ref, k_ref, v_ref, dec_ref, o_ref, st_ref, xi_ref, zeta_ref, dm_ref = refs
    C = RET_C

    @pl.when(pl.program_id(1) == 0)
    def _():
        st_ref[...] = jnp.zeros_like(st_ref)

    lg_f = _log_sigmoid(dec_ref[0:1, :])
    lg_b = _log_sigmoid(dec_ref[1:2, :])
    lg = lg_b if rev else lg_f
    if not rev:
        @pl.when((pl.program_id(0) == 0) & (pl.program_id(1) == 0))
        def _():
            row = lax.broadcasted_iota(jnp.int32, (C, C), 0)
            col = lax.broadcasted_iota(jnp.int32, (C, C), 1)
            d_ts = (row - col).astype(F32)
            for h in range(N_HEADS):
                lf1 = lg_f[:, h * HEAD_DIM:h * HEAD_DIM + 1]
                lb1 = lg_b[:, h * HEAD_DIM:h * HEAD_DIM + 1]
                dm_ref[h] = (jnp.where(d_ts >= 0, jnp.exp(jnp.maximum(d_ts, 0.0) * lf1), 0.0)
                             + jnp.where(d_ts <= 0, jnp.exp(jnp.maximum(-d_ts, 0.0) * lb1), 0.0))

    @pl.when((pl.program_id(0) == 0) & (pl.program_id(1) == 0))
    def _():
        t = lax.broadcasted_iota(jnp.int32, (C, 1), 0).astype(F32)
        if rev:
            xi_ref[...] = jnp.exp((C - t) * lg)
            zeta_ref[...] = jnp.exp(t * lg)
        else:
            xi_ref[...] = jnp.exp((t + 1.0) * lg)
            zeta_ref[...] = jnp.exp((C - 1.0 - t) * lg)

    seqs = range(q_ref.shape[0])
    cd = jnp.exp(C * lg)
    pre = []
    for s in seqs:
        q = q_ref[s].astype(F32)
        k = k_ref[s].astype(F32) * HEAD_DIM ** -0.5
        pre.append(((q * xi_ref[...]).astype(BF16), (k * zeta_ref[...]).astype(BF16),
                    q.astype(BF16), k.astype(BF16), v_ref[s]))
    outs = [[] for _ in seqs]
    for h in range(N_HEADS):
        hs = slice(h * HEAD_DIM, (h + 1) * HEAD_DIM)
        for s in seqs:
            qx, kz, qb, kb, v = pre[s]
            st = st_ref[s, h]
            o = _dot(qx[:, hs], st.astype(BF16))
            if not rev:
                sc = (_dot_nt(qb[:, hs], kb[:, hs]) * dm_ref[h]).astype(BF16)
                o = o + _dot(sc, v[:, hs])
            st_ref[s, h] = st * cd[:, h * HEAD_DIM:h * HEAD_DIM + 1] + _dot_tn(kz[:, hs], v[:, hs])
            outs[s].append(o)
    if rev:
        row = lax.broadcasted_iota(jnp.int32, (GROUP_W, GROUP_W), 0)
        col = lax.broadcasted_iota(jnp.int32, (GROUP_W, GROUP_W), 1)
        head_mean = jnp.where(row // HEAD_DIM == col // HEAD_DIM, 1.0 / HEAD_DIM, 0.0).astype(BF16)
    for s in seqs:
        o = jnp.concatenate(outs[s], axis=1)
        if rev:
            g = g_ref[s].astype(F32)
            o_ref[s] = (_rms_heads(o + part_ref[s], nw_ref[...], head_mean) * _silu(g)).astype(BF16)
        else:
            o_ref[s] = o


def _retention(p3, dec_lanes, norm_w, batch, seq):
    C = RET_C
    nc = seq // C
    ns = SEQS_PER_STEP if batch % SEQS_PER_STEP == 0 else 1
    scratch = [pltpu.VMEM((ns, N_HEADS, HEAD_DIM, HEAD_DIM), F32)] + [pltpu.VMEM((C, GROUP_W), F32)] * 2

    def grp(g, rev):
        if rev:
            return pl.BlockSpec((ns, C, GROUP_W), lambda b, j: (b, nc - 1 - j, g))
        return pl.BlockSpec((ns, C, GROUP_W), lambda b, j: (b, j, g))

    part = pl.pallas_call(
        functools.partial(_ret_kernel, rev=False),
        grid=(batch // ns, nc),
        in_specs=[grp(P_DQ, False), grp(P_DK, False), grp(P_DV, False), _const_spec((2, GROUP_W))],
        out_specs=pl.BlockSpec((ns, C, GROUP_W), lambda b, j: (b, j, 0)),
        out_shape=jax.ShapeDtypeStruct((batch, seq, GROUP_W), F32),
        scratch_shapes=scratch + [pltpu.VMEM((N_HEADS, C, C), F32)],
        compiler_params=_cparams(2),
        name="ret_fw",
    )(p3, p3, p3, dec_lanes)
    return pl.pallas_call(
        functools.partial(_ret_kernel, rev=True),
        grid=(batch // ns, nc),
        in_specs=[grp(P_DQ, True), grp(P_DK, True), grp(P_DV, True), _const_spec((2, GROUP_W)),
                  pl.BlockSpec((ns, C, GROUP_W), lambda b, j: (b, nc - 1 - j, 0)),
                  grp(P_DG, True), _const_spec((1, GROUP_W))],
        out_specs=pl.BlockSpec((ns, C, GROUP_W), lambda b, j: (b, nc - 1 - j, 0)),
        out_shape=jax.ShapeDtypeStruct((batch, seq, GROUP_W), BF16),
        scratch_shapes=scratch,
        compiler_params=_cparams(2),
        name="ret_bw",
    )(p3, p3, p3, dec_lanes, part, p3, norm_w)


def _rglru_kernel(*refs, rev, nblk):
    if rev:
        (x_ref, xp_ref, xn_ref, cw_ref, cb_ref, wg_ref, bg_ref, lam_ref, part_ref, gate_ref,
         o_ref, carry_ref, a_ref, u_ref, h_ref) = refs
    else:
        (x_ref, xp_ref, xn_ref, cw_ref, cb_ref, wg_ref, bg_ref, lam_ref,
         o_ref, carry_ref, a_ref, u_ref, h_ref) = refs
    TB = RG_TB
    j = pl.program_id(1)
    blk = (nblk - 1 - j) if rev else j

    @pl.when(j == 0)
    def _():
        carry_ref[...] = jnp.zeros_like(carry_ref)

    prev = jnp.where(blk > 0, xp_ref[0].astype(F32), 0.0)
    nxt = jnp.where(blk < nblk - 1, xn_ref[0].astype(F32), 0.0)
    xe = jnp.concatenate([prev, x_ref[0].astype(F32), nxt], axis=0)
    cw = cw_ref[...]
    xc = cb_ref[...] + sum(
        xe[RG_HALO - CONV_PAD_L + w:RG_HALO - CONV_PAD_L + w + TB] * cw[w:w + 1] for w in range(CONV_W))
    gates = _dot(xc.astype(BF16), wg_ref[...]) + bg_ref[...]
    r = _sigmoid(gates[:, :GROUP_W])
    ig = _sigmoid(gates[:, GROUP_W:])
    log_a = -RG_C * r * _softplus(-lam_ref[...])
    a = jnp.exp(log_a)
    one_m_a2 = 1.0 - jnp.exp(2.0 * log_a)
    root = jnp.where(one_m_a2 > 0.0, one_m_a2 * lax.rsqrt(one_m_a2), 0.0)
    u = root * (ig * xc)

    pos = lax.broadcasted_iota(jnp.int32, (TB, GROUP_W), 0) % SUBLANES
    for d in (1, 2, 4):
        if rev:
            keep = pos < SUBLANES - d
            shift = TB - d
        else:
            keep = pos >= d
            shift = d
        a_s = jnp.where(keep, pltpu.roll(a, shift, 0), 1.0)
        u_s = jnp.where(keep, pltpu.roll(u, shift, 0), 0.0)
        u = a * u_s + u
        a = a * a_s
    a_ref[...] = a
    u_ref[...] = u
    ntile = TB // SUBLANES

    def tile_step(i, hprev):
        ti = (ntile - 1 - i) if rev else i
        r0 = pl.multiple_of(ti * SUBLANES, SUBLANES)
        h = a_ref[pl.ds(r0, SUBLANES), :] * hprev + u_ref[pl.ds(r0, SUBLANES), :]
        h_ref[pl.ds(r0, SUBLANES), :] = h
        edge = h[0:1] if rev else h[SUBLANES - 1:SUBLANES]
        return jnp.broadcast_to(edge, (SUBLANES, GROUP_W))

    carry_ref[...] = lax.fori_loop(0, ntile, tile_step, carry_ref[...])
    if rev:
        hs = h_ref[...] + part_ref[0]
        o_ref[0] = (hs * jax.nn.gelu(gate_ref[0].astype(F32), approximate=True)).astype(BF16)
    else:
        o_ref[0] = h_ref[...]


def _rglru(p3, conv_w, conv_b, wg, bg, lam, batch, seq):
    TB = RG_TB
    nblk = seq // TB
    hpb = TB // RG_HALO
    nhalo = seq // RG_HALO
    scratch = [pltpu.VMEM((SUBLANES, GROUP_W), F32)] + [pltpu.VMEM((TB, GROUP_W), F32)] * 3

    def specs(rev):
        def blk(j):
            return (nblk - 1 - j) if rev else j
        return [
            pl.BlockSpec((1, TB, GROUP_W), lambda b, j: (b, blk(j), P_CX)),
            pl.BlockSpec((1, RG_HALO, GROUP_W), lambda b, j: (b, jnp.maximum(blk(j) * hpb - 1, 0), P_CX)),
            pl.BlockSpec((1, RG_HALO, GROUP_W),
                         lambda b, j: (b, jnp.minimum((blk(j) + 1) * hpb, nhalo - 1), P_CX)),
            _const_spec((CONV_W, GROUP_W)),
            _const_spec((1, GROUP_W)),
            _const_spec((GROUP_W, 2 * GROUP_W)),
            _const_spec((1, 2 * GROUP_W)),
            _const_spec((1, GROUP_W)),
        ]

    part = pl.pallas_call(
        functools.partial(_rglru_kernel, rev=False, nblk=nblk),
        grid=(batch, nblk),
        in_specs=specs(False),
        out_specs=pl.BlockSpec((1, TB, GROUP_W), lambda b, j: (b, j, 0)),
        out_shape=jax.ShapeDtypeStruct((batch, seq, GROUP_W), F32),
        scratch_shapes=scratch,
        compiler_params=_cparams(2),
        name="rglru_fw",
    )(p3, p3, p3, conv_w, conv_b, wg[0], bg[0], lam[0:1])
    return pl.pallas_call(
        functools.partial(_rglru_kernel, rev=True, nblk=nblk),
        grid=(batch, nblk),
        in_specs=specs(True) + [
            pl.BlockSpec((1, TB, GROUP_W), lambda b, j: (b, nblk - 1 - j, 0)),
            pl.BlockSpec((1, TB, GROUP_W), lambda b, j: (b, nblk - 1 - j, P_CG)),
        ],
        out_specs=pl.BlockSpec((1, TB, GROUP_W), lambda b, j: (b, nblk - 1 - j, 0)),
        out_shape=jax.ShapeDtypeStruct((batch, seq, GROUP_W), BF16),
        scratch_shapes=scratch,
        compiler_params=_cparams(2),
        name="rglru_bw",
    )(p3, p3, p3, conv_w, conv_b, wg[1], bg[1], lam[1:2], part, p3)


def _post_kernel(oa_ref, ob_ref, oc_ref, od_ref, r_ref, p_ref, nw_ref, wo_ref, wig_ref, wiu_ref, wfo_ref,
                 wpg_ref, wpp_ref, out_ref):
    nw = nw_ref[...]
    mixed_in = jnp.concatenate([oa_ref[...], ob_ref[...], oc_ref[...], od_ref[...]], axis=1)
    r = r_ref[...] + _rms(_dot(mixed_in, wo_ref[...]), nw[1:2])
    h2 = _rms(r, nw[2:3]).astype(BF16)
    ff = jnp.zeros_like(r)
    for c in range(N_FF_CHUNKS):
        act = _silu(_dot(h2, wig_ref[c])) * _dot(h2, wiu_ref[c])
        ff = ff + _dot(act.astype(BF16), wfo_ref[c])
    r = r + _rms(ff, nw[3:4])
    gate = _sigmoid(_dot(r.astype(BF16), wpg_ref[...]))
    out_ref[...] = r + gate * _dot(p_ref[0].astype(BF16), wpp_ref[...])


def _post(oa, ob, oc, od, r, p, layer, nw, wo, wig, wiu, wfo, wpg, wpp):
    tokens = r.shape[0]
    tm = TM_POST
    mix_spec = pl.BlockSpec((tm, GROUP_W), lambda i: (i, 0))
    return pl.pallas_call(
        _post_kernel,
        grid=(tokens // tm,),
        in_specs=[
            mix_spec, mix_spec, mix_spec, mix_spec,
            pl.BlockSpec((tm, D_MODEL), lambda i: (i, 0)),
            pl.BlockSpec((1, tm, PLE_DIM), lambda i: (layer, i, 0)),
            _const_spec((4, D_MODEL)),
            _const_spec((D_MODEL, D_MODEL)),
            _const_spec((N_FF_CHUNKS, D_MODEL, FF_CHUNK)),
            _const_spec((N_FF_CHUNKS, D_MODEL, FF_CHUNK)),
            _const_spec((N_FF_CHUNKS, FF_CHUNK, D_MODEL)),
            _const_spec((D_MODEL, D_MODEL)),
            _const_spec((PLE_DIM, D_MODEL)),
        ],
        out_specs=pl.BlockSpec((tm, D_MODEL), lambda i: (i, 0)),
        out_shape=jax.ShapeDtypeStruct((tokens, D_MODEL), F32),
        compiler_params=_cparams(1),
        name="post",
    )(oa, ob, oc, od, r, p, nw, wo, wig, wiu, wfo, wpg, wpp)


def _block_diag(w):
    out = jnp.zeros((GROUP_W, GROUP_W), w.dtype)
    for h in range(N_HEADS):
        out = out.at[h * HEAD_DIM:(h + 1) * HEAD_DIM, h * HEAD_DIM:(h + 1) * HEAD_DIM].set(w[h])
    return out


def _prep_weights(norm_w, w_in, diff_lambda, diff_subln_w, hgrn_lb_raw, hgrn_norm_w, rg_conv_w, rg_conv_b,
                  rg_w_a, rg_b_a, rg_w_x, rg_b_x, rg_lambda, ret_decay, ret_norm_w, w_out, w_ffn_in,
                  w_ffn_out, w_ple_gate, w_ple_proj):
    layers = []
    for l in range(DEPTH):
        wi = w_in[l]
        w_nat = jnp.concatenate([wi[:, GROUP_W:2 * GROUP_W], wi[:, 3 * GROUP_W:]], axis=1).astype(BF16)
        w_qv_t = jnp.concatenate([wi[:, :GROUP_W], wi[:, 2 * GROUP_W:3 * GROUP_W]], axis=1).T.astype(BF16)
        wg = [jnp.concatenate([_block_diag(rg_w_a[l, d]), _block_diag(rg_w_x[l, d])], axis=1).astype(BF16)
              for d in range(2)]
        bg = [jnp.concatenate([rg_b_a[l, d], rg_b_x[l, d]])[None, :] for d in range(2)]
        wfi = w_ffn_in[l].astype(BF16)
        layers.append(dict(
            nw=norm_w[l], nw0=norm_w[l, 0:1], w_nat=w_nat, w_qv_t=w_qv_t,
            dl=diff_lambda[l], sw=diff_subln_w[l][:, None],
            lb_raw=hgrn_lb_raw, hgrn_nw=jnp.tile(hgrn_norm_w[l], N_HEADS)[None, :],
            conv_w=rg_conv_w[l], conv_b=rg_conv_b[l][None, :], wg=wg, bg=bg, lam=rg_lambda[l],
            dec=jnp.repeat(ret_decay[l], HEAD_DIM, axis=-1), ret_nw=jnp.tile(ret_norm_w[l], N_HEADS)[None, :],
            wo=w_out[l].astype(BF16),
            wig=wfi[:, :D_FF].reshape(D_MODEL, N_FF_CHUNKS, FF_CHUNK).transpose(1, 0, 2),
            wiu=wfi[:, D_FF:].reshape(D_MODEL, N_FF_CHUNKS, FF_CHUNK).transpose(1, 0, 2),
            wfo=w_ffn_out[l].astype(BF16).reshape(N_FF_CHUNKS, FF_CHUNK, D_MODEL),
            wpg=w_ple_gate[l].astype(BF16), wpp=w_ple_proj[l].astype(BF16),
        ))
    return layers


def _trunk(x, p, layers):
    batch, seq, _ = x.shape
    tokens = batch * seq
    r = x.reshape(tokens, D_MODEL)
    p = p.reshape(DEPTH, tokens, PLE_DIM)
    for l, w in enumerate(layers):
        lambda_init = 0.8 - 0.6 * math.exp(-0.3 * l)
        pn, qt, vt = _inproj(r, w["nw0"], w["w_nat"], w["w_qv_t"], batch, seq)
        p3 = pn.reshape(batch, seq, N_PGROUPS * GROUP_W)
        oa = _attention(qt, p3, vt, w["dl"], w["sw"], batch, seq, lambda_init)
        ob = _hgrn(p3, w["lb_raw"], w["hgrn_nw"], batch, seq, l)
        oc = _rglru(p3, w["conv_w"], w["conv_b"], w["wg"], w["bg"], w["lam"], batch, seq)
        od = _retention(p3, w["dec"], w["ret_nw"], batch, seq)
        flat = lambda o: o.reshape(tokens, GROUP_W)
        r = _post(flat(oa), flat(ob), flat(oc), flat(od), r, p, l, w["nw"], w["wo"], w["wig"], w["wiu"],
                  w["wfo"], w["wpg"], w["wpp"])
    return r.reshape(batch, seq, D_MODEL)


def kernel(x_prompt, x_sample, p_prompt, p_sample, norm_w, w_in, diff_lambda, diff_subln_w, hgrn_lb_raw,
           hgrn_norm_w, rg_conv_w, rg_conv_b, rg_w_a, rg_b_a, rg_w_x, rg_b_x, rg_lambda, ret_decay,
           ret_norm_w, w_out, w_ffn_in, w_ffn_out, w_ple_gate, w_ple_proj):
    layers = _prep_weights(norm_w, w_in, diff_lambda, diff_subln_w, hgrn_lb_raw, hgrn_norm_w, rg_conv_w,
                           rg_conv_b, rg_w_a, rg_b_a, rg_w_x, rg_b_x, rg_lambda, ret_decay, ret_norm_w,
                           w_out, w_ffn_in, w_ffn_out, w_ple_gate, w_ple_proj)
    return (_trunk(x_prompt, p_prompt, layers), _trunk(x_sample, p_sample, layers))
```

```python
import functools
import math

import jax
import jax.numpy as jnp
from jax import lax
from jax.experimental import pallas as pl
from jax.experimental.pallas import tpu as pltpu

F32 = jnp.float32
BF16 = jnp.bfloat16

D_MODEL = 1024
DEPTH = 4
GROUP_W = 256
N_HEADS = 4
HEAD_DIM = 64
DIFF_HALF = 32
D_FF = 2816
FF_CHUNK = 256
N_FF_CHUNKS = D_FF // FF_CHUNK
PLE_DIM = 256
CONV_W = 4
CONV_PAD_L = 2
RG_C = 8.0
EPS = 1e-6
LOG2E = 1.4426950408889634

(P_AK, P_BQ, P_BF_FW, P_BF_BW, P_BI, P_BG, P_CX, P_CG, P_DQ, P_DK, P_DV, P_DG) = range(12)
N_PGROUPS = 12

VMEM_LIMIT_V7X = 56 * 1024 * 1024

TM_INPROJ = 512
TM_POST = 512
TQ_ATTN = 256
TK_ATTN = 256
ATTN_SKEW = 4
ATTN_SLOTS = 8
ATTN_UNROLL = 8
HGRN_C = 64
HGRN_NB = 4
SEQS_PER_STEP = 4
HGRN_MID = HGRN_C // 2
HGRN_SAFE_EXP = 80.0
RET_C = 256
RG_TB = 256
RG_HALO = 16
SUBLANES = 8


def _cparams(n_axes):
    return pltpu.CompilerParams(dimension_semantics=("arbitrary",) * n_axes,
                                vmem_limit_bytes=VMEM_LIMIT_V7X)


def _const_spec(shape):
    nd = len(shape)
    return pl.BlockSpec(shape, lambda *_: (0,) * nd, pipeline_mode=pl.Buffered(1))


def _rms(x, w):
    return x * lax.rsqrt(jnp.mean(x * x, axis=-1, keepdims=True) + EPS) * w


def _rms_heads(x, w, head_mean):
    sq = x * x
    hi = sq.astype(BF16)
    lo = (sq - hi.astype(F32)).astype(BF16)
    ms = _dot(hi, head_mean) + _dot(lo, head_mean)
    return x * lax.rsqrt(ms + EPS) * w


def _sigmoid(x):
    return jax.nn.sigmoid(x)


def _silu(x):
    return x * _sigmoid(x)


def _softplus(x):
    return jnp.maximum(x, 0.0) + jnp.log(1.0 + jnp.exp(-jnp.abs(x)))


def _dot(a, b):
    return jnp.dot(a, b, preferred_element_type=F32)


def _dot_nt(a, b):
    return lax.dot_general(a, b, (((1,), (1,)), ((), ())), preferred_element_type=F32)


def _dot_tn(a, b):
    return lax.dot_general(a, b, (((0,), (0,)), ((), ())), preferred_element_type=F32)


def _inproj_kernel(x_ref, nw_ref, wn_ref, wt_ref, p_ref, qt_ref, vt_ref):
    x = x_ref[...]
    y = _rms(x, nw_ref[...]).astype(BF16)
    for g in range(N_PGROUPS):
        cols = slice(g * GROUP_W, (g + 1) * GROUP_W)
        p_ref[:, cols] = _dot(y, wn_ref[:, cols]).astype(BF16)
    t = _dot_nt(wt_ref[...], y)
    qt_ref[0] = (t[:GROUP_W] * (DIFF_HALF ** -0.5 * LOG2E)).astype(BF16)
    vt_ref[0] = t[GROUP_W:].astype(BF16)


def _inproj(r, nw, w_nat, w_qv_t, batch, seq):
    tokens = batch * seq
    tm = TM_INPROJ
    nsb = seq // tm
    return pl.pallas_call(
        _inproj_kernel,
        grid=(tokens // tm,),
        in_specs=[
            pl.BlockSpec((tm, D_MODEL), lambda i: (i, 0)),
            _const_spec((1, D_MODEL)),
            _const_spec((D_MODEL, N_PGROUPS * GROUP_W)),
            _const_spec((2 * GROUP_W, D_MODEL)),
        ],
        out_specs=[
            pl.BlockSpec((tm, N_PGROUPS * GROUP_W), lambda i: (i, 0)),
            pl.BlockSpec((1, GROUP_W, tm), lambda i: (i // nsb, 0, i % nsb)),
            pl.BlockSpec((1, GROUP_W, tm), lambda i: (i // nsb, 0, i % nsb)),
        ],
        out_shape=[
            jax.ShapeDtypeStruct((tokens, N_PGROUPS * GROUP_W), BF16),
            jax.ShapeDtypeStruct((batch, GROUP_W, seq), BF16),
            jax.ShapeDtypeStruct((batch, GROUP_W, seq), BF16),
        ],
        compiler_params=_cparams(1),
        name="inproj",
    )(r, nw, w_nat, w_qv_t)


def _attn_kernel(qt_ref, k_ref, vt_ref, dl_ref, sw_ref, o_ref, w_ref, m_ref, l_ref, acc_ref, s_ref, t_ref,
                 *, seq, lambda_init):
    tq, tk = TQ_ATTN, TK_ATTN
    assert tq == tk
    q0 = pl.program_id(1) * tq
    nk = seq // tk
    n_chain = 2 * N_HEADS
    sub = lax.broadcasted_iota(jnp.int32, (2 * HEAD_DIM, tq), 0)
    for h in range(N_HEADS):
        half = h // 2
        qt_half = qt_ref[0, half * 2 * HEAD_DIM:(half + 1) * 2 * HEAD_DIM, :]
        for c in range(2):
            lo = (h % 2) * HEAD_DIM + c * DIFF_HALF
            w_ref[2 * h + c] = jnp.where((sub >= lo) & (sub < lo + DIFF_HALF), qt_half,
                                         jnp.zeros_like(qt_half))
    slopes2 = [2.0 ** (-8.0 * (h + 1) / N_HEADS) * LOG2E for h in range(N_HEADS)]

    @pl.when((pl.program_id(0) == 0) & (pl.program_id(1) == 0))
    def _():
        row = lax.broadcasted_iota(jnp.int32, (tk, tq), 0)
        col = lax.broadcasted_iota(jnp.int32, (tk, tq), 1)
        d = (row - col).astype(F32)
        for h in range(N_HEADS):
            t_ref[h, 0] = slopes2[h] * d
            t_ref[h, 1] = -slopes2[h] * jnp.abs(d)
            t_ref[h, 2] = -slopes2[h] * d

    def scores(jj, i):
        k0 = pl.multiple_of(jj * tk, tk)
        half = i // 4
        kt = k_ref[0, pl.ds(k0, tk), half * 2 * HEAD_DIM:(half + 1) * 2 * HEAD_DIM]
        s_ref[i % ATTN_SLOTS] = _dot(kt, w_ref[i])

    def consume(j, i, sel, off, online):
        h = i // 2
        k0 = pl.multiple_of(j * tk, tk)
        u = s_ref[i % ATTN_SLOTS] + t_ref[h, sel]
        vt = vt_ref[0, h * HEAD_DIM:(h + 1) * HEAD_DIM, pl.ds(k0, tk)]
        m = m_ref[i]
        if online:
            m_new = jnp.maximum(m, jnp.max(u, axis=0, keepdims=True) - off[h])
            p = jnp.exp2(u - (m_new + off[h]))
            alpha = jnp.exp2(m - m_new)
            acc_ref[i] = acc_ref[i] * alpha + _dot(vt, p.astype(BF16))
            l_ref[i] = l_ref[i] * alpha + jnp.sum(p, axis=0, keepdims=True)
            m_ref[i] = m_new
        else:
            p = jnp.exp2(u - (m + off[h]))
            acc_ref[i] = acc_ref[i] + _dot(vt, p.astype(BF16))
            l_ref[i] = l_ref[i] + jnp.sum(p, axis=0, keepdims=True)

    def key_sweep(unroll, online):
        assert nk % unroll == 0
        for i in range(ATTN_SKEW):
            scores(0, i)

        def body(jo, carry):
            qi = pl.program_id(1)
            for ju in range(unroll):
                j = jo * unroll + ju
                jn = jnp.minimum(j + 1, nk - 1)
                sel = (j >= qi).astype(jnp.int32) + (j > qi).astype(jnp.int32)
                gap = jnp.abs(q0 - j * tk).astype(F32)
                off = [slopes2[h] * gap for h in range(N_HEADS)]
                for i in range(n_chain):
                    a = i + ATTN_SKEW
                    if a < n_chain:
                        scores(j, a)
                    else:
                        scores(jn, a - n_chain)
                    consume(j, i, sel, off, online)
            return carry

        lax.fori_loop(0, nk // unroll, body, 0)

    for i in range(n_chain):
        h = i // 2
        half = i // 4
        kt = k_ref[0, pl.ds(pl.multiple_of(q0, tk), tk), half * 2 * HEAD_DIM:(half + 1) * 2 * HEAD_DIM]
        m_ref[i] = jnp.max(_dot(kt, w_ref[i]) + t_ref[h, 1], axis=0, keepdims=True)
    dl = dl_ref[...]
    lam = (jnp.exp(jnp.sum(dl[0:1] * dl[1:2], axis=1, keepdims=True))
           - jnp.exp(jnp.sum(dl[2:3] * dl[3:4], axis=1, keepdims=True)) + lambda_init)

    def finish():
        outs = []
        for h in range(N_HEADS):
            o = (acc_ref[2 * h] / l_ref[2 * h] - lam * (acc_ref[2 * h + 1] / l_ref[2 * h + 1]))
            ms = jnp.mean(o * o, axis=0, keepdims=True)
            outs.append(o * lax.rsqrt(ms + EPS) * sw_ref[...] * (1.0 - lambda_init))
        o_ref[0] = jnp.concatenate(outs, axis=0).T.astype(BF16)

    acc_ref[...] = jnp.zeros_like(acc_ref)
    l_ref[...] = jnp.zeros_like(l_ref)
    key_sweep(ATTN_UNROLL, online=False)
    poisoned = (jnp.sum(acc_ref[...] * 0.0) + jnp.sum(l_ref[...] * 0.0)) != 0.0
    finish()

    @pl.when(poisoned)
    def _():
        m_ref[...] = jnp.full(m_ref.shape, -1e30, F32)
        acc_ref[...] = jnp.zeros_like(acc_ref)
        l_ref[...] = jnp.zeros_like(l_ref)
        key_sweep(1, online=True)
        finish()


def _attention(qt, p3, vt, dl, sw, batch, seq, lambda_init):
    tq = TQ_ATTN
    n_chain = 2 * N_HEADS
    scratch = [
        pltpu.VMEM((n_chain, 2 * HEAD_DIM, tq), BF16),
        pltpu.VMEM((n_chain, 1, tq), F32),
        pltpu.VMEM((n_chain, 1, tq), F32),
        pltpu.VMEM((n_chain, HEAD_DIM, tq), F32),
        pltpu.VMEM((ATTN_SLOTS, TK_ATTN, tq), F32),
        pltpu.VMEM((N_HEADS, 3, TK_ATTN, tq), F32),
    ]
    return pl.pallas_call(
        functools.partial(_attn_kernel, seq=seq, lambda_init=lambda_init),
        grid=(batch, seq // tq),
        in_specs=[
            pl.BlockSpec((1, GROUP_W, tq), lambda b, i: (b, 0, i)),
            pl.BlockSpec((1, seq, GROUP_W), lambda b, i: (b, 0, P_AK), pipeline_mode=pl.Buffered(1)),
            pl.BlockSpec((1, GROUP_W, seq), lambda b, i: (b, 0, 0), pipeline_mode=pl.Buffered(1)),
            _const_spec((4, DIFF_HALF)),
            _const_spec((HEAD_DIM, 1)),
        ],
        out_specs=pl.BlockSpec((1, tq, GROUP_W), lambda b, i: (b, i, 0)),
        out_shape=jax.ShapeDtypeStruct((batch, seq, GROUP_W), BF16),
        scratch_shapes=scratch,
        compiler_params=_cparams(2),
        name="diff_attn",
    )(qt, p3, vt, dl, sw)


def _hgrn_kernel(*refs, layer, rev):
    if rev:
        (q_ref, f_ref, v_ref, lbraw_ref, part_ref, g_ref, nw_ref, o_ref,
         st_ref, qs_ref, ks_ref, fs_ref, vs_ref, os_ref) = refs
    else:
        (q_ref, f_ref, v_ref, lbraw_ref, o_ref,
         st_ref, qs_ref, ks_ref, fs_ref, vs_ref, os_ref) = refs
    C = HGRN_C
    T = HGRN_NB * C
    nseq = q_ref.shape[0]

    @pl.when(pl.program_id(1) == 0)
    def _():
        st_ref[...] = jnp.zeros_like(st_ref)

    raw = lbraw_ref[...]
    e = jnp.exp(raw - jnp.max(raw, axis=0, keepdims=True))
    soft = e / jnp.sum(e, axis=0, keepdims=True)
    lb = jnp.zeros((1, GROUP_W), F32)
    for i in range(1, layer + 1):
        lb = lb + soft[i:i + 1]

    row = lax.broadcasted_iota(jnp.int32, (T, T), 0)
    col = lax.broadcasted_iota(jnp.int32, (T, T), 1)
    same = (row // C) == (col // C)
    tri = same & ((row <= col) if rev else (row >= col))
    tri_b = jnp.where(tri, 1.0, 0.0).astype(BF16)
    head_mean = jnp.where(same, 1.0 / HEAD_DIM, 0.0).astype(BF16)

    def finish(s):
        if rev:
            o = os_ref[s] + part_ref[s]
            g = g_ref[s].astype(F32)
            o_ref[s] = (_rms_heads(o, nw_ref[...], head_mean) * _silu(g)).astype(BF16)
        else:
            o_ref[s] = os_ref[s]

    def gates(s):
        xf = f_ref[s].astype(F32)
        sg = _sigmoid(xf)
        fg = lb + (1.0 - lb) * sg
        kk = (1.0 - lb) * (1.0 - sg)
        return _silu(q_ref[s].astype(F32)), kk, fg, v_ref[s]

    def factorised():
        seqs = range(nseq)
        pre = []
        for s in seqs:
            q, kk, fg, v = gates(s)
            lf = jnp.log(fg)
            hi = lf.astype(BF16)
            lo = (lf - hi.astype(F32)).astype(BF16)
            cum = _dot(tri_b, hi) + _dot(tri_b, lo)
            pre.append((q, kk, v, cum))
        mid = []
        for s in seqs:
            q, kk, v, cum = pre[s]

            def chunk_rows(idx, cum=cum):
                return jnp.concatenate(
                    [jnp.broadcast_to(cum[c * C + idx:c * C + idx + 1], (C, GROUP_W)) for c in range(HGRN_NB)],
                    axis=0)

            cm = chunk_rows(HGRN_MID)
            last = chunk_rows(0 if rev else C - 1)
            worst = jnp.max(jnp.abs(cum - cm))
            qt = (q * jnp.exp(cum - cm)).astype(BF16)
            kt = (kk * jnp.exp(cm - cum)).astype(BF16)
            qi = (q * jnp.exp(cum)).astype(BF16)
            ki = (kk * jnp.exp(last - cum)).astype(BF16)
            mid.append((worst, qt, kt, qi, ki, jnp.exp(last), v))
        for h in range(N_HEADS):
            hs = slice(h * HEAD_DIM, (h + 1) * HEAD_DIM)
            for s in seqs:
                _, qt, kt, _, _, _, v = mid[s]
                a = jnp.where(tri, _dot_nt(qt[:, hs], kt[:, hs]), 0.0).astype(BF16)
                os_ref[s, :, hs] = _dot(a, v[:, hs])
        st_in = [st_ref[s] for s in seqs]
        st = list(st_in)
        for c in (reversed(range(HGRN_NB)) if rev else range(HGRN_NB)):
            rs = slice(c * C, (c + 1) * C)
            for s in seqs:
                _, _, _, qi, ki, dl, v = mid[s]
                os_ref[s, rs, :] = os_ref[s, rs, :] + _dot_nt(qi[rs], st[s].astype(BF16))
                st[s] = st[s] * dl[c * C:c * C + 1] + jnp.where(same, _dot_tn(v[rs], ki[rs]), 0.0)
        for s in seqs:
            st_ref[s] = st[s]
        return [(mid[s][0], st_in[s]) for s in seqs]

    def token_by_token(s, st_in):
        q, kk, fg, v = gates(s)
        st_ref[s] = st_in
        qs_ref[...] = q
        ks_ref[...] = kk
        fs_ref[...] = fg
        vs_ref[...] = v.astype(F32)

        def step(i, carry):
            t = (T - 1 - i) if rev else i
            k8 = jnp.broadcast_to(ks_ref[pl.ds(t, 1), :], (SUBLANES, GROUP_W))
            v8 = jnp.broadcast_to(vs_ref[pl.ds(t, 1), :], (SUBLANES, GROUP_W)) * (1.0 / SUBLANES)
            st = st_ref[s] * fs_ref[pl.ds(t, 1), :] + jnp.where(same, _dot_tn(v8, k8), 0.0)
            st_ref[s] = st
            q8 = jnp.broadcast_to(qs_ref[pl.ds(t, 1), :], (SUBLANES, GROUP_W))
            os_ref[s, pl.ds(t, 1), :] = _dot_nt(q8, st)[0:1]
            return carry

        lax.fori_loop(0, T, step, 0)

    done = factorised()
    for s in range(nseq):
        finish(s)
    for s in range(nseq):
        worst, st_in = done[s]

        @pl.when(jnp.logical_not(worst <= HGRN_SAFE_EXP))
        def _(s=s, st_in=st_in):
            token_by_token(s, st_in)
            finish(s)


def _hgrn(p3, lb_raw, norm_w, batch, seq, layer):
    C = HGRN_NB * HGRN_C
    assert C == GROUP_W
    nc = seq // C
    ns = SEQS_PER_STEP if batch % SEQS_PER_STEP == 0 else 1
    scratch = ([pltpu.VMEM((ns, GROUP_W, GROUP_W), F32)] + [pltpu.VMEM((C, GROUP_W), F32)] * 4
               + [pltpu.VMEM((ns, C, GROUP_W), F32)])

    def grp(g, rev):
        if rev:
            return pl.BlockSpec((ns, C, GROUP_W), lambda b, j: (b, nc - 1 - j, g))
        return pl.BlockSpec((ns, C, GROUP_W), lambda b, j: (b, j, g))

    part = pl.pallas_call(
        functools.partial(_hgrn_kernel, layer=layer, rev=False),
        grid=(batch // ns, nc),
        in_specs=[grp(P_BQ, False), grp(P_BF_FW, False), grp(P_BI, False), _const_spec((DEPTH, GROUP_W))],
        out_specs=pl.BlockSpec((ns, C, GROUP_W), lambda b, j: (b, j, 0)),
        out_shape=jax.ShapeDtypeStruct((batch, seq, GROUP_W), F32),
        scratch_shapes=scratch,
        compiler_params=_cparams(2),
        name="hgrn_fw",
    )(p3, p3, p3, lb_raw[:, 0])
    return pl.pallas_call(
        functools.partial(_hgrn_kernel, layer=layer, rev=True),
        grid=(batch // ns, nc),
        in_specs=[grp(P_BQ, True), grp(P_BF_BW, True), grp(P_BI, True), _const_spec((DEPTH, GROUP_W)),
                  pl.BlockSpec((ns, C, GROUP_W), lambda b, j: (b, nc - 1 - j, 0)),
                  grp(P_BG, True), _const_spec((1, GROUP_W))],
        out_specs=pl.BlockSpec((ns, C, GROUP_W), lambda b, j: (b, nc - 1 - j, 0)),
        out_shape=jax.ShapeDtypeStruct((batch, seq, GROUP_W), BF16),
        scratch_shapes=scratch,
        compiler_params=_cparams(2),
        name="hgrn_bw",
    )(p3, p3, p3, lb_raw[:, 1], part, p3, norm_w)


def _log_sigmoid(x):
    return jnp.minimum(x, 0.0) - jnp.log(1.0 + jnp.exp(-jnp.abs(x)))


def _ret_kernel(*refs, rev):
    if rev:
        q_ref, k_ref, v_ref, dec_ref, part_ref, g_ref, nw_ref, o_ref, st_ref, xi_ref, zeta_ref = refs
    else:
        q_ref, k_ref, v_ref, dec_ref, o_ref, st_ref, xi_ref, zeta_ref, dm_ref = refs
    C = RET_C

    @pl.when(pl.program_id(1) == 0)
    def _():
        st_ref[...] = jnp.zeros_like(st_ref)

    lg_f = _log_sigmoid(dec_ref[0:1, :])
    lg_b = _log_sigmoid(dec_ref[1:2, :])
    lg = lg_b if rev else lg_f
    if not rev:
        @pl.when((pl.program_id(0) == 0) & (pl.program_id(1) == 0))
        def _():
            row = lax.broadcasted_iota(jnp.int32, (C, C), 0)
            col = lax.broadcasted_iota(jnp.int32, (C, C), 1)
            d_ts = (row - col).astype(F32)
            for h in range(N_HEADS):
                lf1 = lg_f[:, h * HEAD_DIM:h * HEAD_DIM + 1]
                lb1 = lg_b[:, h * HEAD_DIM:h * HEAD_DIM + 1]
                dm_ref[h] = (jnp.where(d_ts >= 0, jnp.exp(jnp.maximum(d_ts, 0.0) * lf1), 0.0)
                             + jnp.where(d_ts <= 0, jnp.exp(jnp.maximum(-d_ts, 0.0) * lb1), 0.0))

    @pl.when((pl.program_id(0) == 0) & (pl.program_id(1) == 0))
    def _():
        t = lax.broadcasted_iota(jnp.int32, (C, 1), 0).astype(F32)
        if rev:
            xi_ref[...] = jnp.exp((C - t) * lg)
            zeta_ref[...] = jnp.exp(t * lg)
        else:
            xi_ref[...] = jnp.exp((t + 1.0) * lg)
            zeta_ref[...] = jnp.exp((C - 1.0 - t) * lg)

    seqs = range(q_ref.shape[0])
    cd = jnp.exp(C * lg)
    pre = []
    for s in seqs:
        q = q_ref[s].astype(F32)
        k = k_ref[s].astype(F32) * HEAD_DIM ** -0.5
        pre.append(((q * xi_ref[...]).astype(BF16), (k * zeta_ref[...]).astype(BF16),
                    q.astype(BF16), k.astype(BF16), v_ref[s]))
    outs = [[] for _ in seqs]
    for h in range(N_HEADS):
        hs = slice(h * HEAD_DIM, (h + 1) * HEAD_DIM)
        for s in seqs:
            qx, kz, qb, kb, v = pre[s]
            st = st_ref[s, h]
            o = _dot(qx[:, hs], st.astype(BF16))
            if not rev:
                sc = (_dot_nt(qb[:, hs], kb[:, hs]) * dm_ref[h]).astype(BF16)
                o = o + _dot(sc, v[:, hs])
            st_ref[s, h] = st * cd[:, h * HEAD_DIM:h * HEAD_DIM + 1] + _dot_tn(kz[:, hs], v[:, hs])
            outs[s].append(o)
    if rev:
        row = lax.broadcasted_iota(jnp.int32, (GROUP_W, GROUP_W), 0)
        col = lax.broadcasted_iota(jnp.int32, (GROUP_W, GROUP_W), 1)
        head_mean = jnp.where(row // HEAD_DIM == col // HEAD_DIM, 1.0 / HEAD_DIM, 0.0).astype(BF16)
    for s in seqs:
        o = jnp.concatenate(outs[s], axis=1)
        if rev:
            g = g_ref[s].astype(F32)
            o_ref[s] = (_rms_heads(o + part_ref[s], nw_ref[...], head_mean) * _silu(g)).astype(BF16)
        else:
            o_ref[s] = o


def _retention(p3, dec_lanes, norm_w, batch, seq):
    C = RET_C
    nc = seq // C
    ns = SEQS_PER_STEP if batch % SEQS_PER_STEP == 0 else 1
    scratch = [pltpu.VMEM((ns, N_HEADS, HEAD_DIM, HEAD_DIM), F32)] + [pltpu.VMEM((C, GROUP_W), F32)] * 2

    def grp(g, rev):
        if rev:
            return pl.BlockSpec((ns, C, GROUP_W), lambda b, j: (b, nc - 1 - j, g))
        return pl.BlockSpec((ns, C, GROUP_W), lambda b, j: (b, j, g))

    part = pl.pallas_call(
        functools.partial(_ret_kernel, rev=False),
        grid=(batch // ns, nc),
        in_specs=[grp(P_DQ, False), grp(P_DK, False), grp(P_DV, False), _const_spec((2, GROUP_W))],
        out_specs=pl.BlockSpec((ns, C, GROUP_W), lambda b, j: (b, j, 0)),
        out_shape=jax.ShapeDtypeStruct((batch, seq, GROUP_W), F32),
        scratch_shapes=scratch + [pltpu.VMEM((N_HEADS, C, C), F32)],
        compiler_params=_cparams(2),
        name="ret_fw",
    )(p3, p3, p3, dec_lanes)
    return pl.pallas_call(
        functools.partial(_ret_kernel, rev=True),
        grid=(batch // ns, nc),
        in_specs=[grp(P_DQ, True), grp(P_DK, True), grp(P_DV, True), _const_spec((2, GROUP_W)),
                  pl.BlockSpec((ns, C, GROUP_W), lambda b, j: (b, nc - 1 - j, 0)),
                  grp(P_DG, True), _const_spec((1, GROUP_W))],
        out_specs=pl.BlockSpec((ns, C, GROUP_W), lambda b, j: (b, nc - 1 - j, 0)),
        out_shape=jax.ShapeDtypeStruct((batch, seq, GROUP_W), BF16),
        scratch_shapes=scratch,
        compiler_params=_cparams(2),
        name="ret_bw",
    )(p3, p3, p3, dec_lanes, part, p3, norm_w)


def _rglru_kernel(*refs, rev, nblk):
    if rev:
        (x_ref, xp_ref, xn_ref, cw_ref, cb_ref, wg_ref, bg_ref, lam_ref, part_ref, gate_ref,
         o_ref, carry_ref, a_ref, u_ref, h_ref) = refs
    else:
        (x_ref, xp_ref, xn_ref, cw_ref, cb_ref, wg_ref, bg_ref, lam_ref,
         o_ref, carry_ref, a_ref, u_ref, h_ref) = refs
    TB = RG_TB
    j = pl.program_id(1)
    blk = (nblk - 1 - j) if rev else j

    @pl.when(j == 0)
    def _():
        carry_ref[...] = jnp.zeros_like(carry_ref)

    prev = jnp.where(blk > 0, xp_ref[0].astype(F32), 0.0)
    nxt = jnp.where(blk < nblk - 1, xn_ref[0].astype(F32), 0.0)
    xe = jnp.concatenate([prev, x_ref[0].astype(F32), nxt], axis=0)
    cw = cw_ref[...]
    xc = cb_ref[...] + sum(
        xe[RG_HALO - CONV_PAD_L + w:RG_HALO - CONV_PAD_L + w + TB] * cw[w:w + 1] for w in range(CONV_W))
    gates = _dot(xc.astype(BF16), wg_ref[...]) + bg_ref[...]
    r = _sigmoid(gates[:, :GROUP_W])
    ig = _sigmoid(gates[:, GROUP_W:])
    log_a = -RG_C * r * _softplus(-lam_ref[...])
    a = jnp.exp(log_a)
    one_m_a2 = 1.0 - jnp.exp(2.0 * log_a)
    root = jnp.where(one_m_a2 > 0.0, one_m_a2 * lax.rsqrt(one_m_a2), 0.0)
    u = root * (ig * xc)

    pos = lax.broadcasted_iota(jnp.int32, (TB, GROUP_W), 0) % SUBLANES
    for d in (1, 2, 4):
        if rev:
            keep = pos < SUBLANES - d
            shift = TB - d
        else:
            keep = pos >= d
            shift = d
        a_s = jnp.where(keep, pltpu.roll(a, shift, 0), 1.0)
        u_s = jnp.where(keep, pltpu.roll(u, shift, 0), 0.0)
        u = a * u_s + u
        a = a * a_s
    a_ref[...] = a
    u_ref[...] = u
    ntile = TB // SUBLANES

    def tile_step(i, hprev):
        ti = (ntile - 1 - i) if rev else i
        r0 = pl.multiple_of(ti * SUBLANES, SUBLANES)
        h = a_ref[pl.ds(r0, SUBLANES), :] * hprev + u_ref[pl.ds(r0, SUBLANES), :]
        h_ref[pl.ds(r0, SUBLANES), :] = h
        edge = h[0:1] if rev else h[SUBLANES - 1:SUBLANES]
        return jnp.broadcast_to(edge, (SUBLANES, GROUP_W))

    carry_ref[...] = lax.fori_loop(0, ntile, tile_step, carry_ref[...])
    if rev:
        hs = h_ref[...] + part_ref[0]
        o_ref[0] = (hs * jax.nn.gelu(gate_ref[0].astype(F32), approximate=True)).astype(BF16)
    else:
        o_ref[0] = h_ref[...]


def _rglru(p3, conv_w, conv_b, wg, bg, lam, batch, seq):
    TB = RG_TB
    nblk = seq // TB
    hpb = TB // RG_HALO
    nhalo = seq // RG_HALO
    scratch = [pltpu.VMEM((SUBLANES, GROUP_W), F32)] + [pltpu.VMEM((TB, GROUP_W), F32)] * 3

    def specs(rev):
        def blk(j):
            return (nblk - 1 - j) if rev else j
        return [
            pl.BlockSpec((1, TB, GROUP_W), lambda b, j: (b, blk(j), P_CX)),
            pl.BlockSpec((1, RG_HALO, GROUP_W), lambda b, j: (b, jnp.maximum(blk(j) * hpb - 1, 0), P_CX)),
            pl.BlockSpec((1, RG_HALO, GROUP_W),
                         lambda b, j: (b, jnp.minimum((blk(j) + 1) * hpb, nhalo - 1), P_CX)),
            _const_spec((CONV_W, GROUP_W)),
            _const_spec((1, GROUP_W)),
            _const_spec((GROUP_W, 2 * GROUP_W)),
            _const_spec((1, 2 * GROUP_W)),
            _const_spec((1, GROUP_W)),
        ]

    part = pl.pallas_call(
        functools.partial(_rglru_kernel, rev=False, nblk=nblk),
        grid=(batch, nblk),
        in_specs=specs(False),
        out_specs=pl.BlockSpec((1, TB, GROUP_W), lambda b, j: (b, j, 0)),
        out_shape=jax.ShapeDtypeStruct((batch, seq, GROUP_W), F32),
        scratch_shapes=scratch,
        compiler_params=_cparams(2),
        name="rglru_fw",
    )(p3, p3, p3, conv_w, conv_b, wg[0], bg[0], lam[0:1])
    return pl.pallas_call(
        functools.partial(_rglru_kernel, rev=True, nblk=nblk),
        grid=(batch, nblk),
        in_specs=specs(True) + [
            pl.BlockSpec((1, TB, GROUP_W), lambda b, j: (b, nblk - 1 - j, 0)),
            pl.BlockSpec((1, TB, GROUP_W), lambda b, j: (b, nblk - 1 - j, P_CG)),
        ],
        out_specs=pl.BlockSpec((1, TB, GROUP_W), lambda b, j: (b, nblk - 1 - j, 0)),
        out_shape=jax.ShapeDtypeStruct((batch, seq, GROUP_W), BF16),
        scratch_shapes=scratch,
        compiler_params=_cparams(2),
        name="rglru_bw",
    )(p3, p3, p3, conv_w, conv_b, wg[1], bg[1], lam[1:2], part, p3)


def _post_kernel(oa_ref, ob_ref, oc_ref, od_ref, r_ref, p_ref, nw_ref, wo_ref, wig_ref, wiu_ref, wfo_ref,
                 wpg_ref, wpp_ref, out_ref):
    nw = nw_ref[...]
    mixed_in = jnp.concatenate([oa_ref[...], ob_ref[...], oc_ref[...], od_ref[...]], axis=1)
    r = r_ref[...] + _rms(_dot(mixed_in, wo_ref[...]), nw[1:2])
    h2 = _rms(r, nw[2:3]).astype(BF16)
    ff = jnp.zeros_like(r)
    for c in range(N_FF_CHUNKS):
        act = _silu(_dot(h2, wig_ref[c])) * _dot(h2, wiu_ref[c])
        ff = ff + _dot(act.astype(BF16), wfo_ref[c])
    r = r + _rms(ff, nw[3:4])
    gate = _sigmoid(_dot(r.astype(BF16), wpg_ref[...]))
    out_ref[...] = r + gate * _dot(p_ref[0].astype(BF16), wpp_ref[...])


def _post(oa, ob, oc, od, r, p, layer, nw, wo, wig, wiu, wfo, wpg, wpp):
    tokens = r.shape[0]
    tm = TM_POST
    mix_spec = pl.BlockSpec((tm, GROUP_W), lambda i: (i, 0))
    return pl.pallas_call(
        _post_kernel,
        grid=(tokens // tm,),
        in_specs=[
            mix_spec, mix_spec, mix_spec, mix_spec,
            pl.BlockSpec((tm, D_MODEL), lambda i: (i, 0)),
            pl.BlockSpec((1, tm, PLE_DIM), lambda i: (layer, i, 0)),
            _const_spec((4, D_MODEL)),
            _const_spec((D_MODEL, D_MODEL)),
            _const_spec((N_FF_CHUNKS, D_MODEL, FF_CHUNK)),
            _const_spec((N_FF_CHUNKS, D_MODEL, FF_CHUNK)),
            _const_spec((N_FF_CHUNKS, FF_CHUNK, D_MODEL)),
            _const_spec((D_MODEL, D_MODEL)),
            _const_spec((PLE_DIM, D_MODEL)),
        ],
        out_specs=pl.BlockSpec((tm, D_MODEL), lambda i: (i, 0)),
        out_shape=jax.ShapeDtypeStruct((tokens, D_MODEL), F32),
        compiler_params=_cparams(1),
        name="post",
    )(oa, ob, oc, od, r, p, nw, wo, wig, wiu, wfo, wpg, wpp)


def _block_diag(w):
    out = jnp.zeros((GROUP_W, GROUP_W), w.dtype)
    for h in range(N_HEADS):
        out = out.at[h * HEAD_DIM:(h + 1) * HEAD_DIM, h * HEAD_DIM:(h + 1) * HEAD_DIM].set(w[h])
    return out


def _prep_weights(norm_w, w_in, diff_lambda, diff_subln_w, hgrn_lb_raw, hgrn_norm_w, rg_conv_w, rg_conv_b,
                  rg_w_a, rg_b_a, rg_w_x, rg_b_x, rg_lambda, ret_decay, ret_norm_w, w_out, w_ffn_in,
                  w_ffn_out, w_ple_gate, w_ple_proj):
    layers = []
    for l in range(DEPTH):
        wi = w_in[l]
        w_nat = jnp.concatenate([wi[:, GROUP_W:2 * GROUP_W], wi[:, 3 * GROUP_W:]], axis=1).astype(BF16)
        w_qv_t = jnp.concatenate([wi[:, :GROUP_W], wi[:, 2 * GROUP_W:3 * GROUP_W]], axis=1).T.astype(BF16)
        wg = [jnp.concatenate([_block_diag(rg_w_a[l, d]), _block_diag(rg_w_x[l, d])], axis=1).astype(BF16)
              for d in range(2)]
        bg = [jnp.concatenate([rg_b_a[l, d], rg_b_x[l, d]])[None, :] for d in range(2)]
        wfi = w_ffn_in[l].astype(BF16)
        layers.append(dict(
            nw=norm_w[l], nw0=norm_w[l, 0:1], w_nat=w_nat, w_qv_t=w_qv_t,
            dl=diff_lambda[l], sw=diff_subln_w[l][:, None],
            lb_raw=hgrn_lb_raw, hgrn_nw=jnp.tile(hgrn_norm_w[l], N_HEADS)[None, :],
            conv_w=rg_conv_w[l], conv_b=rg_conv_b[l][None, :], wg=wg, bg=bg, lam=rg_lambda[l],
            dec=jnp.repeat(ret_decay[l], HEAD_DIM, axis=-1), ret_nw=jnp.tile(ret_norm_w[l], N_HEADS)[None, :],
            wo=w_out[l].astype(BF16),
            wig=wfi[:, :D_FF].reshape(D_MODEL, N_FF_CHUNKS, FF_CHUNK).transpose(1, 0, 2),
            wiu=wfi[:, D_FF:].reshape(D_MODEL, N_FF_CHUNKS, FF_CHUNK).transpose(1, 0, 2),
            wfo=w_ffn_out[l].astype(BF16).reshape(N_FF_CHUNKS, FF_CHUNK, D_MODEL),
            wpg=w_ple_gate[l].astype(BF16), wpp=w_ple_proj[l].astype(BF16),
        ))
    return layers


def _trunk(x, p, layers):
    batch, seq, _ = x.shape
    tokens = batch * seq
    r = x.reshape(tokens, D_MODEL)
    p = p.reshape(DEPTH, tokens, PLE_DIM)
    for l, w in enumerate(layers):
        lambda_init = 0.8 - 0.6 * math.exp(-0.3 * l)
        pn, qt, vt = _inproj(r, w["nw0"], w["w_nat"], w["w_qv_t"], batch, seq)
        p3 = pn.reshape(batch, seq, N_PGROUPS * GROUP_W)
        oa = _attention(qt, p3, vt, w["dl"], w["sw"], batch, seq, lambda_init)
        ob = _hgrn(p3, w["lb_raw"], w["hgrn_nw"], batch, seq, l)
        oc = _rglru(p3, w["conv_w"], w["conv_b"], w["wg"], w["bg"], w["lam"], batch, seq)
        od = _retention(p3, w["dec"], w["ret_nw"], batch, seq)
        flat = lambda o: o.reshape(tokens, GROUP_W)
        r = _post(flat(oa), flat(ob), flat(oc), flat(od), r, p, l, w["nw"], w["wo"], w["wig"], w["wiu"],
                  w["wfo"], w["wpg"], w["wpp"])
    return r.reshape(batch, seq, D_MODEL)


def kernel(x_prompt, x_sample, p_prompt, p_sample, norm_w, w_in, diff_lambda, diff_subln_w, hgrn_lb_raw,
           hgrn_norm_w, rg_conv_w, rg_conv_b, rg_w_a, rg_b_a, rg_w_x, rg_b_x, rg_lambda, ret_decay,
           ret_norm_w, w_out, w_ffn_in, w_ffn_out, w_ple_gate, w_ple_proj):
    layers = _prep_weights(norm_w, w_in, diff_lambda, diff_subln_w, hgrn_lb_raw, hgrn_norm_w, rg_conv_w,
                           rg_conv_b, rg_w_a, rg_b_a, rg_w_x, rg_b_x, rg_lambda, ret_decay, ret_norm_w,
                           w_out, w_ffn_in, w_ffn_out, w_ple_gate, w_ple_proj)
    return (_trunk(x_prompt, p_prompt, layers), _trunk(x_sample, p_sample, layers))
```

```python
import functools
import math

import jax
import jax.numpy as jnp
from jax import lax
from jax.experimental import pallas as pl
from jax.experimental.pallas import tpu as pltpu

F32 = jnp.float32
BF16 = jnp.bfloat16

D_MODEL = 1024
DEPTH = 4
GROUP_W = 256
N_HEADS = 4
HEAD_DIM = 64
DIFF_HALF = 32
D_FF = 2816
FF_CHUNK = 256
N_FF_CHUNKS = D_FF // FF_CHUNK
PLE_DIM = 256
CONV_W = 4
CONV_PAD_L = 2
RG_C = 8.0
EPS = 1e-6
LOG2E = 1.4426950408889634

(P_AK, P_BQ, P_BF_FW, P_BF_BW, P_BI, P_BG, P_CX, P_CG, P_DQ, P_DK, P_DV, P_DG) = range(12)
N_PGROUPS = 12

VMEM_LIMIT_V7X = 56 * 1024 * 1024

TM_INPROJ = 512
TM_POST = 512
TQ_ATTN = 256
TK_ATTN = 256
ATTN_SKEW = 4
ATTN_SLOTS = 8
ATTN_UNROLL = 16
HGRN_C = 64
HGRN_NB = 4
SEQS_PER_STEP = 4
HGRN_MID = HGRN_C // 2
HGRN_SAFE_EXP = 80.0
RET_C = 256
RG_TB = 256
RG_HALO = 16
SUBLANES = 8


def _cparams(n_axes):
    return pltpu.CompilerParams(dimension_semantics=("arbitrary",) * n_axes,
                                vmem_limit_bytes=VMEM_LIMIT_V7X)


def _const_spec(shape):
    nd = len(shape)
    return pl.BlockSpec(shape, lambda *_: (0,) * nd, pipeline_mode=pl.Buffered(1))


def _rms(x, w):
    return x * lax.rsqrt(jnp.mean(x * x, axis=-1, keepdims=True) + EPS) * w


def _rms_heads(x, w, head_mean):
    sq = x * x
    hi = sq.astype(BF16)
    lo = (sq - hi.astype(F32)).astype(BF16)
    ms = _dot(hi, head_mean) + _dot(lo, head_mean)
    return x * lax.rsqrt(ms + EPS) * w


def _sigmoid(x):
    return jax.nn.sigmoid(x)


def _silu(x):
    return x * _sigmoid(x)


def _softplus(x):
    return jnp.maximum(x, 0.0) + jnp.log(1.0 + jnp.exp(-jnp.abs(x)))


def _dot(a, b):
    return jnp.dot(a, b, preferred_element_type=F32)


def _dot_nt(a, b):
    return lax.dot_general(a, b, (((1,), (1,)), ((), ())), preferred_element_type=F32)


def _dot_tn(a, b):
    return lax.dot_general(a, b, (((0,), (0,)), ((), ())), preferred_element_type=F32)


def _inproj_kernel(x_ref, nw_ref, wn_ref, wt_ref, p_ref, qt_ref, vt_ref):
    x = x_ref[...]
    y = _rms(x, nw_ref[...]).astype(BF16)
    for g in range(N_PGROUPS):
        cols = slice(g * GROUP_W, (g + 1) * GROUP_W)
        p_ref[:, cols] = _dot(y, wn_ref[:, cols]).astype(BF16)
    t = _dot_nt(wt_ref[...], y)
    qt_ref[0] = (t[:GROUP_W] * (DIFF_HALF ** -0.5 * LOG2E)).astype(BF16)
    vt_ref[0] = t[GROUP_W:].astype(BF16)


def _inproj(r, nw, w_nat, w_qv_t, batch, seq):
    tokens = batch * seq
    tm = TM_INPROJ
    nsb = seq // tm
    return pl.pallas_call(
        _inproj_kernel,
        grid=(tokens // tm,),
        in_specs=[
            pl.BlockSpec((tm, D_MODEL), lambda i: (i, 0)),
            _const_spec((1, D_MODEL)),
            _const_spec((D_MODEL, N_PGROUPS * GROUP_W)),
            _const_spec((2 * GROUP_W, D_MODEL)),
        ],
        out_specs=[
            pl.BlockSpec((tm, N_PGROUPS * GROUP_W), lambda i: (i, 0)),
            pl.BlockSpec((1, GROUP_W, tm), lambda i: (i // nsb, 0, i % nsb)),
            pl.BlockSpec((1, GROUP_W, tm), lambda i: (i // nsb, 0, i % nsb)),
        ],
        out_shape=[
            jax.ShapeDtypeStruct((tokens, N_PGROUPS * GROUP_W), BF16),
            jax.ShapeDtypeStruct((batch, GROUP_W, seq), BF16),
            jax.ShapeDtypeStruct((batch, GROUP_W, seq), BF16),
        ],
        compiler_params=_cparams(1),
        name="inproj",
    )(r, nw, w_nat, w_qv_t)


def _attn_kernel(qt_ref, k_ref, vt_ref, dl_ref, sw_ref, o_ref, w_ref, m_ref, l_ref, acc_ref, s_ref, t_ref,
                 *, seq, lambda_init):
    tq, tk = TQ_ATTN, TK_ATTN
    assert tq == tk
    q0 = pl.program_id(1) * tq
    nk = seq // tk
    n_chain = 2 * N_HEADS
    sub = lax.broadcasted_iota(jnp.int32, (2 * HEAD_DIM, tq), 0)
    for h in range(N_HEADS):
        half = h // 2
        qt_half = qt_ref[0, half * 2 * HEAD_DIM:(half + 1) * 2 * HEAD_DIM, :]
        for c in range(2):
            lo = (h % 2) * HEAD_DIM + c * DIFF_HALF
            w_ref[2 * h + c] = jnp.where((sub >= lo) & (sub < lo + DIFF_HALF), qt_half,
                                         jnp.zeros_like(qt_half))
    slopes2 = [2.0 ** (-8.0 * (h + 1) / N_HEADS) * LOG2E for h in range(N_HEADS)]

    @pl.when((pl.program_id(0) == 0) & (pl.program_id(1) == 0))
    def _():
        row = lax.broadcasted_iota(jnp.int32, (tk, tq), 0)
        col = lax.broadcasted_iota(jnp.int32, (tk, tq), 1)
        d = (row - col).astype(F32)
        for h in range(N_HEADS):
            t_ref[h, 0] = slopes2[h] * d
            t_ref[h, 1] = -slopes2[h] * jnp.abs(d)
            t_ref[h, 2] = -slopes2[h] * d

    def scores(jj, i):
        k0 = pl.multiple_of(jj * tk, tk)
        half = i // 4
        kt = k_ref[0, pl.ds(k0, tk), half * 2 * HEAD_DIM:(half + 1) * 2 * HEAD_DIM]
        s_ref[i % ATTN_SLOTS] = _dot(kt, w_ref[i])

    def consume(j, i, sel, off, online):
        h = i // 2
        k0 = pl.multiple_of(j * tk, tk)
        u = s_ref[i % ATTN_SLOTS] + t_ref[h, sel]
        vt = vt_ref[0, h * HEAD_DIM:(h + 1) * HEAD_DIM, pl.ds(k0, tk)]
        m = m_ref[i]
        if online:
            m_new = jnp.maximum(m, jnp.max(u, axis=0, keepdims=True) - off[h])
            p = jnp.exp2(u - (m_new + off[h]))
            alpha = jnp.exp2(m - m_new)
            acc_ref[i] = acc_ref[i] * alpha + _dot(vt, p.astype(BF16))
            l_ref[i] = l_ref[i] * alpha + jnp.sum(p, axis=0, keepdims=True)
            m_ref[i] = m_new
        else:
            p = jnp.exp2(u - (m + off[h]))
            acc_ref[i] = acc_ref[i] + _dot(vt, p.astype(BF16))
            l_ref[i] = l_ref[i] + jnp.sum(p, axis=0, keepdims=True)

    def key_sweep(unroll, online):
        assert nk % unroll == 0
        for i in range(ATTN_SKEW):
            scores(0, i)

        def body(jo, carry):
            qi = pl.program_id(1)
            for ju in range(unroll):
                j = jo * unroll + ju
                jn = jnp.minimum(j + 1, nk - 1)
                sel = (j >= qi).astype(jnp.int32) + (j > qi).astype(jnp.int32)
                gap = jnp.abs(q0 - j * tk).astype(F32)
                off = [slopes2[h] * gap for h in range(N_HEADS)]
                for i in range(n_chain):
                    a = i + ATTN_SKEW
                    if a < n_chain:
                        scores(j, a)
                    else:
                        scores(jn, a - n_chain)
                    consume(j, i, sel, off, online)
            return carry

        lax.fori_loop(0, nk // unroll, body, 0)

    for i in range(n_chain):
        h = i // 2
        half = i // 4
        kt = k_ref[0, pl.ds(pl.multiple_of(q0, tk), tk), half * 2 * HEAD_DIM:(half + 1) * 2 * HEAD_DIM]
        m_ref[i] = jnp.max(_dot(kt, w_ref[i]) + t_ref[h, 1], axis=0, keepdims=True)
    dl = dl_ref[...]
    lam = (jnp.exp(jnp.sum(dl[0:1] * dl[1:2], axis=1, keepdims=True))
           - jnp.exp(jnp.sum(dl[2:3] * dl[3:4], axis=1, keepdims=True)) + lambda_init)

    def finish():
        outs = []
        for h in range(N_HEADS):
            o = (acc_ref[2 * h] / l_ref[2 * h] - lam * (acc_ref[2 * h + 1] / l_ref[2 * h + 1]))
            ms = jnp.mean(o * o, axis=0, keepdims=True)
            outs.append(o * lax.rsqrt(ms + EPS) * sw_ref[...] * (1.0 - lambda_init))
        o_ref[0] = jnp.concatenate(outs, axis=0).T.astype(BF16)

    acc_ref[...] = jnp.zeros_like(acc_ref)
    l_ref[...] = jnp.zeros_like(l_ref)
    key_sweep(ATTN_UNROLL, online=False)
    poisoned = (jnp.sum(acc_ref[...] * 0.0) + jnp.sum(l_ref[...] * 0.0)) != 0.0
    finish()

    @pl.when(poisoned)
    def _():
        m_ref[...] = jnp.full(m_ref.shape, -1e30, F32)
        acc_ref[...] = jnp.zeros_like(acc_ref)
        l_ref[...] = jnp.zeros_like(l_ref)
        key_sweep(1, online=True)
        finish()


def _attention(qt, p3, vt, dl, sw, batch, seq, lambda_init):
    tq = TQ_ATTN
    n_chain = 2 * N_HEADS
    scratch = [
        pltpu.VMEM((n_chain, 2 * HEAD_DIM, tq), BF16),
        pltpu.VMEM((n_chain, 1, tq), F32),
        pltpu.VMEM((n_chain, 1, tq), F32),
        pltpu.VMEM((n_chain, HEAD_DIM, tq), F32),
        pltpu.VMEM((ATTN_SLOTS, TK_ATTN, tq), F32),
        pltpu.VMEM((N_HEADS, 3, TK_ATTN, tq), F32),
    ]
    return pl.pallas_call(
        functools.partial(_attn_kernel, seq=seq, lambda_init=lambda_init),
        grid=(batch, seq // tq),
        in_specs=[
            pl.BlockSpec((1, GROUP_W, tq), lambda b, i: (b, 0, i)),
            pl.BlockSpec((1, seq, GROUP_W), lambda b, i: (b, 0, P_AK), pipeline_mode=pl.Buffered(1)),
            pl.BlockSpec((1, GROUP_W, seq), lambda b, i: (b, 0, 0), pipeline_mode=pl.Buffered(1)),
            _const_spec((4, DIFF_HALF)),
            _const_spec((HEAD_DIM, 1)),
        ],
        out_specs=pl.BlockSpec((1, tq, GROUP_W), lambda b, i: (b, i, 0)),
        out_shape=jax.ShapeDtypeStruct((batch, seq, GROUP_W), BF16),
        scratch_shapes=scratch,
        compiler_params=_cparams(2),
        name="diff_attn",
    )(qt, p3, vt, dl, sw)


def _hgrn_kernel(*refs, layer, rev):
    if rev:
        (q_ref, f_ref, v_ref, lbraw_ref, part_ref, g_ref, nw_ref, o_ref,
         st_ref, qs_ref, ks_ref, fs_ref, vs_ref, os_ref) = refs
    else:
        (q_ref, f_ref, v_ref, lbraw_ref, o_ref,
         st_ref, qs_ref, ks_ref, fs_ref, vs_ref, os_ref) = refs
    C = HGRN_C
    T = HGRN_NB * C
    nseq = q_ref.shape[0]

    @pl.when(pl.program_id(1) == 0)
    def _():
        st_ref[...] = jnp.zeros_like(st_ref)

    raw = lbraw_ref[...]
    e = jnp.exp(raw - jnp.max(raw, axis=0, keepdims=True))
    soft = e / jnp.sum(e, axis=0, keepdims=True)
    lb = jnp.zeros((1, GROUP_W), F32)
    for i in range(1, layer + 1):
        lb = lb + soft[i:i + 1]

    row = lax.broadcasted_iota(jnp.int32, (T, T), 0)
    col = lax.broadcasted_iota(jnp.int32, (T, T), 1)
    same = (row // C) == (col // C)
    tri = same & ((row <= col) if rev else (row >= col))
    tri_b = jnp.where(tri, 1.0, 0.0).astype(BF16)
    head_mean = jnp.where(same, 1.0 / HEAD_DIM, 0.0).astype(BF16)

    def finish(s):
        if rev:
            o = os_ref[s] + part_ref[s]
            g = g_ref[s].astype(F32)
            o_ref[s] = (_rms_heads(o, nw_ref[...], head_mean) * _silu(g)).astype(BF16)
        else:
            o_ref[s] = os_ref[s]

    def gates(s):
        xf = f_ref[s].astype(F32)
        sg = _sigmoid(xf)
        fg = lb + (1.0 - lb) * sg
        kk = (1.0 - lb) * (1.0 - sg)
        return _silu(q_ref[s].astype(F32)), kk, fg, v_ref[s]

    def factorised():
        seqs = range(nseq)
        pre = []
        for s in seqs:
            q, kk, fg, v = gates(s)
            lf = jnp.log(fg)
            hi = lf.astype(BF16)
            lo = (lf - hi.astype(F32)).astype(BF16)
            cum = _dot(tri_b, hi) + _dot(tri_b, lo)
            pre.append((q, kk, v, cum))
        mid = []
        for s in seqs:
            q, kk, v, cum = pre[s]

            def chunk_rows(idx, cum=cum):
                return jnp.concatenate(
                    [jnp.broadcast_to(cum[c * C + idx:c * C + idx + 1], (C, GROUP_W)) for c in range(HGRN_NB)],
                    axis=0)

            cm = chunk_rows(HGRN_MID)
            last = chunk_rows(0 if rev else C - 1)
            worst = jnp.max(jnp.abs(cum - cm))
            qt = (q * jnp.exp(cum - cm)).astype(BF16)
            kt = (kk * jnp.exp(cm - cum)).astype(BF16)
            qi = (q * jnp.exp(cum)).astype(BF16)
            ki = (kk * jnp.exp(last - cum)).astype(BF16)
            mid.append((worst, qt, kt, qi, ki, jnp.exp(last), v))
        for h in range(N_HEADS):
            hs = slice(h * HEAD_DIM, (h + 1) * HEAD_DIM)
            for s in seqs:
                _, qt, kt, _, _, _, v = mid[s]
                a = jnp.where(tri, _dot_nt(qt[:, hs], kt[:, hs]), 0.0).astype(BF16)
                os_ref[s, :, hs] = _dot(a, v[:, hs])
        st_in = [st_ref[s] for s in seqs]
        st = list(st_in)
        for c in (reversed(range(HGRN_NB)) if rev else range(HGRN_NB)):
            rs = slice(c * C, (c + 1) * C)
            for s in seqs:
                _, _, _, qi, ki, dl, v = mid[s]
                os_ref[s, rs, :] = os_ref[s, rs, :] + _dot_nt(qi[rs], st[s].astype(BF16))
                st[s] = st[s] * dl[c * C:c * C + 1] + jnp.where(same, _dot_tn(v[rs], ki[rs]), 0.0)
        for s in seqs:
            st_ref[s] = st[s]
        return [(mid[s][0], st_in[s]) for s in seqs]

    def token_by_token(s, st_in):
        q, kk, fg, v = gates(s)
        st_ref[s] = st_in
        qs_ref[...] = q
        ks_ref[...] = kk
        fs_ref[...] = fg
        vs_ref[...] = v.astype(F32)

        def step(i, carry):
            t = (T - 1 - i) if rev else i
            k8 = jnp.broadcast_to(ks_ref[pl.ds(t, 1), :], (SUBLANES, GROUP_W))
            v8 = jnp.broadcast_to(vs_ref[pl.ds(t, 1), :], (SUBLANES, GROUP_W)) * (1.0 / SUBLANES)
            st = st_ref[s] * fs_ref[pl.ds(t, 1), :] + jnp.where(same, _dot_tn(v8, k8), 0.0)
            st_ref[s] = st
            q8 = jnp.broadcast_to(qs_ref[pl.ds(t, 1), :], (SUBLANES, GROUP_W))
            os_ref[s, pl.ds(t, 1), :] = _dot_nt(q8, st)[0:1]
            return carry

        lax.fori_loop(0, T, step, 0)

    done = factorised()
    for s in range(nseq):
        finish(s)
    for s in range(nseq):
        worst, st_in = done[s]

        @pl.when(jnp.logical_not(worst <= HGRN_SAFE_EXP))
        def _(s=s, st_in=st_in):
            token_by_token(s, st_in)
            finish(s)


def _hgrn(p3, lb_raw, norm_w, batch, seq, layer):
    C = HGRN_NB * HGRN_C
    assert C == GROUP_W
    nc = seq // C
    ns = SEQS_PER_STEP if batch % SEQS_PER_STEP == 0 else 1
    scratch = ([pltpu.VMEM((ns, GROUP_W, GROUP_W), F32)] + [pltpu.VMEM((C, GROUP_W), F32)] * 4
               + [pltpu.VMEM((ns, C, GROUP_W), F32)])

    def grp(g, rev):
        if rev:
            return pl.BlockSpec((ns, C, GROUP_W), lambda b, j: (b, nc - 1 - j, g))
        return pl.BlockSpec((ns, C, GROUP_W), lambda b, j: (b, j, g))

    part = pl.pallas_call(
        functools.partial(_hgrn_kernel, layer=layer, rev=False),
        grid=(batch // ns, nc),
        in_specs=[grp(P_BQ, False), grp(P_BF_FW, False), grp(P_BI, False), _const_spec((DEPTH, GROUP_W))],
        out_specs=pl.BlockSpec((ns, C, GROUP_W), lambda b, j: (b, j, 0)),
        out_shape=jax.ShapeDtypeStruct((batch, seq, GROUP_W), F32),
        scratch_shapes=scratch,
        compiler_params=_cparams(2),
        name="hgrn_fw",
    )(p3, p3, p3, lb_raw[:, 0])
    return pl.pallas_call(
        functools.partial(_hgrn_kernel, layer=layer, rev=True),
        grid=(batch // ns, nc),
        in_specs=[grp(P_BQ, True), grp(P_BF_BW, True), grp(P_BI, True), _const_spec((DEPTH, GROUP_W)),
                  pl.BlockSpec((ns, C, GROUP_W), lambda b, j: (b, nc - 1 - j, 0)),
                  grp(P_BG, True), _const_spec((1, GROUP_W))],
        out_specs=pl.BlockSpec((ns, C, GROUP_W), lambda b, j: (b, nc - 1 - j, 0)),
        out_shape=jax.ShapeDtypeStruct((batch, seq, GROUP_W), BF16),
        scratch_shapes=scratch,
        compiler_params=_cparams(2),
        name="hgrn_bw",
    )(p3, p3, p3, lb_raw[:, 1], part, p3, norm_w)


def _log_sigmoid(x):
    return jnp.minimum(x, 0.0) - jnp.log(1.0 + jnp.exp(-jnp.abs(x)))


def _ret_kernel(*refs, rev):
    if rev:
        q_ref, k_ref, v_ref, dec_ref, part_ref, g_ref, nw_ref, o_ref, st_ref, xi_ref, zeta_ref = refs
    else:
        q_ref, k_ref, v_ref, dec_ref, o_ref, st_ref, xi_ref, zeta_ref, dm_ref = refs
    C = RET_C

    @pl.when(pl.program_id(1) == 0)
    def _():
        st_ref[...] = jnp.zeros_like(st_ref)

    lg_f = _log_sigmoid(dec_ref[0:1, :])
    lg_b = _log_sigmoid(dec_ref[1:2, :])
    lg = lg_b if rev else lg_f
    if not rev:
        @pl.when((pl.program_id(0) == 0) & (pl.program_id(1) == 0))
        def _():
            row = lax.broadcasted_iota(jnp.int32, (C, C), 0)
            col = lax.broadcasted_iota(jnp.int32, (C, C), 1)
            d_ts = (row - col).astype(F32)
            for h in range(N_HEADS):
                lf1 = lg_f[:, h * HEAD_DIM:h * HEAD_DIM + 1]
                lb1 = lg_b[:, h * HEAD_DIM:h * HEAD_DIM + 1]
                dm_ref[h] = (jnp.where(d_ts >= 0, jnp.exp(jnp.maximum(d_ts, 0.0) * lf1), 0.0)
                             + jnp.where(d_ts <= 0, jnp.exp(jnp.maximum(-d_ts, 0.0) * lb1), 0.0))

    @pl.when((pl.program_id(0) == 0) & (pl.program_id(1) == 0))
    def _():
        t = lax.broadcasted_iota(jnp.int32, (C, 1), 0).astype(F32)
        if rev:
            xi_ref[...] = jnp.exp((C - t) * lg)
            zeta_ref[...] = jnp.exp(t * lg)
        else:
            xi_ref[...] = jnp.exp((t + 1.0) * lg)
            zeta_ref[...] = jnp.exp((C - 1.0 - t) * lg)

    seqs = range(q_ref.shape[0])
    cd = jnp.exp(C * lg)
    pre = []
    for s in seqs:
        q = q_ref[s].astype(F32)
        k = k_ref[s].astype(F32) * HEAD_DIM ** -0.5
        pre.append(((q * xi_ref[...]).astype(BF16), (k * zeta_ref[...]).astype(BF16),
                    q.astype(BF16), k.astype(BF16), v_ref[s]))
    outs = [[] for _ in seqs]
    for h in range(N_HEADS):
        hs = slice(h * HEAD_DIM, (h + 1) * HEAD_DIM)
        for s in seqs:
            qx, kz, qb, kb, v = pre[s]
            st = st_ref[s, h]
            o = _dot(qx[:, hs], st.astype(BF16))
            if not rev:
                sc = (_dot_nt(qb[:, hs], kb[:, hs]) * dm_ref[h]).astype(BF16)
                o = o + _dot(sc, v[:, hs])
            st_ref[s, h] = st * cd[:, h * HEAD_DIM:h * HEAD_DIM + 1] + _dot_tn(kz[:, hs], v[:, hs])
            outs[s].append(o)
    if rev:
        row = lax.broadcasted_iota(jnp.int32, (GROUP_W, GROUP_W), 0)
        col = lax.broadcasted_iota(jnp.int32, (GROUP_W, GROUP_W), 1)
        head_mean = jnp.where(row // HEAD_DIM == col // HEAD_DIM, 1.0 / HEAD_DIM, 0.0).astype(BF16)
    for s in seqs:
        o = jnp.concatenate(outs[s], axis=1)
        if rev:
            g = g_ref[s].astype(F32)
            o_ref[s] = (_rms_heads(o + part_ref[s], nw_ref[...], head_mean) * _silu(g)).astype(BF16)
        else:
            o_ref[s] = o


def _retention(p3, dec_lanes, norm_w, batch, seq):
    C = RET_C
    nc = seq // C
    ns = SEQS_PER_STEP if batch % SEQS_PER_STEP == 0 else 1
    scratch = [pltpu.VMEM((ns, N_HEADS, HEAD_DIM, HEAD_DIM), F32)] + [pltpu.VMEM((C, GROUP_W), F32)] * 2

    def grp(g, rev):
        if rev:
            return pl.BlockSpec((ns, C, GROUP_W), lambda b, j: (b, nc - 1 - j, g))
        return pl.BlockSpec((ns, C, GROUP_W), lambda b, j: (b, j, g))

    part = pl.pallas_call(
        functools.partial(_ret_kernel, rev=False),
        grid=(batch // ns, nc),
        in_specs=[grp(P_DQ, False), grp(P_DK, False), grp(P_DV, False), _const_spec((2, GROUP_W))],
        out_specs=pl.BlockSpec((ns, C, GROUP_W), lambda b, j: (b, j, 0)),
        out_shape=jax.ShapeDtypeStruct((batch, seq, GROUP_W), F32),
        scratch_shapes=scratch + [pltpu.VMEM((N_HEADS, C, C), F32)],
        compiler_params=_cparams(2),
        name="ret_fw",
    )(p3, p3, p3, dec_lanes)
    return pl.pallas_call(
        functools.partial(_ret_kernel, rev=True),
        grid=(batch // ns, nc),
        in_specs=[grp(P_DQ, True), grp(P_DK, True), grp(P_DV, True), _const_spec((2, GROUP_W)),
                  pl.BlockSpec((ns, C, GROUP_W), lambda b, j: (b, nc - 1 - j, 0)),
                  grp(P_DG, True), _const_spec((1, GROUP_W))],
        out_specs=pl.BlockSpec((ns, C, GROUP_W), lambda b, j: (b, nc - 1 - j, 0)),
        out_shape=jax.ShapeDtypeStruct((batch, seq, GROUP_W), BF16),
        scratch_shapes=scratch,
        compiler_params=_cparams(2),
        name="ret_bw",
    )(p3, p3, p3, dec_lanes, part, p3, norm_w)


def _rglru_kernel(*refs, rev, nblk):
    if rev:
        (x_ref, xp_ref, xn_ref, cw_ref, cb_ref, wg_ref, bg_ref, lam_ref, part_ref, gate_ref,
         o_ref, carry_ref, a_ref, u_ref, h_ref) = refs
    else:
        (x_ref, xp_ref, xn_ref, cw_ref, cb_ref, wg_ref, bg_ref, lam_ref,
         o_ref, carry_ref, a_ref, u_ref, h_ref) = refs
    TB = RG_TB
    j = pl.program_id(1)
    blk = (nblk - 1 - j) if rev else j

    @pl.when(j == 0)
    def _():
        carry_ref[...] = jnp.zeros_like(carry_ref)

    prev = jnp.where(blk > 0, xp_ref[0].astype(F32), 0.0)
    nxt = jnp.where(blk < nblk - 1, xn_ref[0].astype(F32), 0.0)
    xe = jnp.concatenate([prev, x_ref[0].astype(F32), nxt], axis=0)
    cw = cw_ref[...]
    xc = cb_ref[...] + sum(
        xe[RG_HALO - CONV_PAD_L + w:RG_HALO - CONV_PAD_L + w + TB] * cw[w:w + 1] for w in range(CONV_W))
    gates = _dot(xc.astype(BF16), wg_ref[...]) + bg_ref[...]
    r = _sigmoid(gates[:, :GROUP_W])
    ig = _sigmoid(gates[:, GROUP_W:])
    log_a = -RG_C * r * _softplus(-lam_ref[...])
    a = jnp.exp(log_a)
    one_m_a2 = 1.0 - a * a
    root = jnp.where(one_m_a2 > 0.0, one_m_a2 * lax.rsqrt(one_m_a2), 0.0)
    u = root * (ig * xc)

    pos = lax.broadcasted_iota(jnp.int32, (TB, GROUP_W), 0) % SUBLANES
    for d in (1, 2, 4):
        if rev:
            keep = pos < SUBLANES - d
            shift = TB - d
        else:
            keep = pos >= d
            shift = d
        a_s = jnp.where(keep, pltpu.roll(a, shift, 0), 1.0)
        u_s = jnp.where(keep, pltpu.roll(u, shift, 0), 0.0)
        u = a * u_s + u
        a = a * a_s
    a_ref[...] = a
    u_ref[...] = u
    ntile = TB // SUBLANES

    def tile_step(i, hprev):
        ti = (ntile - 1 - i) if rev else i
        r0 = pl.multiple_of(ti * SUBLANES, SUBLANES)
        h = a_ref[pl.ds(r0, SUBLANES), :] * hprev + u_ref[pl.ds(r0, SUBLANES), :]
        h_ref[pl.ds(r0, SUBLANES), :] = h
        edge = h[0:1] if rev else h[SUBLANES - 1:SUBLANES]
        return jnp.broadcast_to(edge, (SUBLANES, GROUP_W))

    carry_ref[...] = lax.fori_loop(0, ntile, tile_step, carry_ref[...])
    if rev:
        hs = h_ref[...] + part_ref[0]
        o_ref[0] = (hs * jax.nn.gelu(gate_ref[0].astype(F32), approximate=True)).astype(BF16)
    else:
        o_ref[0] = h_ref[...]


def _rglru(p3, conv_w, conv_b, wg, bg, lam, batch, seq):
    TB = RG_TB
    nblk = seq // TB
    hpb = TB // RG_HALO
    nhalo = seq // RG_HALO
    scratch = [pltpu.VMEM((SUBLANES, GROUP_W), F32)] + [pltpu.VMEM((TB, GROUP_W), F32)] * 3

    def specs(rev):
        def blk(j):
            return (nblk - 1 - j) if rev else j
        return [
            pl.BlockSpec((1, TB, GROUP_W), lambda b, j: (b, blk(j), P_CX)),
            pl.BlockSpec((1, RG_HALO, GROUP_W), lambda b, j: (b, jnp.maximum(blk(j) * hpb - 1, 0), P_CX)),
            pl.BlockSpec((1, RG_HALO, GROUP_W),
                         lambda b, j: (b, jnp.minimum((blk(j) + 1) * hpb, nhalo - 1), P_CX)),
            _const_spec((CONV_W, GROUP_W)),
            _const_spec((1, GROUP_W)),
            _const_spec((GROUP_W, 2 * GROUP_W)),
            _const_spec((1, 2 * GROUP_W)),
            _const_spec((1, GROUP_W)),
        ]

    part = pl.pallas_call(
        functools.partial(_rglru_kernel, rev=False, nblk=nblk),
        grid=(batch, nblk),
        in_specs=specs(False),
        out_specs=pl.BlockSpec((1, TB, GROUP_W), lambda b, j: (b, j, 0)),
        out_shape=jax.ShapeDtypeStruct((batch, seq, GROUP_W), F32),
        scratch_shapes=scratch,
        compiler_params=_cparams(2),
        name="rglru_fw",
    )(p3, p3, p3, conv_w, conv_b, wg[0], bg[0], lam[0:1])
    return pl.pallas_call(
        functools.partial(_rglru_kernel, rev=True, nblk=nblk),
        grid=(batch, nblk),
        in_specs=specs(True) + [
            pl.BlockSpec((1, TB, GROUP_W), lambda b, j: (b, nblk - 1 - j, 0)),
            pl.BlockSpec((1, TB, GROUP_W), lambda b, j: (b, nblk - 1 - j, P_CG)),
        ],
        out_specs=pl.BlockSpec((1, TB, GROUP_W), lambda b, j: (b, nblk - 1 - j, 0)),
        out_shape=jax.ShapeDtypeStruct((batch, seq, GROUP_W), BF16),
        scratch_shapes=scratch,
        compiler_params=_cparams(2),
        name="rglru_bw",
    )(p3, p3, p3, conv_w, conv_b, wg[1], bg[1], lam[1:2], part, p3)


def _post_kernel(oa_ref, ob_ref, oc_ref, od_ref, r_ref, p_ref, nw_ref, wo_ref, wig_ref, wiu_ref, wfo_ref,
                 wpg_ref, wpp_ref, out_ref):
    nw = nw_ref[...]
    mixed_in = jnp.concatenate([oa_ref[...], ob_ref[...], oc_ref[...], od_ref[...]], axis=1)
    r = r_ref[...] + _rms(_dot(mixed_in, wo_ref[...]), nw[1:2])
    h2 = _rms(r, nw[2:3]).astype(BF16)
    ff = jnp.zeros_like(r)
    for c in range(N_FF_CHUNKS):
        act = _silu(_dot(h2, wig_ref[c])) * _dot(h2, wiu_ref[c])
        ff = ff + _dot(act.astype(BF16), wfo_ref[c])
    r = r + _rms(ff, nw[3:4])
    gate = _sigmoid(_dot(r.astype(BF16), wpg_ref[...]))
    out_ref[...] = r + gate * _dot(p_ref[0].astype(BF16), wpp_ref[...])


def _post(oa, ob, oc, od, r, p, layer, nw, wo, wig, wiu, wfo, wpg, wpp):
    tokens = r.shape[0]
    tm = TM_POST
    mix_spec = pl.BlockSpec((tm, GROUP_W), lambda i: (i, 0))
    return pl.pallas_call(
        _post_kernel,
        grid=(tokens // tm,),
        in_specs=[
            mix_spec, mix_spec, mix_spec, mix_spec,
            pl.BlockSpec((tm, D_MODEL), lambda i: (i, 0)),
            pl.BlockSpec((1, tm, PLE_DIM), lambda i: (layer, i, 0)),
            _const_spec((4, D_MODEL)),
            _const_spec((D_MODEL, D_MODEL)),
            _const_spec((N_FF_CHUNKS, D_MODEL, FF_CHUNK)),
            _const_spec((N_FF_CHUNKS, D_MODEL, FF_CHUNK)),
            _const_spec((N_FF_CHUNKS, FF_CHUNK, D_MODEL)),
            _const_spec((D_MODEL, D_MODEL)),
            _const_spec((PLE_DIM, D_MODEL)),
        ],
        out_specs=pl.BlockSpec((tm, D_MODEL), lambda i: (i, 0)),
        out_shape=jax.ShapeDtypeStruct((tokens, D_MODEL), F32),
        compiler_params=_cparams(1),
        name="post",
    )(oa, ob, oc, od, r, p, nw, wo, wig, wiu, wfo, wpg, wpp)


def _block_diag(w):
    out = jnp.zeros((GROUP_W, GROUP_W), w.dtype)
    for h in range(N_HEADS):
        out = out.at[h * HEAD_DIM:(h + 1) * HEAD_DIM, h * HEAD_DIM:(h + 1) * HEAD_DIM].set(w[h])
    return out


def _prep_weights(norm_w, w_in, diff_lambda, diff_subln_w, hgrn_lb_raw, hgrn_norm_w, rg_conv_w, rg_conv_b,
                  rg_w_a, rg_b_a, rg_w_x, rg_b_x, rg_lambda, ret_decay, ret_norm_w, w_out, w_ffn_in,
                  w_ffn_out, w_ple_gate, w_ple_proj):
    layers = []
    for l in range(DEPTH):
        wi = w_in[l]
        w_nat = jnp.concatenate([wi[:, GROUP_W:2 * GROUP_W], wi[:, 3 * GROUP_W:]], axis=1).astype(BF16)
        w_qv_t = jnp.concatenate([wi[:, :GROUP_W], wi[:, 2 * GROUP_W:3 * GROUP_W]], axis=1).T.astype(BF16)
        wg = [jnp.concatenate([_block_diag(rg_w_a[l, d]), _block_diag(rg_w_x[l, d])], axis=1).astype(BF16)
              for d in range(2)]
        bg = [jnp.concatenate([rg_b_a[l, d], rg_b_x[l, d]])[None, :] for d in range(2)]
        wfi = w_ffn_in[l].astype(BF16)
        layers.append(dict(
            nw=norm_w[l], nw0=norm_w[l, 0:1], w_nat=w_nat, w_qv_t=w_qv_t,
            dl=diff_lambda[l], sw=diff_subln_w[l][:, None],
            lb_raw=hgrn_lb_raw, hgrn_nw=jnp.tile(hgrn_norm_w[l], N_HEADS)[None, :],
            conv_w=rg_conv_w[l], conv_b=rg_conv_b[l][None, :], wg=wg, bg=bg, lam=rg_lambda[l],
            dec=jnp.repeat(ret_decay[l], HEAD_DIM, axis=-1), ret_nw=jnp.tile(ret_norm_w[l], N_HEADS)[None, :],
            wo=w_out[l].astype(BF16),
            wig=wfi[:, :D_FF].reshape(D_MODEL, N_FF_CHUNKS, FF_CHUNK).transpose(1, 0, 2),
            wiu=wfi[:, D_FF:].reshape(D_MODEL, N_FF_CHUNKS, FF_CHUNK).transpose(1, 0, 2),
            wfo=w_ffn_out[l].astype(BF16).reshape(N_FF_CHUNKS, FF_CHUNK, D_MODEL),
            wpg=w_ple_gate[l].astype(BF16), wpp=w_ple_proj[l].astype(BF16),
        ))
    return layers


def _trunk(x, p, layers):
    batch, seq, _ = x.shape
    tokens = batch * seq
    r = x.reshape(tokens, D_MODEL)
    p = p.reshape(DEPTH, tokens, PLE_DIM)
    for l, w in enumerate(layers):
        lambda_init = 0.8 - 0.6 * math.exp(-0.3 * l)
        pn, qt, vt = _inproj(r, w["nw0"], w["w_nat"], w["w_qv_t"], batch, seq)
        p3 = pn.reshape(batch, seq, N_PGROUPS * GROUP_W)
        oa = _attention(qt, p3, vt, w["dl"], w["sw"], batch, seq, lambda_init)
        ob = _hgrn(p3, w["lb_raw"], w["hgrn_nw"], batch, seq, l)
        oc = _rglru(p3, w["conv_w"], w["conv_b"], w["wg"], w["bg"], w["lam"], batch, seq)
        od = _retention(p3, w["dec"], w["ret_nw"], batch, seq)
        flat = lambda o: o.reshape(tokens, GROUP_W)
        r = _post(flat(oa), flat(ob), flat(oc), flat(od), r, p, l, w["nw"], w["wo"], w["wig"], w["wiu"],
                  w["wfo"], w["wpg"], w["wpp"])
    return r.reshape(batch, seq, D_MODEL)


def kernel(x_prompt, x_sample, p_prompt, p_sample, norm_w, w_in, diff_lambda, diff_subln_w, hgrn_lb_raw,
           hgrn_norm_w, rg_conv_w, rg_conv_b, rg_w_a, rg_b_a, rg_w_x, rg_b_x, rg_lambda, ret_decay,
           ret_norm_w, w_out, w_ffn_in, w_ffn_out, w_ple_gate, w_ple_proj):
    layers = _prep_weights(norm_w, w_in, diff_lambda, diff_subln_w, hgrn_lb_raw, hgrn_norm_w, rg_conv_w,
                           rg_conv_b, rg_w_a, rg_b_a, rg_w_x, rg_b_x, rg_lambda, ret_decay, ret_norm_w,
                           w_out, w_ffn_in, w_ffn_out, w_ple_gate, w_ple_proj)
    return (_trunk(x_prompt, p_prompt, layers), _trunk(x_sample, p_sample, layers))
```

```python
import functools
import math

import jax
import jax.numpy as jnp
from jax import lax
from jax.experimental import pallas as pl
from jax.experimental.pallas import tpu as pltpu

F32 = jnp.float32
BF16 = jnp.bfloat16

D_MODEL = 1024
DEPTH = 4
GROUP_W = 256
N_HEADS = 4
HEAD_DIM = 64
DIFF_HALF = 32
D_FF = 2816
FF_CHUNK = 256
N_FF_CHUNKS = D_FF // FF_CHUNK
PLE_DIM = 256
CONV_W = 4
CONV_PAD_L = 2
RG_C = 8.0
EPS = 1e-6
LOG2E = 1.4426950408889634

(P_AK, P_BQ, P_BF_FW, P_BF_BW, P_BI, P_BG, P_CX, P_CG, P_DQ, P_DK, P_DV, P_DG) = range(12)
N_PGROUPS = 12

VMEM_LIMIT_V7X = 56 * 1024 * 1024

TM_INPROJ = 512
TM_POST = 512
TQ_ATTN = 256
TK_ATTN = 256
ATTN_SKEW = 4
ATTN_SLOTS = 8
ATTN_UNROLL = 16
HGRN_C = 64
HGRN_NB = 4
SEQS_PER_STEP = 4
HGRN_MID = HGRN_C // 2
HGRN_SAFE_EXP = 80.0
RET_C = 256
RG_TB = 256
RG_HALO = 16
SUBLANES = 8


def _cparams(n_axes):
    return pltpu.CompilerParams(dimension_semantics=("arbitrary",) * n_axes,
                                vmem_limit_bytes=VMEM_LIMIT_V7X)


def _const_spec(shape):
    nd = len(shape)
    return pl.BlockSpec(shape, lambda *_: (0,) * nd, pipeline_mode=pl.Buffered(1))


def _rms(x, w):
    return x * lax.rsqrt(jnp.mean(x * x, axis=-1, keepdims=True) + EPS) * w


def _rms_heads(x, w, head_mean):
    sq = x * x
    hi = sq.astype(BF16)
    lo = (sq - hi.astype(F32)).astype(BF16)
    ms = _dot(hi, head_mean) + _dot(lo, head_mean)
    return x * lax.rsqrt(ms + EPS) * w


def _sigmoid(x):
    return jax.nn.sigmoid(x)


def _silu(x):
    return x * _sigmoid(x)


def _softplus(x):
    return jnp.maximum(x, 0.0) + jnp.log(1.0 + jnp.exp(-jnp.abs(x)))


def _dot(a, b):
    return jnp.dot(a, b, preferred_element_type=F32)


def _dot_nt(a, b):
    return lax.dot_general(a, b, (((1,), (1,)), ((), ())), preferred_element_type=F32)


def _dot_tn(a, b):
    return lax.dot_general(a, b, (((0,), (0,)), ((), ())), preferred_element_type=F32)


def _inproj_kernel(x_ref, nw_ref, wn_ref, wt_ref, p_ref, qt_ref, vt_ref):
    x = x_ref[...]
    y = _rms(x, nw_ref[...]).astype(BF16)
    for g in range(N_PGROUPS):
        cols = slice(g * GROUP_W, (g + 1) * GROUP_W)
        p_ref[:, cols] = _dot(y, wn_ref[:, cols]).astype(BF16)
    t = _dot_nt(wt_ref[...], y)
    qt_ref[0] = (t[:GROUP_W] * (DIFF_HALF ** -0.5 * LOG2E)).astype(BF16)
    vt_ref[0] = t[GROUP_W:].astype(BF16)


def _inproj(r, nw, w_nat, w_qv_t, batch, seq):
    tokens = batch * seq
    tm = TM_INPROJ
    nsb = seq // tm
    return pl.pallas_call(
        _inproj_kernel,
        grid=(tokens // tm,),
        in_specs=[
            pl.BlockSpec((tm, D_MODEL), lambda i: (i, 0)),
            _const_spec((1, D_MODEL)),
            _const_spec((D_MODEL, N_PGROUPS * GROUP_W)),
            _const_spec((2 * GROUP_W, D_MODEL)),
        ],
        out_specs=[
            pl.BlockSpec((tm, N_PGROUPS * GROUP_W), lambda i: (i, 0)),
            pl.BlockSpec((1, GROUP_W, tm), lambda i: (i // nsb, 0, i % nsb)),
            pl.BlockSpec((1, GROUP_W, tm), lambda i: (i // nsb, 0, i % nsb)),
        ],
        out_shape=[
            jax.ShapeDtypeStruct((tokens, N_PGROUPS * GROUP_W), BF16),
            jax.ShapeDtypeStruct((batch, GROUP_W, seq), BF16),
            jax.ShapeDtypeStruct((batch, GROUP_W, seq), BF16),
        ],
        compiler_params=_cparams(1),
        name="inproj",
    )(r, nw, w_nat, w_qv_t)


def _attn_kernel(qt_ref, k_ref, vt_ref, dl_ref, sw_ref, o_ref, w_ref, m_ref, l_ref, acc_ref, s_ref, t_ref,
                 *, seq, lambda_init):
    tq, tk = TQ_ATTN, TK_ATTN
    assert tq == tk
    q0 = pl.program_id(1) * tq
    nk = seq // tk
    n_chain = 2 * N_HEADS
    sub = lax.broadcasted_iota(jnp.int32, (2 * HEAD_DIM, tq), 0)
    for h in range(N_HEADS):
        half = h // 2
        qt_half = qt_ref[0, half * 2 * HEAD_DIM:(half + 1) * 2 * HEAD_DIM, :]
        for c in range(2):
            lo = (h % 2) * HEAD_DIM + c * DIFF_HALF
            w_ref[2 * h + c] = jnp.where((sub >= lo) & (sub < lo + DIFF_HALF), qt_half,
                                         jnp.zeros_like(qt_half))
    slopes2 = [2.0 ** (-8.0 * (h + 1) / N_HEADS) * LOG2E for h in range(N_HEADS)]

    @pl.when((pl.program_id(0) == 0) & (pl.program_id(1) == 0))
    def _():
        row = lax.broadcasted_iota(jnp.int32, (tk, tq), 0)
        col = lax.broadcasted_iota(jnp.int32, (tk, tq), 1)
        d = (row - col).astype(F32)
        for h in range(N_HEADS):
            t_ref[h, 0] = slopes2[h] * d
            t_ref[h, 1] = -slopes2[h] * jnp.abs(d)
            t_ref[h, 2] = -slopes2[h] * d

    def scores(jj, i):
        k0 = pl.multiple_of(jj * tk, tk)
        half = i // 4
        kt = k_ref[0, pl.ds(k0, tk), half * 2 * HEAD_DIM:(half + 1) * 2 * HEAD_DIM]
        s_ref[i % ATTN_SLOTS] = _dot(kt, w_ref[i])

    def consume(j, i, sel, off, online):
        h = i // 2
        k0 = pl.multiple_of(j * tk, tk)
        u = s_ref[i % ATTN_SLOTS] + t_ref[h, sel]
        vt = vt_ref[0, h * HEAD_DIM:(h + 1) * HEAD_DIM, pl.ds(k0, tk)]
        m = m_ref[i]
        if online:
            m_new = jnp.maximum(m, jnp.max(u, axis=0, keepdims=True) - off[h])
            p = jnp.exp2(u - (m_new + off[h]))
            alpha = jnp.exp2(m - m_new)
            acc_ref[i] = acc_ref[i] * alpha + _dot(vt, p.astype(BF16))
            l_ref[i] = l_ref[i] * alpha + jnp.sum(p, axis=0, keepdims=True)
            m_ref[i] = m_new
        else:
            p = jnp.exp2(u - (m + off[h]))
            acc_ref[i] = acc_ref[i] + _dot(vt, p.astype(BF16))
            l_ref[i] = l_ref[i] + jnp.sum(p, axis=0, keepdims=True)

    def key_sweep(unroll, online):
        assert nk % unroll == 0
        for i in range(ATTN_SKEW):
            scores(0, i)

        def body(jo, carry):
            qi = pl.program_id(1)
            for ju in range(unroll):
                j = jo * unroll + ju
                jn = jnp.minimum(j + 1, nk - 1)
                sel = (j >= qi).astype(jnp.int32) + (j > qi).astype(jnp.int32)
                gap = jnp.abs(q0 - j * tk).astype(F32)
                off = [slopes2[h] * gap for h in range(N_HEADS)]
                for i in range(n_chain):
                    a = i + ATTN_SKEW
                    if a < n_chain:
                        scores(j, a)
                    else:
                        scores(jn, a - n_chain)
                    consume(j, i, sel, off, online)
            return carry

        lax.fori_loop(0, nk // unroll, body, 0)

    for i in range(n_chain):
        h = i // 2
        half = i // 4
        kt = k_ref[0, pl.ds(pl.multiple_of(q0, tk), tk), half * 2 * HEAD_DIM:(half + 1) * 2 * HEAD_DIM]
        m_ref[i] = jnp.max(_dot(kt, w_ref[i]) + t_ref[h, 1], axis=0, keepdims=True)
    dl = dl_ref[...]
    lam = (jnp.exp(jnp.sum(dl[0:1] * dl[1:2], axis=1, keepdims=True))
           - jnp.exp(jnp.sum(dl[2:3] * dl[3:4], axis=1, keepdims=True)) + lambda_init)

    def finish():
        outs = []
        for h in range(N_HEADS):
            o = (acc_ref[2 * h] / l_ref[2 * h] - lam * (acc_ref[2 * h + 1] / l_ref[2 * h + 1]))
            ms = jnp.mean(o * o, axis=0, keepdims=True)
            outs.append(o * lax.rsqrt(ms + EPS) * sw_ref[...] * (1.0 - lambda_init))
        o_ref[0] = jnp.concatenate(outs, axis=0).T.astype(BF16)

    acc_ref[...] = jnp.zeros_like(acc_ref)
    l_ref[...] = jnp.zeros_like(l_ref)
    key_sweep(ATTN_UNROLL, online=False)
    poisoned = (jnp.sum(acc_ref[...] * 0.0) + jnp.sum(l_ref[...] * 0.0)) != 0.0
    finish()

    @pl.when(poisoned)
    def _():
        m_ref[...] = jnp.full(m_ref.shape, -1e30, F32)
        acc_ref[...] = jnp.zeros_like(acc_ref)
        l_ref[...] = jnp.zeros_like(l_ref)
        key_sweep(1, online=True)
        finish()


def _attention(qt, p3, vt, dl, sw, batch, seq, lambda_init):
    tq = TQ_ATTN
    n_chain = 2 * N_HEADS
    scratch = [
        pltpu.VMEM((n_chain, 2 * HEAD_DIM, tq), BF16),
        pltpu.VMEM((n_chain, 1, tq), F32),
        pltpu.VMEM((n_chain, 1, tq), F32),
        pltpu.VMEM((n_chain, HEAD_DIM, tq), F32),
        pltpu.VMEM((ATTN_SLOTS, TK_ATTN, tq), F32),
        pltpu.VMEM((N_HEADS, 3, TK_ATTN, tq), F32),
    ]
    return pl.pallas_call(
        functools.partial(_attn_kernel, seq=seq, lambda_init=lambda_init),
        grid=(batch, seq // tq),
        in_specs=[
            pl.BlockSpec((1, GROUP_W, tq), lambda b, i: (b, 0, i)),
            pl.BlockSpec((1, seq, GROUP_W), lambda b, i: (b, 0, P_AK), pipeline_mode=pl.Buffered(1)),
            pl.BlockSpec((1, GROUP_W, seq), lambda b, i: (b, 0, 0), pipeline_mode=pl.Buffered(1)),
            _const_spec((4, DIFF_HALF)),
            _const_spec((HEAD_DIM, 1)),
        ],
        out_specs=pl.BlockSpec((1, tq, GROUP_W), lambda b, i: (b, i, 0)),
        out_shape=jax.ShapeDtypeStruct((batch, seq, GROUP_W), BF16),
        scratch_shapes=scratch,
        compiler_params=_cparams(2),
        name="diff_attn",
    )(qt, p3, vt, dl, sw)


def _hgrn_kernel(*refs, layer, rev):
    if rev:
        (q_ref, f_ref, v_ref, lbraw_ref, part_ref, g_ref, nw_ref, o_ref,
         st_ref, qs_ref, ks_ref, fs_ref, vs_ref, os_ref) = refs
    else:
        (q_ref, f_ref, v_ref, lbraw_ref, o_ref,
         st_ref, qs_ref, ks_ref, fs_ref, vs_ref, os_ref) = refs
    C = HGRN_C
    T = HGRN_NB * C
    nseq = q_ref.shape[0]

    @pl.when(pl.program_id(1) == 0)
    def _():
        st_ref[...] = jnp.zeros_like(st_ref)

    raw = lbraw_ref[...]
    e = jnp.exp(raw - jnp.max(raw, axis=0, keepdims=True))
    soft = e / jnp.sum(e, axis=0, keepdims=True)
    lb = jnp.zeros((1, GROUP_W), F32)
    for i in range(1, layer + 1):
        lb = lb + soft[i:i + 1]

    row = lax.broadcasted_iota(jnp.int32, (T, T), 0)
    col = lax.broadcasted_iota(jnp.int32, (T, T), 1)
    same = (row // C) == (col // C)
    tri = same & ((row <= col) if rev else (row >= col))
    tri_b = jnp.where(tri, 1.0, 0.0).astype(BF16)
    head_mean = jnp.where(same, 1.0 / HEAD_DIM, 0.0).astype(BF16)

    def finish(s):
        if rev:
            o = os_ref[s] + part_ref[s]
            g = g_ref[s].astype(F32)
            o_ref[s] = (_rms_heads(o, nw_ref[...], head_mean) * _silu(g)).astype(BF16)
        else:
            o_ref[s] = os_ref[s]

    def gates(s):
        xf = f_ref[s].astype(F32)
        sg = _sigmoid(xf)
        fg = lb + (1.0 - lb) * sg
        kk = (1.0 - lb) * (1.0 - sg)
        return _silu(q_ref[s].astype(F32)), kk, fg, v_ref[s]

    def factorised():
        seqs = range(nseq)
        pre = []
        for s in seqs:
            q, kk, fg, v = gates(s)
            lf = jnp.log(fg)
            hi = lf.astype(BF16)
            lo = (lf - hi.astype(F32)).astype(BF16)
            cum = _dot(tri_b, hi) + _dot(tri_b, lo)
            pre.append((q, kk, v, cum))
        mid = []
        for s in seqs:
            q, kk, v, cum = pre[s]

            def chunk_rows(idx, cum=cum):
                return jnp.concatenate(
                    [jnp.broadcast_to(cum[c * C + idx:c * C + idx + 1], (C, GROUP_W)) for c in range(HGRN_NB)],
                    axis=0)

            cm = chunk_rows(HGRN_MID)
            last = chunk_rows(0 if rev else C - 1)
            worst = jnp.max(jnp.abs(cum - cm))
            qt = (q * jnp.exp(cum - cm)).astype(BF16)
            kt = (kk * jnp.exp(cm - cum)).astype(BF16)
            qi = (q * jnp.exp(cum)).astype(BF16)
            ki = (kk * jnp.exp(last - cum)).astype(BF16)
            mid.append((worst, qt, kt, qi, ki, jnp.exp(last), v))
        for h in range(N_HEADS):
            hs = slice(h * HEAD_DIM, (h + 1) * HEAD_DIM)
            for s in seqs:
                _, qt, kt, _, _, _, v = mid[s]
                a = jnp.where(tri, _dot_nt(qt[:, hs], kt[:, hs]), 0.0).astype(BF16)
                os_ref[s, :, hs] = _dot(a, v[:, hs])
        st_in = [st_ref[s] for s in seqs]
        st = list(st_in)
        for c in (reversed(range(HGRN_NB)) if rev else range(HGRN_NB)):
            rs = slice(c * C, (c + 1) * C)
            for s in seqs:
                _, _, _, qi, ki, dl, v = mid[s]
                os_ref[s, rs, :] = os_ref[s, rs, :] + _dot_nt(qi[rs], st[s].astype(BF16))
                st[s] = st[s] * dl[c * C:c * C + 1] + jnp.where(same, _dot_tn(v[rs], ki[rs]), 0.0)
        for s in seqs:
            st_ref[s] = st[s]
        return [(mid[s][0], st_in[s]) for s in seqs]

    def token_by_token(s, st_in):
        q, kk, fg, v = gates(s)
        st_ref[s] = st_in
        qs_ref[...] = q
        ks_ref[...] = kk
        fs_ref[...] = fg
        vs_ref[...] = v.astype(F32)

        def step(i, carry):
            t = (T - 1 - i) if rev else i
            k8 = jnp.broadcast_to(ks_ref[pl.ds(t, 1), :], (SUBLANES, GROUP_W))
            v8 = jnp.broadcast_to(vs_ref[pl.ds(t, 1), :], (SUBLANES, GROUP_W)) * (1.0 / SUBLANES)
            st = st_ref[s] * fs_ref[pl.ds(t, 1), :] + jnp.where(same, _dot_tn(v8, k8), 0.0)
            st_ref[s] = st
            q8 = jnp.broadcast_to(qs_ref[pl.ds(t, 1), :], (SUBLANES, GROUP_W))
            os_ref[s, pl.ds(t, 1), :] = _dot_nt(q8, st)[0:1]
            return carry

        lax.fori_loop(0, T, step, 0)

    done = factorised()
    for s in range(nseq):
        finish(s)
    for s in range(nseq):
        worst, st_in = done[s]

        @pl.when(jnp.logical_not(worst <= HGRN_SAFE_EXP))
        def _(s=s, st_in=st_in):
            token_by_token(s, st_in)
            finish(s)


def _hgrn(p3, lb_raw, norm_w, batch, seq, layer):
    C = HGRN_NB * HGRN_C
    assert C == GROUP_W
    nc = seq // C
    ns = SEQS_PER_STEP if batch % SEQS_PER_STEP == 0 else 1
    scratch = ([pltpu.VMEM((ns, GROUP_W, GROUP_W), F32)] + [pltpu.VMEM((C, GROUP_W), F32)] * 4
               + [pltpu.VMEM((ns, C, GROUP_W), F32)])

    def grp(g, rev):
        if rev:
            return pl.BlockSpec((ns, C, GROUP_W), lambda b, j: (b, nc - 1 - j, g))
        return pl.BlockSpec((ns, C, GROUP_W), lambda b, j: (b, j, g))

    part = pl.pallas_call(
        functools.partial(_hgrn_kernel, layer=layer, rev=False),
        grid=(batch // ns, nc),
        in_specs=[grp(P_BQ, False), grp(P_BF_FW, False), grp(P_BI, False), _const_spec((DEPTH, GROUP_W))],
        out_specs=pl.BlockSpec((ns, C, GROUP_W), lambda b, j: (b, j, 0)),
        out_shape=jax.ShapeDtypeStruct((batch, seq, GROUP_W), F32),
        scratch_shapes=scratch,
        compiler_params=_cparams(2),
        name="hgrn_fw",
    )(p3, p3, p3, lb_raw[:, 0])
    return pl.pallas_call(
        functools.partial(_hgrn_kernel, layer=layer, rev=True),
        grid=(batch // ns, nc),
        in_specs=[grp(P_BQ, True), grp(P_BF_BW, True), grp(P_BI, True), _const_spec((DEPTH, GROUP_W)),
                  pl.BlockSpec((ns, C, GROUP_W), lambda b, j: (b, nc - 1 - j, 0)),
                  grp(P_BG, True), _const_spec((1, GROUP_W))],
        out_specs=pl.BlockSpec((ns, C, GROUP_W), lambda b, j: (b, nc - 1 - j, 0)),
        out_shape=jax.ShapeDtypeStruct((batch, seq, GROUP_W), BF16),
        scratch_shapes=scratch,
        compiler_params=_cparams(2),
        name="hgrn_bw",
    )(p3, p3, p3, lb_raw[:, 1], part, p3, norm_w)


def _log_sigmoid(x):
    return jnp.minimum(x, 0.0) - jnp.log(1.0 + jnp.exp(-jnp.abs(x)))


def _ret_kernel(*refs, rev):
    if rev:
        q_ref, k_ref, v_ref, dec_ref, part_ref, g_ref, nw_ref, o_ref, st_ref, xi_ref, zeta_ref = refs
    else:
        q_ref, k_ref, v_ref, dec_ref, o_ref, st_ref, xi_ref, zeta_ref, dm_ref = refs
    C = RET_C

    @pl.when(pl.program_id(1) == 0)
    def _():
        st_ref[...] = jnp.zeros_like(st_ref)

    lg_f = _log_sigmoid(dec_ref[0:1, :])
    lg_b = _log_sigmoid(dec_ref[1:2, :])
    lg = lg_b if rev else lg_f
    if not rev:
        @pl.when((pl.program_id(0) == 0) & (pl.program_id(1) == 0))
        def _():
            row = lax.broadcasted_iota(jnp.int32, (C, C), 0)
            col = lax.broadcasted_iota(jnp.int32, (C, C), 1)
            d_ts = (row - col).astype(F32)
            for h in range(N_HEADS):
                lf1 = lg_f[:, h * HEAD_DIM:h * HEAD_DIM + 1]
                lb1 = lg_b[:, h * HEAD_DIM:h * HEAD_DIM + 1]
                dm_ref[h] = (jnp.where(d_ts >= 0, jnp.exp(jnp.maximum(d_ts, 0.0) * lf1), 0.0)
                             + jnp.where(d_ts <= 0, jnp.exp(jnp.maximum(-d_ts, 0.0) * lb1), 0.0))

    @pl.when((pl.program_id(0) == 0) & (pl.program_id(1) == 0))
    def _():
        t = lax.broadcasted_iota(jnp.int32, (C, 1), 0).astype(F32)
        if rev:
            xi_ref[...] = jnp.exp((C - t) * lg)
            zeta_ref[...] = jnp.exp(t * lg)
        else:
            xi_ref[...] = jnp.exp((t + 1.0) * lg)
            zeta_ref[...] = jnp.exp((C - 1.0 - t) * lg)

    seqs = range(q_ref.shape[0])
    cd = jnp.exp(C * lg)
    pre = []
    for s in seqs:
        q = q_ref[s].astype(F32)
        k = k_ref[s].astype(F32) * HEAD_DIM ** -0.5
        pre.append(((q * xi_ref[...]).astype(BF16), (k * zeta_ref[...]).astype(BF16),
                    q.astype(BF16), k.astype(BF16), v_ref[s]))
    outs = [[] for _ in seqs]
    for h in range(N_HEADS):
        hs = slice(h * HEAD_DIM, (h + 1) * HEAD_DIM)
        sc = [_dot_nt(pre[s][2][:, hs], pre[s][3][:, hs]) for s in seqs] if not rev else None
        st = [st_ref[s, h] for s in seqs]
        o = [_dot(pre[s][0][:, hs], st[s].astype(BF16)) for s in seqs]
        upd = [_dot_tn(pre[s][1][:, hs], pre[s][4][:, hs]) for s in seqs]
        for s in seqs:
            if not rev:
                o[s] = o[s] + _dot((sc[s] * dm_ref[h]).astype(BF16), pre[s][4][:, hs])
            st_ref[s, h] = st[s] * cd[:, h * HEAD_DIM:h * HEAD_DIM + 1] + upd[s]
            outs[s].append(o[s])
    if rev:
        row = lax.broadcasted_iota(jnp.int32, (GROUP_W, GROUP_W), 0)
        col = lax.broadcasted_iota(jnp.int32, (GROUP_W, GROUP_W), 1)
        head_mean = jnp.where(row // HEAD_DIM == col // HEAD_DIM, 1.0 / HEAD_DIM, 0.0).astype(BF16)
    for s in seqs:
        o = jnp.concatenate(outs[s], axis=1)
        if rev:
            g = g_ref[s].astype(F32)
            o_ref[s] = (_rms_heads(o + part_ref[s], nw_ref[...], head_mean) * _silu(g)).astype(BF16)
        else:
            o_ref[s] = o


def _retention(p3, dec_lanes, norm_w, batch, seq):
    C = RET_C
    nc = seq // C
    ns = SEQS_PER_STEP if batch % SEQS_PER_STEP == 0 else 1
    scratch = [pltpu.VMEM((ns, N_HEADS, HEAD_DIM, HEAD_DIM), F32)] + [pltpu.VMEM((C, GROUP_W), F32)] * 2

    def grp(g, rev):
        if rev:
            return pl.BlockSpec((ns, C, GROUP_W), lambda b, j: (b, nc - 1 - j, g))
        return pl.BlockSpec((ns, C, GROUP_W), lambda b, j: (b, j, g))

    part = pl.pallas_call(
        functools.partial(_ret_kernel, rev=False),
        grid=(batch // ns, nc),
        in_specs=[grp(P_DQ, False), grp(P_DK, False), grp(P_DV, False), _const_spec((2, GROUP_W))],
        out_specs=pl.BlockSpec((ns, C, GROUP_W), lambda b, j: (b, j, 0)),
        out_shape=jax.ShapeDtypeStruct((batch, seq, GROUP_W), F32),
        scratch_shapes=scratch + [pltpu.VMEM((N_HEADS, C, C), F32)],
        compiler_params=_cparams(2),
        name="ret_fw",
    )(p3, p3, p3, dec_lanes)
    return pl.pallas_call(
        functools.partial(_ret_kernel, rev=True),
        grid=(batch // ns, nc),
        in_specs=[grp(P_DQ, True), grp(P_DK, True), grp(P_DV, True), _const_spec((2, GROUP_W)),
                  pl.BlockSpec((ns, C, GROUP_W), lambda b, j: (b, nc - 1 - j, 0)),
                  grp(P_DG, True), _const_spec((1, GROUP_W))],
        out_specs=pl.BlockSpec((ns, C, GROUP_W), lambda b, j: (b, nc - 1 - j, 0)),
        out_shape=jax.ShapeDtypeStruct((batch, seq, GROUP_W), BF16),
        scratch_shapes=scratch,
        compiler_params=_cparams(2),
        name="ret_bw",
    )(p3, p3, p3, dec_lanes, part, p3, norm_w)


def _rglru_kernel(*refs, rev, nblk):
    if rev:
        (x_ref, xp_ref, xn_ref, cw_ref, cb_ref, wg_ref, bg_ref, lam_ref, part_ref, gate_ref,
         o_ref, carry_ref, a_ref, u_ref, h_ref) = refs
    else:
        (x_ref, xp_ref, xn_ref, cw_ref, cb_ref, wg_ref, bg_ref, lam_ref,
         o_ref, carry_ref, a_ref, u_ref, h_ref) = refs
    TB = RG_TB
    j = pl.program_id(1)
    blk = (nblk - 1 - j) if rev else j

    @pl.when(j == 0)
    def _():
        carry_ref[...] = jnp.zeros_like(carry_ref)

    prev = jnp.where(blk > 0, xp_ref[0].astype(F32), 0.0)
    nxt = jnp.where(blk < nblk - 1, xn_ref[0].astype(F32), 0.0)
    xe = jnp.concatenate([prev, x_ref[0].astype(F32), nxt], axis=0)
    cw = cw_ref[...]
    xc = cb_ref[...] + sum(
        xe[RG_HALO - CONV_PAD_L + w:RG_HALO - CONV_PAD_L + w + TB] * cw[w:w + 1] for w in range(CONV_W))
    gates = _dot(xc.astype(BF16), wg_ref[...]) + bg_ref[...]
    r = _sigmoid(gates[:, :GROUP_W])
    ig = _sigmoid(gates[:, GROUP_W:])
    log_a = -RG_C * r * _softplus(-lam_ref[...])
    a = jnp.exp(log_a)
    one_m_a2 = 1.0 - a * a
    root = jnp.where(one_m_a2 > 0.0, one_m_a2 * lax.rsqrt(one_m_a2), 0.0)
    u = root * (ig * xc)

    pos = lax.broadcasted_iota(jnp.int32, (TB, GROUP_W), 0) % SUBLANES
    for d in (1, 2, 4):
        if rev:
            keep = pos < SUBLANES - d
            shift = TB - d
        else:
            keep = pos >= d
            shift = d
        a_s = jnp.where(keep, pltpu.roll(a, shift, 0), 1.0)
        u_s = jnp.where(keep, pltpu.roll(u, shift, 0), 0.0)
        u = a * u_s + u
        a = a * a_s
    a_ref[...] = a
    u_ref[...] = u
    ntile = TB // SUBLANES

    def tile_step(i, hprev):
        ti = (ntile - 1 - i) if rev else i
        r0 = pl.multiple_of(ti * SUBLANES, SUBLANES)
        h = a_ref[pl.ds(r0, SUBLANES), :] * hprev + u_ref[pl.ds(r0, SUBLANES), :]
        h_ref[pl.ds(r0, SUBLANES), :] = h
        edge = h[0:1] if rev else h[SUBLANES - 1:SUBLANES]
        return jnp.broadcast_to(edge, (SUBLANES, GROUP_W))

    carry_ref[...] = lax.fori_loop(0, ntile, tile_step, carry_ref[...])
    if rev:
        hs = h_ref[...] + part_ref[0]
        o_ref[0] = (hs * jax.nn.gelu(gate_ref[0].astype(F32), approximate=True)).astype(BF16)
    else:
        o_ref[0] = h_ref[...]


def _rglru(p3, conv_w, conv_b, wg, bg, lam, batch, seq):
    TB = RG_TB
    nblk = seq // TB
    hpb = TB // RG_HALO
    nhalo = seq // RG_HALO
    scratch = [pltpu.VMEM((SUBLANES, GROUP_W), F32)] + [pltpu.VMEM((TB, GROUP_W), F32)] * 3

    def specs(rev):
        def blk(j):
            return (nblk - 1 - j) if rev else j
        return [
            pl.BlockSpec((1, TB, GROUP_W), lambda b, j: (b, blk(j), P_CX)),
            pl.BlockSpec((1, RG_HALO, GROUP_W), lambda b, j: (b, jnp.maximum(blk(j) * hpb - 1, 0), P_CX)),
            pl.BlockSpec((1, RG_HALO, GROUP_W),
                         lambda b, j: (b, jnp.minimum((blk(j) + 1) * hpb, nhalo - 1), P_CX)),
            _const_spec((CONV_W, GROUP_W)),
            _const_spec((1, GROUP_W)),
            _const_spec((GROUP_W, 2 * GROUP_W)),
            _const_spec((1, 2 * GROUP_W)),
            _const_spec((1, GROUP_W)),
        ]

    part = pl.pallas_call(
        functools.partial(_rglru_kernel, rev=False, nblk=nblk),
        grid=(batch, nblk),
        in_specs=specs(False),
        out_specs=pl.BlockSpec((1, TB, GROUP_W), lambda b, j: (b, j, 0)),
        out_shape=jax.ShapeDtypeStruct((batch, seq, GROUP_W), F32),
        scratch_shapes=scratch,
        compiler_params=_cparams(2),
        name="rglru_fw",
    )(p3, p3, p3, conv_w, conv_b, wg[0], bg[0], lam[0:1])
    return pl.pallas_call(
        functools.partial(_rglru_kernel, rev=True, nblk=nblk),
        grid=(batch, nblk),
        in_specs=specs(True) + [
            pl.BlockSpec((1, TB, GROUP_W), lambda b, j: (b, nblk - 1 - j, 0)),
            pl.BlockSpec((1, TB, GROUP_W), lambda b, j: (b, nblk - 1 - j, P_CG)),
        ],
        out_specs=pl.BlockSpec((1, TB, GROUP_W), lambda b, j: (b, nblk - 1 - j, 0)),
        out_shape=jax.ShapeDtypeStruct((batch, seq, GROUP_W), BF16),
        scratch_shapes=scratch,
        compiler_params=_cparams(2),
        name="rglru_bw",
    )(p3, p3, p3, conv_w, conv_b, wg[1], bg[1], lam[1:2], part, p3)


def _post_kernel(oa_ref, ob_ref, oc_ref, od_ref, r_ref, p_ref, nw_ref, wo_ref, wig_ref, wiu_ref, wfo_ref,
                 wpg_ref, wpp_ref, out_ref):
    nw = nw_ref[...]
    mixed_in = jnp.concatenate([oa_ref[...], ob_ref[...], oc_ref[...], od_ref[...]], axis=1)
    r = r_ref[...] + _rms(_dot(mixed_in, wo_ref[...]), nw[1:2])
    h2 = _rms(r, nw[2:3]).astype(BF16)
    ff = jnp.zeros_like(r)
    for c in range(N_FF_CHUNKS):
        act = _silu(_dot(h2, wig_ref[c])) * _dot(h2, wiu_ref[c])
        ff = ff + _dot(act.astype(BF16), wfo_ref[c])
    r = r + _rms(ff, nw[3:4])
    gate = _sigmoid(_dot(r.astype(BF16), wpg_ref[...]))
    out_ref[...] = r + gate * _dot(p_ref[0].astype(BF16), wpp_ref[...])


def _post(oa, ob, oc, od, r, p, layer, nw, wo, wig, wiu, wfo, wpg, wpp):
    tokens = r.shape[0]
    tm = TM_POST
    mix_spec = pl.BlockSpec((tm, GROUP_W), lambda i: (i, 0))
    return pl.pallas_call(
        _post_kernel,
        grid=(tokens // tm,),
        in_specs=[
            mix_spec, mix_spec, mix_spec, mix_spec,
            pl.BlockSpec((tm, D_MODEL), lambda i: (i, 0)),
            pl.BlockSpec((1, tm, PLE_DIM), lambda i: (layer, i, 0)),
            _const_spec((4, D_MODEL)),
            _const_spec((D_MODEL, D_MODEL)),
            _const_spec((N_FF_CHUNKS, D_MODEL, FF_CHUNK)),
            _const_spec((N_FF_CHUNKS, D_MODEL, FF_CHUNK)),
            _const_spec((N_FF_CHUNKS, FF_CHUNK, D_MODEL)),
            _const_spec((D_MODEL, D_MODEL)),
            _const_spec((PLE_DIM, D_MODEL)),
        ],
        out_specs=pl.BlockSpec((tm, D_MODEL), lambda i: (i, 0)),
        out_shape=jax.ShapeDtypeStruct((tokens, D_MODEL), F32),
        compiler_params=_cparams(1),
        name="post",
    )(oa, ob, oc, od, r, p, nw, wo, wig, wiu, wfo, wpg, wpp)


def _block_diag(w):
    out = jnp.zeros((GROUP_W, GROUP_W), w.dtype)
    for h in range(N_HEADS):
        out = out.at[h * HEAD_DIM:(h + 1) * HEAD_DIM, h * HEAD_DIM:(h + 1) * HEAD_DIM].set(w[h])
    return out


def _prep_weights(norm_w, w_in, diff_lambda, diff_subln_w, hgrn_lb_raw, hgrn_norm_w, rg_conv_w, rg_conv_b,
                  rg_w_a, rg_b_a, rg_w_x, rg_b_x, rg_lambda, ret_decay, ret_norm_w, w_out, w_ffn_in,
                  w_ffn_out, w_ple_gate, w_ple_proj):
    layers = []
    for l in range(DEPTH):
        wi = w_in[l]
        w_nat = jnp.concatenate([wi[:, GROUP_W:2 * GROUP_W], wi[:, 3 * GROUP_W:]], axis=1).astype(BF16)
        w_qv_t = jnp.concatenate([wi[:, :GROUP_W], wi[:, 2 * GROUP_W:3 * GROUP_W]], axis=1).T.astype(BF16)
        wg = [jnp.concatenate([_block_diag(rg_w_a[l, d]), _block_diag(rg_w_x[l, d])], axis=1).astype(BF16)
              for d in range(2)]
        bg = [jnp.concatenate([rg_b_a[l, d], rg_b_x[l, d]])[None, :] for d in range(2)]
        wfi = w_ffn_in[l].astype(BF16)
        layers.append(dict(
            nw=norm_w[l], nw0=norm_w[l, 0:1], w_nat=w_nat, w_qv_t=w_qv_t,
            dl=diff_lambda[l], sw=diff_subln_w[l][:, None],
            lb_raw=hgrn_lb_raw, hgrn_nw=jnp.tile(hgrn_norm_w[l], N_HEADS)[None, :],
            conv_w=rg_conv_w[l], conv_b=rg_conv_b[l][None, :], wg=wg, bg=bg, lam=rg_lambda[l],
            dec=jnp.repeat(ret_decay[l], HEAD_DIM, axis=-1), ret_nw=jnp.tile(ret_norm_w[l], N_HEADS)[None, :],
            wo=w_out[l].astype(BF16),
            wig=wfi[:, :D_FF].reshape(D_MODEL, N_FF_CHUNKS, FF_CHUNK).transpose(1, 0, 2),
            wiu=wfi[:, D_FF:].reshape(D_MODEL, N_FF_CHUNKS, FF_CHUNK).transpose(1, 0, 2),
            wfo=w_ffn_out[l].astype(BF16).reshape(N_FF_CHUNKS, FF_CHUNK, D_MODEL),
            wpg=w_ple_gate[l].astype(BF16), wpp=w_ple_proj[l].astype(BF16),
        ))
    return layers


def _trunk(x, p, layers):
    batch, seq, _ = x.shape
    tokens = batch * seq
    r = x.reshape(tokens, D_MODEL)
    p = p.reshape(DEPTH, tokens, PLE_DIM)
    for l, w in enumerate(layers):
        lambda_init = 0.8 - 0.6 * math.exp(-0.3 * l)
        pn, qt, vt = _inproj(r, w["nw0"], w["w_nat"], w["w_qv_t"], batch, seq)
        p3 = pn.reshape(batch, seq, N_PGROUPS * GROUP_W)
        oa = _attention(qt, p3, vt, w["dl"], w["sw"], batch, seq, lambda_init)
        ob = _hgrn(p3, w["lb_raw"], w["hgrn_nw"], batch, seq, l)
        oc = _rglru(p3, w["conv_w"], w["conv_b"], w["wg"], w["bg"], w["lam"], batch, seq)
        od = _retention(p3, w["dec"], w["ret_nw"], batch, seq)
        flat = lambda o: o.reshape(tokens, GROUP_W)
        r = _post(flat(oa), flat(ob), flat(oc), flat(od), r, p, l, w["nw"], w["wo"], w["wig"], w["wiu"],
                  w["wfo"], w["wpg"], w["wpp"])
    return r.reshape(batch, seq, D_MODEL)


def kernel(x_prompt, x_sample, p_prompt, p_sample, norm_w, w_in, diff_lambda, diff_subln_w, hgrn_lb_raw,
           hgrn_norm_w, rg_conv_w, rg_conv_b, rg_w_a, rg_b_a, rg_w_x, rg_b_x, rg_lambda, ret_decay,
           ret_norm_w, w_out, w_ffn_in, w_ffn_out, w_ple_gate, w_ple_proj):
    layers = _prep_weights(norm_w, w_in, diff_lambda, diff_subln_w, hgrn_lb_raw, hgrn_norm_w, rg_conv_w,
                           rg_conv_b, rg_w_a, rg_b_a, rg_w_x, rg_b_x, rg_lambda, ret_decay, ret_norm_w,
                           w_out, w_ffn_in, w_ffn_out, w_ple_gate, w_ple_proj)
    return (_trunk(x_prompt, p_prompt, layers), _trunk(x_sample, p_sample, layers))
```

```python
import functools
import math

import jax
import jax.numpy as jnp
from jax import lax
from jax.experimental import pallas as pl
from jax.experimental.pallas import tpu as pltpu

F32 = jnp.float32
BF16 = jnp.bfloat16

D_MODEL = 1024
DEPTH = 4
GROUP_W = 256
N_HEADS = 4
HEAD_DIM = 64
DIFF_HALF = 32
D_FF = 2816
FF_CHUNK = 256
N_FF_CHUNKS = D_FF // FF_CHUNK
PLE_DIM = 256
CONV_W = 4
CONV_PAD_L = 2
RG_C = 8.0
EPS = 1e-6
LOG2E = 1.4426950408889634

(P_AK, P_BQ, P_BF_FW, P_BF_BW, P_BI, P_BG, P_CX, P_CG, P_DQ, P_DK, P_DV, P_DG) = range(12)
N_PGROUPS = 12

VMEM_LIMIT_V7X = 56 * 1024 * 1024

TM_INPROJ = 512
TM_POST = 512
TQ_ATTN = 256
TK_ATTN = 256
ATTN_SKEW = 4
ATTN_SLOTS = 8
ATTN_UNROLL = 16
HGRN_C = 64
HGRN_NB = 4
SEQS_PER_STEP = 4
HGRN_MID = HGRN_C // 2
HGRN_SAFE_EXP = 80.0
RET_C = 256
RG_TB = 256
RG_HALO = 16
SUBLANES = 8


def _cparams(n_axes):
    return pltpu.CompilerParams(dimension_semantics=("arbitrary",) * n_axes,
                                vmem_limit_bytes=VMEM_LIMIT_V7X)


def _const_spec(shape):
    nd = len(shape)
    return pl.BlockSpec(shape, lambda *_: (0,) * nd, pipeline_mode=pl.Buffered(1))


def _rms(x, w):
    return x * lax.rsqrt(jnp.mean(x * x, axis=-1, keepdims=True) + EPS) * w


def _rms_heads(x, w, head_mean):
    sq = x * x
    hi = sq.astype(BF16)
    lo = (sq - hi.astype(F32)).astype(BF16)
    ms = _dot(hi, head_mean) + _dot(lo, head_mean)
    return x * lax.rsqrt(ms + EPS) * w


def _sigmoid(x):
    return jax.nn.sigmoid(x)


def _silu(x):
    return x * _sigmoid(x)


def _softplus(x):
    return jnp.maximum(x, 0.0) + jnp.log(1.0 + jnp.exp(-jnp.abs(x)))


def _dot(a, b):
    return jnp.dot(a, b, preferred_element_type=F32)


def _dot_nt(a, b):
    return lax.dot_general(a, b, (((1,), (1,)), ((), ())), preferred_element_type=F32)


def _dot_tn(a, b):
    return lax.dot_general(a, b, (((0,), (0,)), ((), ())), preferred_element_type=F32)


def _inproj_kernel(x_ref, nw_ref, wn_ref, wt_ref, p_ref, qt_ref, vt_ref):
    x = x_ref[...]
    y = _rms(x, nw_ref[...]).astype(BF16)
    for g in range(N_PGROUPS):
        cols = slice(g * GROUP_W, (g + 1) * GROUP_W)
        p_ref[:, cols] = _dot(y, wn_ref[:, cols]).astype(BF16)
    t = _dot_nt(wt_ref[...], y)
    qt_ref[0] = (t[:GROUP_W] * (DIFF_HALF ** -0.5 * LOG2E)).astype(BF16)
    vt_ref[0] = t[GROUP_W:].astype(BF16)


def _inproj(r, nw, w_nat, w_qv_t, batch, seq):
    tokens = batch * seq
    tm = TM_INPROJ
    nsb = seq // tm
    return pl.pallas_call(
        _inproj_kernel,
        grid=(tokens // tm,),
        in_specs=[
            pl.BlockSpec((tm, D_MODEL), lambda i: (i, 0)),
            _const_spec((1, D_MODEL)),
            _const_spec((D_MODEL, N_PGROUPS * GROUP_W)),
            _const_spec((2 * GROUP_W, D_MODEL)),
        ],
        out_specs=[
            pl.BlockSpec((tm, N_PGROUPS * GROUP_W), lambda i: (i, 0)),
            pl.BlockSpec((1, GROUP_W, tm), lambda i: (i // nsb, 0, i % nsb)),
            pl.BlockSpec((1, GROUP_W, tm), lambda i: (i // nsb, 0, i % nsb)),
        ],
        out_shape=[
            jax.ShapeDtypeStruct((tokens, N_PGROUPS * GROUP_W), BF16),
            jax.ShapeDtypeStruct((batch, GROUP_W, seq), BF16),
            jax.ShapeDtypeStruct((batch, GROUP_W, seq), BF16),
        ],
        compiler_params=_cparams(1),
        name="inproj",
    )(r, nw, w_nat, w_qv_t)


def _attn_kernel(qt_ref, k_ref, vt_ref, dl_ref, sw_ref, o_ref, w_ref, m_ref, l_ref, acc_ref, s_ref, t_ref,
                 *, seq, lambda_init):
    tq, tk = TQ_ATTN, TK_ATTN
    assert tq == tk
    q0 = pl.program_id(1) * tq
    nk = seq // tk
    n_chain = 2 * N_HEADS
    sub = lax.broadcasted_iota(jnp.int32, (2 * HEAD_DIM, tq), 0)
    for h in range(N_HEADS):
        half = h // 2
        qt_half = qt_ref[0, half * 2 * HEAD_DIM:(half + 1) * 2 * HEAD_DIM, :]
        for c in range(2):
            lo = (h % 2) * HEAD_DIM + c * DIFF_HALF
            w_ref[2 * h + c] = jnp.where((sub >= lo) & (sub < lo + DIFF_HALF), qt_half,
                                         jnp.zeros_like(qt_half))
    slopes2 = [2.0 ** (-8.0 * (h + 1) / N_HEADS) * LOG2E for h in range(N_HEADS)]

    @pl.when((pl.program_id(0) == 0) & (pl.program_id(1) == 0))
    def _():
        row = lax.broadcasted_iota(jnp.int32, (tk, tq), 0)
        col = lax.broadcasted_iota(jnp.int32, (tk, tq), 1)
        d = (row - col).astype(F32)
        for h in range(N_HEADS):
            t_ref[h, 0] = slopes2[h] * d
            t_ref[h, 1] = -slopes2[h] * jnp.abs(d)
            t_ref[h, 2] = -slopes2[h] * d

    def scores(jj, i):
        k0 = pl.multiple_of(jj * tk, tk)
        half = i // 4
        kt = k_ref[0, pl.ds(k0, tk), half * 2 * HEAD_DIM:(half + 1) * 2 * HEAD_DIM]
        s_ref[i % ATTN_SLOTS] = _dot(kt, w_ref[i])

    def consume(j, i, sel, off, online):
        h = i // 2
        k0 = pl.multiple_of(j * tk, tk)
        u = s_ref[i % ATTN_SLOTS] + t_ref[h, sel]
        vt = vt_ref[0, h * HEAD_DIM:(h + 1) * HEAD_DIM, pl.ds(k0, tk)]
        m = m_ref[i]
        if online:
            m_new = jnp.maximum(m, jnp.max(u, axis=0, keepdims=True) - off[h])
            p = jnp.exp2(u - (m_new + off[h]))
            alpha = jnp.exp2(m - m_new)
            acc_ref[i] = acc_ref[i] * alpha + _dot(vt, p.astype(BF16))
            l_ref[i] = l_ref[i] * alpha + jnp.sum(p, axis=0, keepdims=True)
            m_ref[i] = m_new
        else:
            p = jnp.exp2(u - (m + off[h]))
            acc_ref[i] = acc_ref[i] + _dot(vt, p.astype(BF16))
            l_ref[i] = l_ref[i] + jnp.sum(p, axis=0, keepdims=True)

    def key_sweep(unroll, online):
        assert nk % unroll == 0
        for i in range(ATTN_SKEW):
            scores(0, i)

        def body(jo, carry):
            qi = pl.program_id(1)
            for ju in range(unroll):
                j = jo * unroll + ju
                jn = jnp.minimum(j + 1, nk - 1)
                sel = (j >= qi).astype(jnp.int32) + (j > qi).astype(jnp.int32)
                gap = jnp.abs(q0 - j * tk).astype(F32)
                off = [slopes2[h] * gap for h in range(N_HEADS)]
                for i in range(n_chain):
                    a = i + ATTN_SKEW
                    if a < n_chain:
                        scores(j, a)
                    else:
                        scores(jn, a - n_chain)
                    consume(j, i, sel, off, online)
            return carry

        lax.fori_loop(0, nk // unroll, body, 0)

    for i in range(n_chain):
        h = i // 2
        half = i // 4
        kt = k_ref[0, pl.ds(pl.multiple_of(q0, tk), tk), half * 2 * HEAD_DIM:(half + 1) * 2 * HEAD_DIM]
        m_ref[i] = jnp.max(_dot(kt, w_ref[i]) + t_ref[h, 1], axis=0, keepdims=True)
    dl = dl_ref[...]
    lam = (jnp.exp(jnp.sum(dl[0:1] * dl[1:2], axis=1, keepdims=True))
           - jnp.exp(jnp.sum(dl[2:3] * dl[3:4], axis=1, keepdims=True)) + lambda_init)

    def finish():
        outs = []
        for h in range(N_HEADS):
            o = (acc_ref[2 * h] / l_ref[2 * h] - lam * (acc_ref[2 * h + 1] / l_ref[2 * h + 1]))
            ms = jnp.mean(o * o, axis=0, keepdims=True)
            outs.append(o * lax.rsqrt(ms + EPS) * sw_ref[...] * (1.0 - lambda_init))
        o_ref[0] = jnp.concatenate(outs, axis=0).T.astype(BF16)

    acc_ref[...] = jnp.zeros_like(acc_ref)
    l_ref[...] = jnp.zeros_like(l_ref)
    key_sweep(ATTN_UNROLL, online=False)
    poisoned = (jnp.sum(acc_ref[...] * 0.0) + jnp.sum(l_ref[...] * 0.0)) != 0.0
    finish()

    @pl.when(poisoned)
    def _():
        m_ref[...] = jnp.full(m_ref.shape, -1e30, F32)
        acc_ref[...] = jnp.zeros_like(acc_ref)
        l_ref[...] = jnp.zeros_like(l_ref)
        key_sweep(1, online=True)
        finish()


def _attention(qt, p3, vt, dl, sw, batch, seq, lambda_init):
    tq = TQ_ATTN
    n_chain = 2 * N_HEADS
    scratch = [
        pltpu.VMEM((n_chain, 2 * HEAD_DIM, tq), BF16),
        pltpu.VMEM((n_chain, 1, tq), F32),
        pltpu.VMEM((n_chain, 1, tq), F32),
        pltpu.VMEM((n_chain, HEAD_DIM, tq), F32),
        pltpu.VMEM((ATTN_SLOTS, TK_ATTN, tq), F32),
        pltpu.VMEM((N_HEADS, 3, TK_ATTN, tq), F32),
    ]
    return pl.pallas_call(
        functools.partial(_attn_kernel, seq=seq, lambda_init=lambda_init),
        grid=(batch, seq // tq),
        in_specs=[
            pl.BlockSpec((1, GROUP_W, tq), lambda b, i: (b, 0, i)),
            pl.BlockSpec((1, seq, GROUP_W), lambda b, i: (b, 0, P_AK), pipeline_mode=pl.Buffered(1)),
            pl.BlockSpec((1, GROUP_W, seq), lambda b, i: (b, 0, 0), pipeline_mode=pl.Buffered(1)),
            _const_spec((4, DIFF_HALF)),
            _const_spec((HEAD_DIM, 1)),
        ],
        out_specs=pl.BlockSpec((1, tq, GROUP_W), lambda b, i: (b, i, 0)),
        out_shape=jax.ShapeDtypeStruct((batch, seq, GROUP_W), BF16),
        scratch_shapes=scratch,
        compiler_params=_cparams(2),
        name="diff_attn",
    )(qt, p3, vt, dl, sw)


def _hgrn_kernel(*refs, layer, rev):
    if rev:
        (q_ref, f_ref, v_ref, lbraw_ref, part_ref, g_ref, nw_ref, o_ref,
         st_ref, qs_ref, ks_ref, fs_ref, vs_ref, os_ref) = refs
    else:
        (q_ref, f_ref, v_ref, lbraw_ref, o_ref,
         st_ref, qs_ref, ks_ref, fs_ref, vs_ref, os_ref) = refs
    C = HGRN_C
    T = HGRN_NB * C
    nseq = q_ref.shape[0]

    @pl.when(pl.program_id(1) == 0)
    def _():
        st_ref[...] = jnp.zeros_like(st_ref)

    raw = lbraw_ref[...]
    e = jnp.exp(raw - jnp.max(raw, axis=0, keepdims=True))
    soft = e / jnp.sum(e, axis=0, keepdims=True)
    lb = jnp.zeros((1, GROUP_W), F32)
    for i in range(1, layer + 1):
        lb = lb + soft[i:i + 1]

    row = lax.broadcasted_iota(jnp.int32, (T, T), 0)
    col = lax.broadcasted_iota(jnp.int32, (T, T), 1)
    same = (row // C) == (col // C)
    tri = same & ((row <= col) if rev else (row >= col))
    tri_b = jnp.where(tri, 1.0, 0.0).astype(BF16)
    head_mean = jnp.where(same, 1.0 / HEAD_DIM, 0.0).astype(BF16)

    def finish(s):
        if rev:
            o = os_ref[s] + part_ref[s]
            g = g_ref[s].astype(F32)
            o_ref[s] = (_rms_heads(o, nw_ref[...], head_mean) * _silu(g)).astype(BF16)
        else:
            o_ref[s] = os_ref[s]

    def gates(s):
        xf = f_ref[s].astype(F32)
        sg = _sigmoid(xf)
        fg = lb + (1.0 - lb) * sg
        kk = (1.0 - lb) * (1.0 - sg)
        return _silu(q_ref[s].astype(F32)), kk, fg, v_ref[s]

    def factorised():
        seqs = range(nseq)
        pre = []
        for s in seqs:
            q, kk, fg, v = gates(s)
            lf = jnp.log(fg)
            hi = lf.astype(BF16)
            lo = (lf - hi.astype(F32)).astype(BF16)
            cum = _dot(tri_b, hi) + _dot(tri_b, lo)
            pre.append((q, kk, v, cum))
        mid = []
        for s in seqs:
            q, kk, v, cum = pre[s]

            def chunk_rows(idx, cum=cum):
                return jnp.concatenate(
                    [jnp.broadcast_to(cum[c * C + idx:c * C + idx + 1], (C, GROUP_W)) for c in range(HGRN_NB)],
                    axis=0)

            cm = chunk_rows(HGRN_MID)
            last = chunk_rows(0 if rev else C - 1)
            worst = jnp.max(jnp.abs(cum - cm))
            qt = (q * jnp.exp(cum - cm)).astype(BF16)
            kt = (kk * jnp.exp(cm - cum)).astype(BF16)
            qi = (q * jnp.exp(cum)).astype(BF16)
            ki = (kk * jnp.exp(last - cum)).astype(BF16)
            mid.append((worst, qt, kt, qi, ki, jnp.exp(last), v))
        for h in range(N_HEADS):
            hs = slice(h * HEAD_DIM, (h + 1) * HEAD_DIM)
            for s in seqs:
                _, qt, kt, _, _, _, v = mid[s]
                a = jnp.where(tri, _dot_nt(qt[:, hs], kt[:, hs]), 0.0).astype(BF16)
                os_ref[s, :, hs] = _dot(a, v[:, hs])
        st_in = [st_ref[s] for s in seqs]
        st = list(st_in)
        for c in (reversed(range(HGRN_NB)) if rev else range(HGRN_NB)):
            rs = slice(c * C, (c + 1) * C)
            for s in seqs:
                _, _, _, qi, ki, dl, v = mid[s]
                os_ref[s, rs, :] = os_ref[s, rs, :] + _dot_nt(qi[rs], st[s].astype(BF16))
                st[s] = st[s] * dl[c * C:c * C + 1] + jnp.where(same, _dot_tn(v[rs], ki[rs]), 0.0)
        for s in seqs:
            st_ref[s] = st[s]
        return [(mid[s][0], st_in[s]) for s in seqs]

    def token_by_token(s, st_in):
        q, kk, fg, v = gates(s)
        st_ref[s] = st_in
        qs_ref[...] = q
        ks_ref[...] = kk
        fs_ref[...] = fg
        vs_ref[...] = v.astype(F32)

        def step(i, carry):
            t = (T - 1 - i) if rev else i
            k8 = jnp.broadcast_to(ks_ref[pl.ds(t, 1), :], (SUBLANES, GROUP_W))
            v8 = jnp.broadcast_to(vs_ref[pl.ds(t, 1), :], (SUBLANES, GROUP_W)) * (1.0 / SUBLANES)
            st = st_ref[s] * fs_ref[pl.ds(t, 1), :] + jnp.where(same, _dot_tn(v8, k8), 0.0)
            st_ref[s] = st
            q8 = jnp.broadcast_to(qs_ref[pl.ds(t, 1), :], (SUBLANES, GROUP_W))
            os_ref[s, pl.ds(t, 1), :] = _dot_nt(q8, st)[0:1]
            return carry

        lax.fori_loop(0, T, step, 0)

    done = factorised()
    for s in range(nseq):
        finish(s)
    for s in range(nseq):
        worst, st_in = done[s]

        @pl.when(jnp.logical_not(worst <= HGRN_SAFE_EXP))
        def _(s=s, st_in=st_in):
            token_by_token(s, st_in)
            finish(s)


def _hgrn(p3, lb_raw, norm_w, batch, seq, layer):
    C = HGRN_NB * HGRN_C
    assert C == GROUP_W
    nc = seq // C
    ns = SEQS_PER_STEP if batch % SEQS_PER_STEP == 0 else 1
    scratch = ([pltpu.VMEM((ns, GROUP_W, GROUP_W), F32)] + [pltpu.VMEM((C, GROUP_W), F32)] * 4
               + [pltpu.VMEM((ns, C, GROUP_W), F32)])

    def grp(g, rev):
        if rev:
            return pl.BlockSpec((ns, C, GROUP_W), lambda b, j: (b, nc - 1 - j, g))
        return pl.BlockSpec((ns, C, GROUP_W), lambda b, j: (b, j, g))

    part = pl.pallas_call(
        functools.partial(_hgrn_kernel, layer=layer, rev=False),
        grid=(batch // ns, nc),
        in_specs=[grp(P_BQ, False), grp(P_BF_FW, False), grp(P_BI, False), _const_spec((DEPTH, GROUP_W))],
        out_specs=pl.BlockSpec((ns, C, GROUP_W), lambda b, j: (b, j, 0)),
        out_shape=jax.ShapeDtypeStruct((batch, seq, GROUP_W), F32),
        scratch_shapes=scratch,
        compiler_params=_cparams(2),
        name="hgrn_fw",
    )(p3, p3, p3, lb_raw[:, 0])
    return pl.pallas_call(
        functools.partial(_hgrn_kernel, layer=layer, rev=True),
        grid=(batch // ns, nc),
        in_specs=[grp(P_BQ, True), grp(P_BF_BW, True), grp(P_BI, True), _const_spec((DEPTH, GROUP_W)),
                  pl.BlockSpec((ns, C, GROUP_W), lambda b, j: (b, nc - 1 - j, 0)),
                  grp(P_BG, True), _const_spec((1, GROUP_W))],
        out_specs=pl.BlockSpec((ns, C, GROUP_W), lambda b, j: (b, nc - 1 - j, 0)),
        out_shape=jax.ShapeDtypeStruct((batch, seq, GROUP_W), BF16),
        scratch_shapes=scratch,
        compiler_params=_cparams(2),
        name="hgrn_bw",
    )(p3, p3, p3, lb_raw[:, 1], part, p3, norm_w)


def _log_sigmoid(x):
    return jnp.minimum(x, 0.0) - jnp.log(1.0 + jnp.exp(-jnp.abs(x)))


def _ret_kernel(*refs, rev):
    if rev:
        q_ref, k_ref, v_ref, dec_ref, part_ref, g_ref, nw_ref, o_ref, st_ref, xi_ref, zeta_ref = refs
    else:
        q_ref, k_ref, v_ref, dec_ref, o_ref, st_ref, xi_ref, zeta_ref, dm_ref = refs
    C = RET_C

    @pl.when(pl.program_id(1) == 0)
    def _():
        st_ref[...] = jnp.zeros_like(st_ref)

    lg_f = _log_sigmoid(dec_ref[0:1, :])
    lg_b = _log_sigmoid(dec_ref[1:2, :])
    lg = lg_b if rev else lg_f
    if not rev:
        @pl.when((pl.program_id(0) == 0) & (pl.program_id(1) == 0))
        def _():
            row = lax.broadcasted_iota(jnp.int32, (C, C), 0)
            col = lax.broadcasted_iota(jnp.int32, (C, C), 1)
            d_ts = (row - col).astype(F32)
            for h in range(N_HEADS):
                lf1 = lg_f[:, h * HEAD_DIM:h * HEAD_DIM + 1]
                lb1 = lg_b[:, h * HEAD_DIM:h * HEAD_DIM + 1]
                dm_ref[h] = (jnp.where(d_ts >= 0, jnp.exp(jnp.maximum(d_ts, 0.0) * lf1), 0.0)
                             + jnp.where(d_ts <= 0, jnp.exp(jnp.maximum(-d_ts, 0.0) * lb1), 0.0))

    @pl.when((pl.program_id(0) == 0) & (pl.program_id(1) == 0))
    def _():
        t = lax.broadcasted_iota(jnp.int32, (C, 1), 0).astype(F32)
        if rev:
            xi_ref[...] = jnp.exp((C - t) * lg)
            zeta_ref[...] = jnp.exp(t * lg)
        else:
            xi_ref[...] = jnp.exp((t + 1.0) * lg)
            zeta_ref[...] = jnp.exp((C - 1.0 - t) * lg)

    seqs = range(q_ref.shape[0])
    cd = jnp.exp(C * lg)
    pre = []
    for s in seqs:
        q = q_ref[s].astype(F32)
        k = k_ref[s].astype(F32) * HEAD_DIM ** -0.5
        pre.append(((q * xi_ref[...]).astype(BF16), (k * zeta_ref[...]).astype(BF16),
                    q.astype(BF16), k.astype(BF16), v_ref[s]))
    outs = [[] for _ in seqs]
    for h in range(N_HEADS):
        hs = slice(h * HEAD_DIM, (h + 1) * HEAD_DIM)
        sc = [_dot_nt(pre[s][2][:, hs], pre[s][3][:, hs]) for s in seqs] if not rev else None
        st = [st_ref[s, h] for s in seqs]
        o = [_dot(pre[s][0][:, hs], st[s].astype(BF16)) for s in seqs]
        upd = [_dot_tn(pre[s][1][:, hs], pre[s][4][:, hs]) for s in seqs]
        for s in seqs:
            if not rev:
                o[s] = o[s] + _dot((sc[s] * dm_ref[h]).astype(BF16), pre[s][4][:, hs])
            st_ref[s, h] = st[s] * cd[:, h * HEAD_DIM:h * HEAD_DIM + 1] + upd[s]
            outs[s].append(o[s])
    if rev:
        row = lax.broadcasted_iota(jnp.int32, (GROUP_W, GROUP_W), 0)
        col = lax.broadcasted_iota(jnp.int32, (GROUP_W, GROUP_W), 1)
        head_mean = jnp.where(row // HEAD_DIM == col // HEAD_DIM, 1.0 / HEAD_DIM, 0.0).astype(BF16)
    for s in seqs:
        o = jnp.concatenate(outs[s], axis=1)
        if rev:
            g = g_ref[s].astype(F32)
            o_ref[s] = (_rms_heads(o + part_ref[s], nw_ref[...], head_mean) * _silu(g)).astype(BF16)
        else:
            o_ref[s] = o


def _retention(p3, dec_lanes, norm_w, batch, seq):
    C = RET_C
    nc = seq // C
    ns = SEQS_PER_STEP if batch % SEQS_PER_STEP == 0 else 1
    scratch = [pltpu.VMEM((ns, N_HEADS, HEAD_DIM, HEAD_DIM), F32)] + [pltpu.VMEM((C, GROUP_W), F32)] * 2

    def grp(g, rev):
        if rev:
            return pl.BlockSpec((ns, C, GROUP_W), lambda b, j: (b, nc - 1 - j, g))
        return pl.BlockSpec((ns, C, GROUP_W), lambda b, j: (b, j, g))

    part = pl.pallas_call(
        functools.partial(_ret_kernel, rev=False),
        grid=(batch // ns, nc),
        in_specs=[grp(P_DQ, False), grp(P_DK, False), grp(P_DV, False), _const_spec((2, GROUP_W))],
        out_specs=pl.BlockSpec((ns, C, GROUP_W), lambda b, j: (b, j, 0)),
        out_shape=jax.ShapeDtypeStruct((batch, seq, GROUP_W), F32),
        scratch_shapes=scratch + [pltpu.VMEM((N_HEADS, C, C), F32)],
        compiler_params=_cparams(2),
        name="ret_fw",
    )(p3, p3, p3, dec_lanes)
    return pl.pallas_call(
        functools.partial(_ret_kernel, rev=True),
        grid=(batch // ns, nc),
        in_specs=[grp(P_DQ, True), grp(P_DK, True), grp(P_DV, True), _const_spec((2, GROUP_W)),
                  pl.BlockSpec((ns, C, GROUP_W), lambda b, j: (b, nc - 1 - j, 0)),
                  grp(P_DG, True), _const_spec((1, GROUP_W))],
        out_specs=pl.BlockSpec((ns, C, GROUP_W), lambda b, j: (b, nc - 1 - j, 0)),
        out_shape=jax.ShapeDtypeStruct((batch, seq, GROUP_W), BF16),
        scratch_shapes=scratch,
        compiler_params=_cparams(2),
        name="ret_bw",
    )(p3, p3, p3, dec_lanes, part, p3, norm_w)


def _rglru_kernel(*refs, rev, nblk):
    if rev:
        (x_ref, xp_ref, xn_ref, cw_ref, cb_ref, wg_ref, bg_ref, lam_ref, part_ref, gate_ref,
         o_ref, carry_ref, a_ref, u_ref, h_ref) = refs
    else:
        (x_ref, xp_ref, xn_ref, cw_ref, cb_ref, wg_ref, bg_ref, lam_ref,
         o_ref, carry_ref, a_ref, u_ref, h_ref) = refs
    TB = RG_TB
    j = pl.program_id(1)
    blk = (nblk - 1 - j) if rev else j

    @pl.when(j == 0)
    def _():
        carry_ref[...] = jnp.zeros_like(carry_ref)

    prev = jnp.where(blk > 0, xp_ref[0].astype(F32), 0.0)
    nxt = jnp.where(blk < nblk - 1, xn_ref[0].astype(F32), 0.0)
    xe = jnp.concatenate([prev, x_ref[0].astype(F32), nxt], axis=0)
    cw = cw_ref[...]
    xc = cb_ref[...] + sum(
        xe[RG_HALO - CONV_PAD_L + w:RG_HALO - CONV_PAD_L + w + TB] * cw[w:w + 1] for w in range(CONV_W))
    gates = _dot(xc.astype(BF16), wg_ref[...]) + bg_ref[...]
    r = _sigmoid(gates[:, :GROUP_W])
    ig = _sigmoid(gates[:, GROUP_W:])
    log_a = -RG_C * r * _softplus(-lam_ref[...])
    a = jnp.exp(log_a)
    one_m_a2 = 1.0 - a * a
    root = jnp.where(one_m_a2 > 0.0, one_m_a2 * lax.rsqrt(one_m_a2), 0.0)
    u = root * (ig * xc)

    ntile = TB // SUBLANES
    a = a.reshape(ntile, SUBLANES, GROUP_W)
    u = u.reshape(ntile, SUBLANES, GROUP_W)
    pos = lax.broadcasted_iota(jnp.int32, (ntile, SUBLANES, GROUP_W), 1)
    for d in (1, 2, 4):
        if rev:
            keep = pos < SUBLANES - d
            shift = SUBLANES - d
        else:
            keep = pos >= d
            shift = d
        a_s = jnp.where(keep, pltpu.roll(a, shift, 1), 1.0)
        u_s = jnp.where(keep, pltpu.roll(u, shift, 1), 0.0)
        u = a * u_s + u
        a = a * a_s
    a_ref[...] = a.reshape(TB, GROUP_W)
    u_ref[...] = u.reshape(TB, GROUP_W)

    def tile_step(i, hprev):
        ti = (ntile - 1 - i) if rev else i
        r0 = pl.multiple_of(ti * SUBLANES, SUBLANES)
        h = a_ref[pl.ds(r0, SUBLANES), :] * hprev + u_ref[pl.ds(r0, SUBLANES), :]
        h_ref[pl.ds(r0, SUBLANES), :] = h
        edge = h[0:1] if rev else h[SUBLANES - 1:SUBLANES]
        return jnp.broadcast_to(edge, (SUBLANES, GROUP_W))

    carry_ref[...] = lax.fori_loop(0, ntile, tile_step, carry_ref[...])
    if rev:
        hs = h_ref[...] + part_ref[0]
        o_ref[0] = (hs * jax.nn.gelu(gate_ref[0].astype(F32), approximate=True)).astype(BF16)
    else:
        o_ref[0] = h_ref[...]


def _rglru(p3, conv_w, conv_b, wg, bg, lam, batch, seq):
    TB = RG_TB
    nblk = seq // TB
    hpb = TB // RG_HALO
    nhalo = seq // RG_HALO
    scratch = [pltpu.VMEM((SUBLANES, GROUP_W), F32)] + [pltpu.VMEM((TB, GROUP_W), F32)] * 3

    def specs(rev):
        def blk(j):
            return (nblk - 1 - j) if rev else j
        return [
            pl.BlockSpec((1, TB, GROUP_W), lambda b, j: (b, blk(j), P_CX)),
            pl.BlockSpec((1, RG_HALO, GROUP_W), lambda b, j: (b, jnp.maximum(blk(j) * hpb - 1, 0), P_CX)),
            pl.BlockSpec((1, RG_HALO, GROUP_W),
                         lambda b, j: (b, jnp.minimum((blk(j) + 1) * hpb, nhalo - 1), P_CX)),
            _const_spec((CONV_W, GROUP_W)),
            _const_spec((1, GROUP_W)),
            _const_spec((GROUP_W, 2 * GROUP_W)),
            _const_spec((1, 2 * GROUP_W)),
            _const_spec((1, GROUP_W)),
        ]

    part = pl.pallas_call(
        functools.partial(_rglru_kernel, rev=False, nblk=nblk),
        grid=(batch, nblk),
        in_specs=specs(False),
        out_specs=pl.BlockSpec((1, TB, GROUP_W), lambda b, j: (b, j, 0)),
        out_shape=jax.ShapeDtypeStruct((batch, seq, GROUP_W), F32),
        scratch_shapes=scratch,
        compiler_params=_cparams(2),
        name="rglru_fw",
    )(p3, p3, p3, conv_w, conv_b, wg[0], bg[0], lam[0:1])
    return pl.pallas_call(
        functools.partial(_rglru_kernel, rev=True, nblk=nblk),
        grid=(batch, nblk),
        in_specs=specs(True) + [
            pl.BlockSpec((1, TB, GROUP_W), lambda b, j: (b, nblk - 1 - j, 0)),
            pl.BlockSpec((1, TB, GROUP_W), lambda b, j: (b, nblk - 1 - j, P_CG)),
        ],
        out_specs=pl.BlockSpec((1, TB, GROUP_W), lambda b, j: (b, nblk - 1 - j, 0)),
        out_shape=jax.ShapeDtypeStruct((batch, seq, GROUP_W), BF16),
        scratch_shapes=scratch,
        compiler_params=_cparams(2),
        name="rglru_bw",
    )(p3, p3, p3, conv_w, conv_b, wg[1], bg[1], lam[1:2], part, p3)


def _post_kernel(oa_ref, ob_ref, oc_ref, od_ref, r_ref, p_ref, nw_ref, wo_ref, wig_ref, wiu_ref, wfo_ref,
                 wpg_ref, wpp_ref, out_ref):
    nw = nw_ref[...]
    mixed_in = jnp.concatenate([oa_ref[...], ob_ref[...], oc_ref[...], od_ref[...]], axis=1)
    r = r_ref[...] + _rms(_dot(mixed_in, wo_ref[...]), nw[1:2])
    h2 = _rms(r, nw[2:3]).astype(BF16)
    ff = jnp.zeros_like(r)
    for c in range(N_FF_CHUNKS):
        act = _silu(_dot(h2, wig_ref[c])) * _dot(h2, wiu_ref[c])
        ff = ff + _dot(act.astype(BF16), wfo_ref[c])
    r = r + _rms(ff, nw[3:4])
    gate = _sigmoid(_dot(r.astype(BF16), wpg_ref[...]))
    out_ref[...] = r + gate * _dot(p_ref[0].astype(BF16), wpp_ref[...])


def _post(oa, ob, oc, od, r, p, layer, nw, wo, wig, wiu, wfo, wpg, wpp):
    tokens = r.shape[0]
    tm = TM_POST
    mix_spec = pl.BlockSpec((tm, GROUP_W), lambda i: (i, 0))
    return pl.pallas_call(
        _post_kernel,
        grid=(tokens // tm,),
        in_specs=[
            mix_spec, mix_spec, mix_spec, mix_spec,
            pl.BlockSpec((tm, D_MODEL), lambda i: (i, 0)),
            pl.BlockSpec((1, tm, PLE_DIM), lambda i: (layer, i, 0)),
            _const_spec((4, D_MODEL)),
            _const_spec((D_MODEL, D_MODEL)),
            _const_spec((N_FF_CHUNKS, D_MODEL, FF_CHUNK)),
            _const_spec((N_FF_CHUNKS, D_MODEL, FF_CHUNK)),
            _const_spec((N_FF_CHUNKS, FF_CHUNK, D_MODEL)),
            _const_spec((D_MODEL, D_MODEL)),
            _const_spec((PLE_DIM, D_MODEL)),
        ],
        out_specs=pl.BlockSpec((tm, D_MODEL), lambda i: (i, 0)),
        out_shape=jax.ShapeDtypeStruct((tokens, D_MODEL), F32),
        compiler_params=_cparams(1),
        name="post",
    )(oa, ob, oc, od, r, p, nw, wo, wig, wiu, wfo, wpg, wpp)


def _block_diag(w):
    out = jnp.zeros((GROUP_W, GROUP_W), w.dtype)
    for h in range(N_HEADS):
        out = out.at[h * HEAD_DIM:(h + 1) * HEAD_DIM, h * HEAD_DIM:(h + 1) * HEAD_DIM].set(w[h])
    return out


def _prep_weights(norm_w, w_in, diff_lambda, diff_subln_w, hgrn_lb_raw, hgrn_norm_w, rg_conv_w, rg_conv_b,
                  rg_w_a, rg_b_a, rg_w_x, rg_b_x, rg_lambda, ret_decay, ret_norm_w, w_out, w_ffn_in,
                  w_ffn_out, w_ple_gate, w_ple_proj):
    layers = []
    for l in range(DEPTH):
        wi = w_in[l]
        w_nat = jnp.concatenate([wi[:, GROUP_W:2 * GROUP_W], wi[:, 3 * GROUP_W:]], axis=1).astype(BF16)
        w_qv_t = jnp.concatenate([wi[:, :GROUP_W], wi[:, 2 * GROUP_W:3 * GROUP_W]], axis=1).T.astype(BF16)
        wg = [jnp.concatenate([_block_diag(rg_w_a[l, d]), _block_diag(rg_w_x[l, d])], axis=1).astype(BF16)
              for d in range(2)]
        bg = [jnp.concatenate([rg_b_a[l, d], rg_b_x[l, d]])[None, :] for d in range(2)]
        wfi = w_ffn_in[l].astype(BF16)
        layers.append(dict(
            nw=norm_w[l], nw0=norm_w[l, 0:1], w_nat=w_nat, w_qv_t=w_qv_t,
            dl=diff_lambda[l], sw=diff_subln_w[l][:, None],
            lb_raw=hgrn_lb_raw, hgrn_nw=jnp.tile(hgrn_norm_w[l], N_HEADS)[None, :],
            conv_w=rg_conv_w[l], conv_b=rg_conv_b[l][None, :], wg=wg, bg=bg, lam=rg_lambda[l],
            dec=jnp.repeat(ret_decay[l], HEAD_DIM, axis=-1), ret_nw=jnp.tile(ret_norm_w[l], N_HEADS)[None, :],
            wo=w_out[l].astype(BF16),
            wig=wfi[:, :D_FF].reshape(D_MODEL, N_FF_CHUNKS, FF_CHUNK).transpose(1, 0, 2),
            wiu=wfi[:, D_FF:].reshape(D_MODEL, N_FF_CHUNKS, FF_CHUNK).transpose(1, 0, 2),
            wfo=w_ffn_out[l].astype(BF16).reshape(N_FF_CHUNKS, FF_CHUNK, D_MODEL),
            wpg=w_ple_gate[l].astype(BF16), wpp=w_ple_proj[l].astype(BF16),
        ))
    return layers


def _trunk(x, p, layers):
    batch, seq, _ = x.shape
    tokens = batch * seq
    r = x.reshape(tokens, D_MODEL)
    p = p.reshape(DEPTH, tokens, PLE_DIM)
    for l, w in enumerate(layers):
        lambda_init = 0.8 - 0.6 * math.exp(-0.3 * l)
        pn, qt, vt = _inproj(r, w["nw0"], w["w_nat"], w["w_qv_t"], batch, seq)
        p3 = pn.reshape(batch, seq, N_PGROUPS * GROUP_W)
        oa = _attention(qt, p3, vt, w["dl"], w["sw"], batch, seq, lambda_init)
        ob = _hgrn(p3, w["lb_raw"], w["hgrn_nw"], batch, seq, l)
        oc = _rglru(p3, w["conv_w"], w["conv_b"], w["wg"], w["bg"], w["lam"], batch, seq)
        od = _retention(p3, w["dec"], w["ret_nw"], batch, seq)
        flat = lambda o: o.reshape(tokens, GROUP_W)
        r = _post(flat(oa), flat(ob), flat(oc), flat(od), r, p, l, w["nw"], w["wo"], w["wig"], w["wiu"],
                  w["wfo"], w["wpg"], w["wpp"])
    return r.reshape(batch, seq, D_MODEL)


def kernel(x_prompt, x_sample, p_prompt, p_sample, norm_w, w_in, diff_lambda, diff_subln_w, hgrn_lb_raw,
           hgrn_norm_w, rg_conv_w, rg_conv_b, rg_w_a, rg_b_a, rg_w_x, rg_b_x, rg_lambda, ret_decay,
           ret_norm_w, w_out, w_ffn_in, w_ffn_out, w_ple_gate, w_ple_proj):
    layers = _prep_weights(norm_w, w_in, diff_lambda, diff_subln_w, hgrn_lb_raw, hgrn_norm_w, rg_conv_w,
                           rg_conv_b, rg_w_a, rg_b_a, rg_w_x, rg_b_x, rg_lambda, ret_decay, ret_norm_w,
                           w_out, w_ffn_in, w_ffn_out, w_ple_gate, w_ple_proj)
    return (_trunk(x_prompt, p_prompt, layers), _trunk(x_sample, p_sample, layers))
```

```python
import functools
import math

import jax
import jax.numpy as jnp
from jax import lax
from jax.experimental import pallas as pl
from jax.experimental.pallas import tpu as pltpu

F32 = jnp.float32
BF16 = jnp.bfloat16

D_MODEL = 1024
DEPTH = 4
GROUP_W = 256
N_HEADS = 4
HEAD_DIM = 64
DIFF_HALF = 32
D_FF = 2816
FF_CHUNK = 256
N_FF_CHUNKS = D_FF // FF_CHUNK
PLE_DIM = 256
CONV_W = 4
CONV_PAD_L = 2
RG_C = 8.0
EPS = 1e-6
LOG2E = 1.4426950408889634

(P_AK, P_BQ, P_BF_FW, P_BF_BW, P_BI, P_BG, P_CX, P_CG, P_DQ, P_DK, P_DV, P_DG) = range(12)
N_PGROUPS = 12

VMEM_LIMIT_V7X = 56 * 1024 * 1024

TM_INPROJ = 512
TM_POST = 512
TQ_ATTN = 256
TK_ATTN = 256
ATTN_SKEW = 4
ATTN_SLOTS = 8
ATTN_UNROLL = 16
HGRN_C = 64
HGRN_NB = 4
SEQS_PER_STEP = 4
HGRN_MID = HGRN_C // 2
HGRN_SAFE_EXP = 80.0
RET_C = 256
RG_TB = 256
RG_HALO = 16
SUBLANES = 8


def _cparams(n_axes):
    return pltpu.CompilerParams(dimension_semantics=("arbitrary",) * n_axes,
                                vmem_limit_bytes=VMEM_LIMIT_V7X)


def _const_spec(shape):
    nd = len(shape)
    return pl.BlockSpec(shape, lambda *_: (0,) * nd, pipeline_mode=pl.Buffered(1))


def _rms(x, w):
    return x * lax.rsqrt(jnp.mean(x * x, axis=-1, keepdims=True) + EPS) * w


def _rms_heads(x, w, head_mean):
    sq = x * x
    hi = sq.astype(BF16)
    lo = (sq - hi.astype(F32)).astype(BF16)
    ms = _dot(hi, head_mean) + _dot(lo, head_mean)
    return x * lax.rsqrt(ms + EPS) * w


def _sigmoid(x):
    return jax.nn.sigmoid(x)


def _silu(x):
    return x * _sigmoid(x)


def _softplus(x):
    return jnp.maximum(x, 0.0) + jnp.log(1.0 + jnp.exp(-jnp.abs(x)))


def _dot(a, b):
    return jnp.dot(a, b, preferred_element_type=F32)


def _dot_nt(a, b):
    return lax.dot_general(a, b, (((1,), (1,)), ((), ())), preferred_element_type=F32)


def _dot_tn(a, b):
    return lax.dot_general(a, b, (((0,), (0,)), ((), ())), preferred_element_type=F32)


def _inproj_kernel(x_ref, nw_ref, wn_ref, wt_ref, p_ref, qt_ref, vt_ref):
    x = x_ref[...]
    y = _rms(x, nw_ref[...]).astype(BF16)
    for g in range(N_PGROUPS):
        cols = slice(g * GROUP_W, (g + 1) * GROUP_W)
        p_ref[:, cols] = _dot(y, wn_ref[:, cols]).astype(BF16)
    t = _dot_nt(wt_ref[...], y)
    qt_ref[0] = (t[:GROUP_W] * (DIFF_HALF ** -0.5 * LOG2E)).astype(BF16)
    vt_ref[0] = t[GROUP_W:].astype(BF16)


def _inproj(r, nw, w_nat, w_qv_t, batch, seq):
    tokens = batch * seq
    tm = TM_INPROJ
    nsb = seq // tm
    return pl.pallas_call(
        _inproj_kernel,
        grid=(tokens // tm,),
        in_specs=[
            pl.BlockSpec((tm, D_MODEL), lambda i: (i, 0)),
            _const_spec((1, D_MODEL)),
            _const_spec((D_MODEL, N_PGROUPS * GROUP_W)),
            _const_spec((2 * GROUP_W, D_MODEL)),
        ],
        out_specs=[
            pl.BlockSpec((tm, N_PGROUPS * GROUP_W), lambda i: (i, 0)),
            pl.BlockSpec((1, GROUP_W, tm), lambda i: (i // nsb, 0, i % nsb)),
            pl.BlockSpec((1, GROUP_W, tm), lambda i: (i // nsb, 0, i % nsb)),
        ],
        out_shape=[
            jax.ShapeDtypeStruct((tokens, N_PGROUPS * GROUP_W), BF16),
            jax.ShapeDtypeStruct((batch, GROUP_W, seq), BF16),
            jax.ShapeDtypeStruct((batch, GROUP_W, seq), BF16),
        ],
        compiler_params=_cparams(1),
        name="inproj",
    )(r, nw, w_nat, w_qv_t)


def _attn_kernel(qt_ref, k_ref, vt_ref, dl_ref, sw_ref, o_ref, w_ref, m_ref, l_ref, acc_ref, s_ref, t_ref,
                 *, seq, lambda_init):
    tq, tk = TQ_ATTN, TK_ATTN
    assert tq == tk
    q0 = pl.program_id(1) * tq
    nk = seq // tk
    n_chain = 2 * N_HEADS
    sub = lax.broadcasted_iota(jnp.int32, (2 * HEAD_DIM, tq), 0)
    for h in range(N_HEADS):
        half = h // 2
        qt_half = qt_ref[0, half * 2 * HEAD_DIM:(half + 1) * 2 * HEAD_DIM, :]
        for c in range(2):
            lo = (h % 2) * HEAD_DIM + c * DIFF_HALF
            w_ref[2 * h + c] = jnp.where((sub >= lo) & (sub < lo + DIFF_HALF), qt_half,
                                         jnp.zeros_like(qt_half))
    slopes2 = [2.0 ** (-8.0 * (h + 1) / N_HEADS) * LOG2E for h in range(N_HEADS)]

    @pl.when((pl.program_id(0) == 0) & (pl.program_id(1) == 0))
    def _():
        row = lax.broadcasted_iota(jnp.int32, (tk, tq), 0)
        col = lax.broadcasted_iota(jnp.int32, (tk, tq), 1)
        d = (row - col).astype(F32)
        for h in range(N_HEADS):
            t_ref[h, 0] = slopes2[h] * d
            t_ref[h, 1] = -slopes2[h] * jnp.abs(d)
            t_ref[h, 2] = -slopes2[h] * d

    def scores(jj, i):
        k0 = pl.multiple_of(jj * tk, tk)
        half = i // 4
        kt = k_ref[0, pl.ds(k0, tk), half * 2 * HEAD_DIM:(half + 1) * 2 * HEAD_DIM]
        s_ref[i % ATTN_SLOTS] = _dot(kt, w_ref[i])

    def consume(j, i, sel, off, online):
        h = i // 2
        k0 = pl.multiple_of(j * tk, tk)
        u = s_ref[i % ATTN_SLOTS] + t_ref[h, sel]
        vt = vt_ref[0, h * HEAD_DIM:(h + 1) * HEAD_DIM, pl.ds(k0, tk)]
        m = m_ref[i]
        if online:
            m_new = jnp.maximum(m, jnp.max(u, axis=0, keepdims=True) - off[h])
            p = jnp.exp2(u - (m_new + off[h]))
            alpha = jnp.exp2(m - m_new)
            acc_ref[i] = acc_ref[i] * alpha + _dot(vt, p.astype(BF16))
            l_ref[i] = l_ref[i] * alpha + jnp.sum(p, axis=0, keepdims=True)
            m_ref[i] = m_new
        else:
            p = jnp.exp2(u - (m + off[h]))
            acc_ref[i] = acc_ref[i] + _dot(vt, p.astype(BF16))
            l_ref[i] = l_ref[i] + jnp.sum(p, axis=0, keepdims=True)

    def key_sweep(unroll, online):
        assert nk % unroll == 0
        for i in range(ATTN_SKEW):
            scores(0, i)

        def body(jo, carry):
            qi = pl.program_id(1)
            for ju in range(unroll):
                j = jo * unroll + ju
                jn = jnp.minimum(j + 1, nk - 1)
                sel = (j >= qi).astype(jnp.int32) + (j > qi).astype(jnp.int32)
                gap = jnp.abs(q0 - j * tk).astype(F32)
                off = [slopes2[h] * gap for h in range(N_HEADS)]
                for i in range(n_chain):
                    a = i + ATTN_SKEW
                    if a < n_chain:
                        scores(j, a)
                    else:
                        scores(jn, a - n_chain)
                    consume(j, i, sel, off, online)
            return carry

        lax.fori_loop(0, nk // unroll, body, 0)

    for i in range(n_chain):
        h = i // 2
        half = i // 4
        kt = k_ref[0, pl.ds(pl.multiple_of(q0, tk), tk), half * 2 * HEAD_DIM:(half + 1) * 2 * HEAD_DIM]
        m_ref[i] = jnp.max(_dot(kt, w_ref[i]) + t_ref[h, 1], axis=0, keepdims=True)
    dl = dl_ref[...]
    lam = (jnp.exp(jnp.sum(dl[0:1] * dl[1:2], axis=1, keepdims=True))
           - jnp.exp(jnp.sum(dl[2:3] * dl[3:4], axis=1, keepdims=True)) + lambda_init)

    def finish():
        outs = []
        for h in range(N_HEADS):
            o = (acc_ref[2 * h] / l_ref[2 * h] - lam * (acc_ref[2 * h + 1] / l_ref[2 * h + 1]))
            ms = jnp.mean(o * o, axis=0, keepdims=True)
            outs.append(o * lax.rsqrt(ms + EPS) * sw_ref[...] * (1.0 - lambda_init))
        o_ref[0] = jnp.concatenate(outs, axis=0).T.astype(BF16)

    acc_ref[...] = jnp.zeros_like(acc_ref)
    l_ref[...] = jnp.zeros_like(l_ref)
    key_sweep(ATTN_UNROLL, online=False)
    poisoned = (jnp.sum(acc_ref[...] * 0.0) + jnp.sum(l_ref[...] * 0.0)) != 0.0
    finish()

    @pl.when(poisoned)
    def _():
        m_ref[...] = jnp.full(m_ref.shape, -1e30, F32)
        acc_ref[...] = jnp.zeros_like(acc_ref)
        l_ref[...] = jnp.zeros_like(l_ref)
        key_sweep(1, online=True)
        finish()


def _attention(qt, p3, vt, dl, sw, batch, seq, lambda_init):
    tq = TQ_ATTN
    n_chain = 2 * N_HEADS
    scratch = [
        pltpu.VMEM((n_chain, 2 * HEAD_DIM, tq), BF16),
        pltpu.VMEM((n_chain, 1, tq), F32),
        pltpu.VMEM((n_chain, 1, tq), F32),
        pltpu.VMEM((n_chain, HEAD_DIM, tq), F32),
        pltpu.VMEM((ATTN_SLOTS, TK_ATTN, tq), F32),
        pltpu.VMEM((N_HEADS, 3, TK_ATTN, tq), F32),
    ]
    return pl.pallas_call(
        functools.partial(_attn_kernel, seq=seq, lambda_init=lambda_init),
        grid=(batch, seq // tq),
        in_specs=[
            pl.BlockSpec((1, GROUP_W, tq), lambda b, i: (b, 0, i)),
            pl.BlockSpec((1, seq, GROUP_W), lambda b, i: (b, 0, P_AK), pipeline_mode=pl.Buffered(1)),
            pl.BlockSpec((1, GROUP_W, seq), lambda b, i: (b, 0, 0), pipeline_mode=pl.Buffered(1)),
            _const_spec((4, DIFF_HALF)),
            _const_spec((HEAD_DIM, 1)),
        ],
        out_specs=pl.BlockSpec((1, tq, GROUP_W), lambda b, i: (b, i, 0)),
        out_shape=jax.ShapeDtypeStruct((batch, seq, GROUP_W), BF16),
        scratch_shapes=scratch,
        compiler_params=_cparams(2),
        name="diff_attn",
    )(qt, p3, vt, dl, sw)


def _hgrn_kernel(*refs, layer, rev):
    if rev:
        (q_ref, f_ref, v_ref, lbraw_ref, part_ref, g_ref, nw_ref, o_ref,
         st_ref, qs_ref, ks_ref, fs_ref, vs_ref, os_ref) = refs
    else:
        (q_ref, f_ref, v_ref, lbraw_ref, o_ref,
         st_ref, qs_ref, ks_ref, fs_ref, vs_ref, os_ref) = refs
    C = HGRN_C
    T = HGRN_NB * C
    nseq = q_ref.shape[0]

    @pl.when(pl.program_id(1) == 0)
    def _():
        st_ref[...] = jnp.zeros_like(st_ref)

    raw = lbraw_ref[...]
    e = jnp.exp(raw - jnp.max(raw, axis=0, keepdims=True))
    soft = e / jnp.sum(e, axis=0, keepdims=True)
    lb = jnp.zeros((1, GROUP_W), F32)
    for i in range(1, layer + 1):
        lb = lb + soft[i:i + 1]

    row = lax.broadcasted_iota(jnp.int32, (T, T), 0)
    col = lax.broadcasted_iota(jnp.int32, (T, T), 1)
    same = (row // C) == (col // C)
    tri = same & ((row <= col) if rev else (row >= col))
    tri_b = jnp.where(tri, 1.0, 0.0).astype(BF16)
    head_mean = jnp.where(same, 1.0 / HEAD_DIM, 0.0).astype(BF16)

    def finish(s):
        if rev:
            o = os_ref[s] + part_ref[s]
            g = g_ref[s].astype(F32)
            o_ref[s] = (_rms_heads(o, nw_ref[...], head_mean) * _silu(g)).astype(BF16)
        else:
            o_ref[s] = os_ref[s]

    def gates(s):
        xf = f_ref[s].astype(F32)
        sg = _sigmoid(xf)
        fg = lb + (1.0 - lb) * sg
        kk = (1.0 - lb) * (1.0 - sg)
        return _silu(q_ref[s].astype(F32)), kk, fg, v_ref[s]

    def factorised():
        seqs = range(nseq)
        pre = []
        for s in seqs:
            q, kk, fg, v = gates(s)
            lf = jnp.log(fg)
            hi = lf.astype(BF16)
            lo = (lf - hi.astype(F32)).astype(BF16)
            cum = _dot(tri_b, hi) + _dot(tri_b, lo)
            pre.append((q, kk, v, cum))
        mid = []
        for s in seqs:
            q, kk, v, cum = pre[s]

            def chunk_rows(idx, cum=cum):
                return jnp.concatenate(
                    [jnp.broadcast_to(cum[c * C + idx:c * C + idx + 1], (C, GROUP_W)) for c in range(HGRN_NB)],
                    axis=0)

            cm = chunk_rows(HGRN_MID)
            last = chunk_rows(0 if rev else C - 1)
            worst = jnp.max(jnp.abs(cum - cm))
            qt = (q * jnp.exp(cum - cm)).astype(BF16)
            kt = (kk * jnp.exp(cm - cum)).astype(BF16)
            qi = (q * jnp.exp(cum)).astype(BF16)
            ki = (kk * jnp.exp(last - cum)).astype(BF16)
            mid.append((worst, qt, kt, qi, ki, jnp.exp(last), v))
        for h in range(N_HEADS):
            hs = slice(h * HEAD_DIM, (h + 1) * HEAD_DIM)
            for s in seqs:
                _, qt, kt, _, _, _, v = mid[s]
                a = jnp.where(tri, _dot_nt(qt[:, hs], kt[:, hs]), 0.0).astype(BF16)
                os_ref[s, :, hs] = _dot(a, v[:, hs])
        st_in = [st_ref[s] for s in seqs]
        st = list(st_in)
        for c in (reversed(range(HGRN_NB)) if rev else range(HGRN_NB)):
            rs = slice(c * C, (c + 1) * C)
            for s in seqs:
                _, _, _, qi, ki, dl, v = mid[s]
                os_ref[s, rs, :] = os_ref[s, rs, :] + _dot_nt(qi[rs], st[s].astype(BF16))
                st[s] = st[s] * dl[c * C:c * C + 1] + jnp.where(same, _dot_tn(v[rs], ki[rs]), 0.0)
        for s in seqs:
            st_ref[s] = st[s]
        return [(mid[s][0], st_in[s]) for s in seqs]

    def token_by_token(s, st_in):
        q, kk, fg, v = gates(s)
        st_ref[s] = st_in
        qs_ref[...] = q
        ks_ref[...] = kk
        fs_ref[...] = fg
        vs_ref[...] = v.astype(F32)

        def step(i, carry):
            t = (T - 1 - i) if rev else i
            k8 = jnp.broadcast_to(ks_ref[pl.ds(t, 1), :], (SUBLANES, GROUP_W))
            v8 = jnp.broadcast_to(vs_ref[pl.ds(t, 1), :], (SUBLANES, GROUP_W)) * (1.0 / SUBLANES)
            st = st_ref[s] * fs_ref[pl.ds(t, 1), :] + jnp.where(same, _dot_tn(v8, k8), 0.0)
            st_ref[s] = st
            q8 = jnp.broadcast_to(qs_ref[pl.ds(t, 1), :], (SUBLANES, GROUP_W))
            os_ref[s, pl.ds(t, 1), :] = _dot_nt(q8, st)[0:1]
            return carry

        lax.fori_loop(0, T, step, 0)

    done = factorised()
    for s in range(nseq):
        finish(s)
    for s in range(nseq):
        worst, st_in = done[s]

        @pl.when(jnp.logical_not(worst <= HGRN_SAFE_EXP))
        def _(s=s, st_in=st_in):
            token_by_token(s, st_in)
            finish(s)


def _hgrn(p3, lb_raw, norm_w, batch, seq, layer):
    C = HGRN_NB * HGRN_C
    assert C == GROUP_W
    nc = seq // C
    ns = SEQS_PER_STEP if batch % SEQS_PER_STEP == 0 else 1
    scratch = ([pltpu.VMEM((ns, GROUP_W, GROUP_W), F32)] + [pltpu.VMEM((C, GROUP_W), F32)] * 4
               + [pltpu.VMEM((ns, C, GROUP_W), F32)])

    def grp(g, rev):
        if rev:
            return pl.BlockSpec((ns, C, GROUP_W), lambda b, j: (b, nc - 1 - j, g))
        return pl.BlockSpec((ns, C, GROUP_W), lambda b, j: (b, j, g))

    part = pl.pallas_call(
        functools.partial(_hgrn_kernel, layer=layer, rev=False),
        grid=(batch // ns, nc),
        in_specs=[grp(P_BQ, False), grp(P_BF_FW, False), grp(P_BI, False), _const_spec((DEPTH, GROUP_W))],
        out_specs=pl.BlockSpec((ns, C, GROUP_W), lambda b, j: (b, j, 0)),
        out_shape=jax.ShapeDtypeStruct((batch, seq, GROUP_W), F32),
        scratch_shapes=scratch,
        compiler_params=_cparams(2),
        name="hgrn_fw",
    )(p3, p3, p3, lb_raw[:, 0])
    return pl.pallas_call(
        functools.partial(_hgrn_kernel, layer=layer, rev=True),
        grid=(batch // ns, nc),
        in_specs=[grp(P_BQ, True), grp(P_BF_BW, True), grp(P_BI, True), _const_spec((DEPTH, GROUP_W)),
                  pl.BlockSpec((ns, C, GROUP_W), lambda b, j: (b, nc - 1 - j, 0)),
                  grp(P_BG, True), _const_spec((1, GROUP_W))],
        out_specs=pl.BlockSpec((ns, C, GROUP_W), lambda b, j: (b, nc - 1 - j, 0)),
        out_shape=jax.ShapeDtypeStruct((batch, seq, GROUP_W), BF16),
        scratch_shapes=scratch,
        compiler_params=_cparams(2),
        name="hgrn_bw",
    )(p3, p3, p3, lb_raw[:, 1], part, p3, norm_w)


def _log_sigmoid(x):
    return jnp.minimum(x, 0.0) - jnp.log(1.0 + jnp.exp(-jnp.abs(x)))


def _ret_kernel(*refs, rev):
    if rev:
        q_ref, k_ref, v_ref, dec_ref, part_ref, g_ref, nw_ref, o_ref, st_ref, xi_ref, zeta_ref = refs
    else:
        q_ref, k_ref, v_ref, dec_ref, o_ref, st_ref, xi_ref, zeta_ref, dm_ref = refs
    C = RET_C

    @pl.when(pl.program_id(1) == 0)
    def _():
        st_ref[...] = jnp.zeros_like(st_ref)

    lg_f = _log_sigmoid(dec_ref[0:1, :])
    lg_b = _log_sigmoid(dec_ref[1:2, :])
    lg = lg_b if rev else lg_f
    if not rev:
        @pl.when((pl.program_id(0) == 0) & (pl.program_id(1) == 0))
        def _():
            row = lax.broadcasted_iota(jnp.int32, (C, C), 0)
            col = lax.broadcasted_iota(jnp.int32, (C, C), 1)
            d_ts = (row - col).astype(F32)
            for h in range(N_HEADS):
                lf1 = lg_f[:, h * HEAD_DIM:h * HEAD_DIM + 1]
                lb1 = lg_b[:, h * HEAD_DIM:h * HEAD_DIM + 1]
                dm_ref[h] = (jnp.where(d_ts >= 0, jnp.exp(jnp.maximum(d_ts, 0.0) * lf1), 0.0)
                             + jnp.where(d_ts <= 0, jnp.exp(jnp.maximum(-d_ts, 0.0) * lb1), 0.0))

    @pl.when((pl.program_id(0) == 0) & (pl.program_id(1) == 0))
    def _():
        t = lax.broadcasted_iota(jnp.int32, (C, 1), 0).astype(F32)
        if rev:
            xi_ref[...] = jnp.exp((C - t) * lg)
            zeta_ref[...] = jnp.exp(t * lg)
        else:
            xi_ref[...] = jnp.exp((t + 1.0) * lg)
            zeta_ref[...] = jnp.exp((C - 1.0 - t) * lg)

    seqs = range(q_ref.shape[0])
    cd = jnp.exp(C * lg)
    pre = []
    for s in seqs:
        q = q_ref[s].astype(F32)
        k = k_ref[s].astype(F32) * HEAD_DIM ** -0.5
        pre.append(((q * xi_ref[...]).astype(BF16), (k * zeta_ref[...]).astype(BF16),
                    q.astype(BF16), k.astype(BF16), v_ref[s]))
    outs = [[] for _ in seqs]
    for h in range(N_HEADS):
        hs = slice(h * HEAD_DIM, (h + 1) * HEAD_DIM)
        sc = [_dot_nt(pre[s][2][:, hs], pre[s][3][:, hs]) for s in seqs] if not rev else None
        st = [st_ref[s, h] for s in seqs]
        o = [_dot(pre[s][0][:, hs], st[s].astype(BF16)) for s in seqs]
        upd = [_dot_tn(pre[s][1][:, hs], pre[s][4][:, hs]) for s in seqs]
        for s in seqs:
            if not rev:
                o[s] = o[s] + _dot((sc[s] * dm_ref[h]).astype(BF16), pre[s][4][:, hs])
            st_ref[s, h] = st[s] * cd[:, h * HEAD_DIM:h * HEAD_DIM + 1] + upd[s]
            outs[s].append(o[s])
    if rev:
        row = lax.broadcasted_iota(jnp.int32, (GROUP_W, GROUP_W), 0)
        col = lax.broadcasted_iota(jnp.int32, (GROUP_W, GROUP_W), 1)
        head_mean = jnp.where(row // HEAD_DIM == col // HEAD_DIM, 1.0 / HEAD_DIM, 0.0).astype(BF16)
    for s in seqs:
        o = jnp.concatenate(outs[s], axis=1)
        if rev:
            g = g_ref[s].astype(F32)
            o_ref[s] = (_rms_heads(o + part_ref[s], nw_ref[...], head_mean) * _silu(g)).astype(BF16)
        else:
            o_ref[s] = o


def _retention(p3, dec_lanes, norm_w, batch, seq):
    C = RET_C
    nc = seq // C
    ns = SEQS_PER_STEP if batch % SEQS_PER_STEP == 0 else 1
    scratch = [pltpu.VMEM((ns, N_HEADS, HEAD_DIM, HEAD_DIM), F32)] + [pltpu.VMEM((C, GROUP_W), F32)] * 2

    def grp(g, rev):
        if rev:
            return pl.BlockSpec((ns, C, GROUP_W), lambda b, j: (b, nc - 1 - j, g))
        return pl.BlockSpec((ns, C, GROUP_W), lambda b, j: (b, j, g))

    part = pl.pallas_call(
        functools.partial(_ret_kernel, rev=False),
        grid=(batch // ns, nc),
        in_specs=[grp(P_DQ, False), grp(P_DK, False), grp(P_DV, False), _const_spec((2, GROUP_W))],
        out_specs=pl.BlockSpec((ns, C, GROUP_W), lambda b, j: (b, j, 0)),
        out_shape=jax.ShapeDtypeStruct((batch, seq, GROUP_W), F32),
        scratch_shapes=scratch + [pltpu.VMEM((N_HEADS, C, C), F32)],
        compiler_params=_cparams(2),
        name="ret_fw",
    )(p3, p3, p3, dec_lanes)
    return pl.pallas_call(
        functools.partial(_ret_kernel, rev=True),
        grid=(batch // ns, nc),
        in_specs=[grp(P_DQ, True), grp(P_DK, True), grp(P_DV, True), _const_spec((2, GROUP_W)),
                  pl.BlockSpec((ns, C, GROUP_W), lambda b, j: (b, nc - 1 - j, 0)),
                  grp(P_DG, True), _const_spec((1, GROUP_W))],
        out_specs=pl.BlockSpec((ns, C, GROUP_W), lambda b, j: (b, nc - 1 - j, 0)),
        out_shape=jax.ShapeDtypeStruct((batch, seq, GROUP_W), BF16),
        scratch_shapes=scratch,
        compiler_params=_cparams(2),
        name="ret_bw",
    )(p3, p3, p3, dec_lanes, part, p3, norm_w)


def _rglru_kernel(*refs, rev, nblk):
    if rev:
        (x_ref, xp_ref, xn_ref, cw_ref, cb_ref, wg_ref, bg_ref, lam_ref, part_ref, gate_ref,
         o_ref, carry_ref, a_ref, u_ref, h_ref) = refs
    else:
        (x_ref, xp_ref, xn_ref, cw_ref, cb_ref, wg_ref, bg_ref, lam_ref,
         o_ref, carry_ref, a_ref, u_ref, h_ref) = refs
    TB = RG_TB
    j = pl.program_id(1)
    blk = (nblk - 1 - j) if rev else j

    @pl.when(j == 0)
    def _():
        carry_ref[...] = jnp.zeros_like(carry_ref)

    prev = jnp.where(blk > 0, xp_ref[0].astype(F32), 0.0)
    nxt = jnp.where(blk < nblk - 1, xn_ref[0].astype(F32), 0.0)
    xe = jnp.concatenate([prev, x_ref[0].astype(F32), nxt], axis=0)
    cw = cw_ref[...]
    xc = cb_ref[...] + sum(
        xe[RG_HALO - CONV_PAD_L + w:RG_HALO - CONV_PAD_L + w + TB] * cw[w:w + 1] for w in range(CONV_W))
    gates = _dot(xc.astype(BF16), wg_ref[...]) + bg_ref[...]
    r = _sigmoid(gates[:, :GROUP_W])
    ig = _sigmoid(gates[:, GROUP_W:])
    log_a = -RG_C * r * _softplus(-lam_ref[...])
    a = jnp.exp(log_a)
    one_m_a2 = 1.0 - a * a
    root = jnp.where(one_m_a2 > 0.0, one_m_a2 * lax.rsqrt(one_m_a2), 0.0)
    u = root * (ig * xc)

    ntile = TB // SUBLANES
    a = a.reshape(ntile, SUBLANES, GROUP_W)
    u = u.reshape(ntile, SUBLANES, GROUP_W)
    pos = lax.broadcasted_iota(jnp.int32, (ntile, SUBLANES, GROUP_W), 1)
    for d in (1, 2, 4):
        if rev:
            keep = pos < SUBLANES - d
            shift = SUBLANES - d
        else:
            keep = pos >= d
            shift = d
        a_s = jnp.where(keep, pltpu.roll(a, shift, 1), 1.0)
        u_s = jnp.where(keep, pltpu.roll(u, shift, 1), 0.0)
        u = a * u_s + u
        a = a * a_s
    a_ref[...] = a.reshape(TB, GROUP_W)
    u_ref[...] = u.reshape(TB, GROUP_W)

    def tile_step(i, hprev):
        ti = (ntile - 1 - i) if rev else i
        r0 = pl.multiple_of(ti * SUBLANES, SUBLANES)
        h = a_ref[pl.ds(r0, SUBLANES), :] * hprev + u_ref[pl.ds(r0, SUBLANES), :]
        h_ref[pl.ds(r0, SUBLANES), :] = h
        edge = h[0:1] if rev else h[SUBLANES - 1:SUBLANES]
        return jnp.broadcast_to(edge, (SUBLANES, GROUP_W))

    carry_ref[...] = lax.fori_loop(0, ntile, tile_step, carry_ref[...])
    if rev:
        hs = h_ref[...] + part_ref[0]
        o_ref[0] = (hs * jax.nn.gelu(gate_ref[0].astype(F32), approximate=True)).astype(BF16)
    else:
        o_ref[0] = h_ref[...]


def _rglru(p3, conv_w, conv_b, wg, bg, lam, batch, seq):
    TB = RG_TB
    nblk = seq // TB
    hpb = TB // RG_HALO
    nhalo = seq // RG_HALO
    scratch = [pltpu.VMEM((SUBLANES, GROUP_W), F32)] + [pltpu.VMEM((TB, GROUP_W), F32)] * 3

    def specs(rev):
        def blk(j):
            return (nblk - 1 - j) if rev else j
        return [
            pl.BlockSpec((1, TB, GROUP_W), lambda b, j: (b, blk(j), P_CX)),
            pl.BlockSpec((1, RG_HALO, GROUP_W), lambda b, j: (b, jnp.maximum(blk(j) * hpb - 1, 0), P_CX)),
            pl.BlockSpec((1, RG_HALO, GROUP_W),
                         lambda b, j: (b, jnp.minimum((blk(j) + 1) * hpb, nhalo - 1), P_CX)),
            _const_spec((CONV_W, GROUP_W)),
            _const_spec((1, GROUP_W)),
            _const_spec((GROUP_W, 2 * GROUP_W)),
            _const_spec((1, 2 * GROUP_W)),
            _const_spec((1, GROUP_W)),
        ]

    part = pl.pallas_call(
        functools.partial(_rglru_kernel, rev=False, nblk=nblk),
        grid=(batch, nblk),
        in_specs=specs(False),
        out_specs=pl.BlockSpec((1, TB, GROUP_W), lambda b, j: (b, j, 0)),
        out_shape=jax.ShapeDtypeStruct((batch, seq, GROUP_W), F32),
        scratch_shapes=scratch,
        compiler_params=_cparams(2),
        name="rglru_fw",
    )(p3, p3, p3, conv_w, conv_b, wg[0], bg[0], lam[0:1])
    return pl.pallas_call(
        functools.partial(_rglru_kernel, rev=True, nblk=nblk),
        grid=(batch, nblk),
        in_specs=specs(True) + [
            pl.BlockSpec((1, TB, GROUP_W), lambda b, j: (b, nblk - 1 - j, 0)),
            pl.BlockSpec((1, TB, GROUP_W), lambda b, j: (b, nblk - 1 - j, P_CG)),
        ],
        out_specs=pl.BlockSpec((1, TB, GROUP_W), lambda b, j: (b, nblk - 1 - j, 0)),
        out_shape=jax.ShapeDtypeStruct((batch, seq, GROUP_W), BF16),
        scratch_shapes=scratch,
        compiler_params=_cparams(2),
        name="rglru_bw",
    )(p3, p3, p3, conv_w, conv_b, wg[1], bg[1], lam[1:2], part, p3)


def _post_kernel(oa_ref, ob_ref, oc_ref, od_ref, r_ref, p_ref, nw_ref, wo_ref, wig_ref, wiu_ref, wfo_ref,
                 wpg_ref, wpp_ref, out_ref, act_ref):
    nw = nw_ref[...]
    mixed_in = jnp.concatenate([oa_ref[...], ob_ref[...], oc_ref[...], od_ref[...]], axis=1)
    r = r_ref[...] + _rms(_dot(mixed_in, wo_ref[...]), nw[1:2])
    h2 = _rms(r, nw[2:3]).astype(BF16)
    for c in range(N_FF_CHUNKS):
        act = _silu(_dot(h2, wig_ref[c])) * _dot(h2, wiu_ref[c])
        act_ref[:, c * FF_CHUNK:(c + 1) * FF_CHUNK] = act.astype(BF16)
    r = r + _rms(_dot(act_ref[...], wfo_ref[...]), nw[3:4])
    gate = _sigmoid(_dot(r.astype(BF16), wpg_ref[...]))
    out_ref[...] = r + gate * _dot(p_ref[0].astype(BF16), wpp_ref[...])


def _post(oa, ob, oc, od, r, p, layer, nw, wo, wig, wiu, wfo, wpg, wpp):
    tokens = r.shape[0]
    tm = TM_POST
    mix_spec = pl.BlockSpec((tm, GROUP_W), lambda i: (i, 0))
    return pl.pallas_call(
        _post_kernel,
        grid=(tokens // tm,),
        in_specs=[
            mix_spec, mix_spec, mix_spec, mix_spec,
            pl.BlockSpec((tm, D_MODEL), lambda i: (i, 0)),
            pl.BlockSpec((1, tm, PLE_DIM), lambda i: (layer, i, 0)),
            _const_spec((4, D_MODEL)),
            _const_spec((D_MODEL, D_MODEL)),
            _const_spec((N_FF_CHUNKS, D_MODEL, FF_CHUNK)),
            _const_spec((N_FF_CHUNKS, D_MODEL, FF_CHUNK)),
            _const_spec((D_FF, D_MODEL)),
            _const_spec((D_MODEL, D_MODEL)),
            _const_spec((PLE_DIM, D_MODEL)),
        ],
        out_specs=pl.BlockSpec((tm, D_MODEL), lambda i: (i, 0)),
        out_shape=jax.ShapeDtypeStruct((tokens, D_MODEL), F32),
        scratch_shapes=[pltpu.VMEM((tm, D_FF), BF16)],
        compiler_params=_cparams(1),
        name="post",
    )(oa, ob, oc, od, r, p, nw, wo, wig, wiu, wfo, wpg, wpp)


def _block_diag(w):
    out = jnp.zeros((GROUP_W, GROUP_W), w.dtype)
    for h in range(N_HEADS):
        out = out.at[h * HEAD_DIM:(h + 1) * HEAD_DIM, h * HEAD_DIM:(h + 1) * HEAD_DIM].set(w[h])
    return out


def _prep_weights(norm_w, w_in, diff_lambda, diff_subln_w, hgrn_lb_raw, hgrn_norm_w, rg_conv_w, rg_conv_b,
                  rg_w_a, rg_b_a, rg_w_x, rg_b_x, rg_lambda, ret_decay, ret_norm_w, w_out, w_ffn_in,
                  w_ffn_out, w_ple_gate, w_ple_proj):
    layers = []
    for l in range(DEPTH):
        wi = w_in[l]
        w_nat = jnp.concatenate([wi[:, GROUP_W:2 * GROUP_W], wi[:, 3 * GROUP_W:]], axis=1).astype(BF16)
        w_qv_t = jnp.concatenate([wi[:, :GROUP_W], wi[:, 2 * GROUP_W:3 * GROUP_W]], axis=1).T.astype(BF16)
        wg = [jnp.concatenate([_block_diag(rg_w_a[l, d]), _block_diag(rg_w_x[l, d])], axis=1).astype(BF16)
              for d in range(2)]
        bg = [jnp.concatenate([rg_b_a[l, d], rg_b_x[l, d]])[None, :] for d in range(2)]
        wfi = w_ffn_in[l].astype(BF16)
        layers.append(dict(
            nw=norm_w[l], nw0=norm_w[l, 0:1], w_nat=w_nat, w_qv_t=w_qv_t,
            dl=diff_lambda[l], sw=diff_subln_w[l][:, None],
            lb_raw=hgrn_lb_raw, hgrn_nw=jnp.tile(hgrn_norm_w[l], N_HEADS)[None, :],
            conv_w=rg_conv_w[l], conv_b=rg_conv_b[l][None, :], wg=wg, bg=bg, lam=rg_lambda[l],
            dec=jnp.repeat(ret_decay[l], HEAD_DIM, axis=-1), ret_nw=jnp.tile(ret_norm_w[l], N_HEADS)[None, :],
            wo=w_out[l].astype(BF16),
            wig=wfi[:, :D_FF].reshape(D_MODEL, N_FF_CHUNKS, FF_CHUNK).transpose(1, 0, 2),
            wiu=wfi[:, D_FF:].reshape(D_MODEL, N_FF_CHUNKS, FF_CHUNK).transpose(1, 0, 2),
            wfo=w_ffn_out[l].astype(BF16),
            wpg=w_ple_gate[l].astype(BF16), wpp=w_ple_proj[l].astype(BF16),
        ))
    return layers


def _trunk(x, p, layers):
    batch, seq, _ = x.shape
    tokens = batch * seq
    r = x.reshape(tokens, D_MODEL)
    p = p.reshape(DEPTH, tokens, PLE_DIM)
    for l, w in enumerate(layers):
        lambda_init = 0.8 - 0.6 * math.exp(-0.3 * l)
        pn, qt, vt = _inproj(r, w["nw0"], w["w_nat"], w["w_qv_t"], batch, seq)
        p3 = pn.reshape(batch, seq, N_PGROUPS * GROUP_W)
        oa = _attention(qt, p3, vt, w["dl"], w["sw"], batch, seq, lambda_init)
        ob = _hgrn(p3, w["lb_raw"], w["hgrn_nw"], batch, seq, l)
        oc = _rglru(p3, w["conv_w"], w["conv_b"], w["wg"], w["bg"], w["lam"], batch, seq)
        od = _retention(p3, w["dec"], w["ret_nw"], batch, seq)
        flat = lambda o: o.reshape(tokens, GROUP_W)
        r = _post(flat(oa), flat(ob), flat(oc), flat(od), r, p, l, w["nw"], w["wo"], w["wig"], w["wiu"],
                  w["wfo"], w["wpg"], w["wpp"])
    return r.reshape(batch, seq, D_MODEL)


def kernel(x_prompt, x_sample, p_prompt, p_sample, norm_w, w_in, diff_lambda, diff_subln_w, hgrn_lb_raw,
           hgrn_norm_w, rg_conv_w, rg_conv_b, rg_w_a, rg_b_a, rg_w_x, rg_b_x, rg_lambda, ret_decay,
           ret_norm_w, w_out, w_ffn_in, w_ffn_out, w_ple_gate, w_ple_proj):
    layers = _prep_weights(norm_w, w_in, diff_lambda, diff_subln_w, hgrn_lb_raw, hgrn_norm_w, rg_conv_w,
                           rg_conv_b, rg_w_a, rg_b_a, rg_w_x, rg_b_x, rg_lambda, ret_decay, ret_norm_w,
                           w_out, w_ffn_in, w_ffn_out, w_ple_gate, w_ple_proj)
    return (_trunk(x_prompt, p_prompt, layers), _trunk(x_sample, p_sample, layers))
```
